```python
import jax, jax.numpy as jnp
from jax import lax
import numpy as np

D_MODEL = 1024
BATCH = 8
SEQ = 2048
DEPTH = 1
DEC_BATCH = 32
DEC_SEQ = 1
PAST_LEN = 16384
PAGE_SIZE = 128

D_HEAD = 64
H_RWKV = 8
H_ATT = 8
W_RWKV = H_RWKV * D_HEAD
W_ATT = H_ATT * D_HEAD
D_DECAY_LORA = 64
D_AAA_LORA = 64
D_GATE_LORA = 128
W_RWKV_IN = 3 * W_RWKV + D_DECAY_LORA + D_AAA_LORA + D_GATE_LORA
W_IN = W_RWKV_IN + 3 * W_ATT
GN_EPS = 64e-5
NORM_EPS = 1e-6
MOBA_BLOCK = 256
MOBA_TOPK = 3
Q_CHUNK = 64
ROT_DIM = D_HEAD // 4
ROPE_THETA = 500000.0
PEER_HEADS = 8
PEER_NKEYS = 128
PEER_EXPERTS = PEER_NKEYS * PEER_NKEYS
PEER_TOPK = 16
PEER_DQ = 256
PEER_CHUNK = 256

kernel_name = 'hybrid_rwkv7_moba_peer_step'


def rms_norm(x, g):
    x32 = x.astype(jnp.float32)
    y = x32 * lax.rsqrt(jnp.mean(x32 * x32, axis=-1, keepdims=True) + NORM_EPS)
    return (y * g.astype(jnp.float32)).astype(x.dtype)


def partial_rope(x, pos):
    half = ROT_DIM // 2
    inv = 1.0 / (ROPE_THETA ** (jnp.arange(half, dtype=jnp.float32) * 2.0 / ROT_DIM))
    ang = pos.astype(jnp.float32)[:, None] * inv[None, :]
    cos = jnp.cos(ang)[None, :, None, :]
    sin = jnp.sin(ang)[None, :, None, :]
    x32 = x.astype(jnp.float32)
    x1 = x32[..., :half]
    x2 = x32[..., half:ROT_DIM]
    out = jnp.concatenate([x1 * cos - x2 * sin, x2 * cos + x1 * sin, x32[..., ROT_DIM:]], axis=-1)
    return out.astype(x.dtype)


def project(x, norm_g, w_in, pos):
    b, t, _ = x.shape
    p = rms_norm(x, norm_g) @ w_in
    q, k, v = jnp.split(p[..., W_RWKV_IN:], 3, axis=-1)
    heads = lambda z: z.reshape(b, t, H_ATT, D_HEAD)
    return p[..., :W_RWKV_IN], partial_rope(heads(q), pos), partial_rope(heads(k), pos), heads(v)


def rwkv_group(p, prev0, s0, mu, w0, w_dec, a0, w_aaa, w_gate, k_k, k_a, r_k, gn_g, gn_b):
    f32 = jnp.float32
    b, t, _ = p.shape
    prev = jnp.concatenate([prev0[:, None, :].astype(p.dtype), p[:, :-1]], axis=1)
    ps = p + (prev - p) * mu
    cut = [W_RWKV, 2 * W_RWKV, 3 * W_RWKV, 3 * W_RWKV + D_DECAY_LORA, 3 * W_RWKV + D_DECAY_LORA + D_AAA_LORA]
    r, k, v, wd, ad, gd = jnp.split(ps, cut, axis=-1)
    logw = -jnp.exp(-jax.nn.softplus(-(w0 + jnp.tanh(wd) @ w_dec).astype(f32)) - 0.5)
    a = jax.nn.sigmoid((a0 + ad @ w_aaa).astype(f32))
    g = (jax.nn.sigmoid(gd) @ w_gate).astype(f32)
    hd = lambda z: z.astype(f32).reshape(b, t, H_RWKV, D_HEAD)
    r, k, v, a, decay = hd(r), hd(k), hd(v), hd(a), jnp.exp(hd(logw))
    kk = k * k_k.astype(f32).reshape(H_RWKV, D_HEAD)
    kk = kk / jnp.maximum(jnp.sqrt(jnp.sum(kk * kk, axis=-1, keepdims=True)), 1e-12)
    k = k * (1.0 + (a - 1.0) * k_a.astype(f32).reshape(H_RWKV, D_HEAD))

    def step(s, xs):
        r_t, w_t, k_t, v_t, kk_t, a_t = xs
        sa = jnp.einsum('bhvk,bhk->bhv', s, -kk_t)
        s = s * w_t[:, :, None, :] + sa[..., None] * (kk_t * a_t)[:, :, None, :] + v_t[..., None] * k_t[:, :, None, :]
        return s, jnp.einsum('bhvk,bhk->bhv', s, r_t)

    tm = lambda z: jnp.moveaxis(z, 1, 0)
    s_fin, ys = lax.scan(step, s0.astype(f32), (tm(r), tm(decay), tm(k), tm(v), tm(kk), tm(a)))
    y = jnp.moveaxis(ys, 0, 1)
    mean = jnp.mean(y, axis=-1, keepdims=True)
    var = jnp.mean(jnp.square(y - mean), axis=-1, keepdims=True)
    yn = ((y - mean) * lax.rsqrt(var + GN_EPS)).reshape(b, t, W_RWKV) * gn_g.astype(f32) + gn_b.astype(f32)
    bonus = (jnp.sum(r * k * r_k.astype(f32), axis=-1, keepdims=True) * v).reshape(b, t, W_RWKV)
    out = ((yn + bonus) * g).astype(p.dtype)
    return out, s_fin, p[:, -1]


def moba_attend(q, k_own, v_own, own_mask, k_sel, v_sel, sel_mask):
    scale = D_HEAD ** -0.5
    s_own = jnp.einsum('bqhd,bkhd->bqhk', q, k_own).astype(jnp.float32) * scale
    s_own = jnp.where(own_mask[None, :, None, :], s_own, -jnp.inf)
    if k_sel is None:
        p = jax.nn.softmax(s_own, axis=-1).astype(v_own.dtype)
        return jnp.einsum('bqhk,bkhd->bqhd', p, v_own)
    s_sel = jnp.einsum('bqhd,bqhskd->bqhsk', q, k_sel).astype(jnp.float32) * scale
    s_sel = jnp.where(sel_mask[..., None], s_sel, -jnp.inf)
    n_s = s_sel.shape[3] * s_sel.shape[4]
    scores = jnp.concatenate([s_sel.reshape(s_sel.shape[:3] + (n_s,)), s_own], axis=-1)
    p = jax.nn.softmax(scores, axis=-1).astype(v_own.dtype)
    p_sel = p[..., :n_s].reshape(s_sel.shape)
    return jnp.einsum('bqhsk,bqhskd->bqhd', p_sel, v_sel) + jnp.einsum('bqhk,bkhd->bqhd', p[..., n_s:], v_own)


def moba_prompt(q, k, v):
    b, t, h, dh = q.shape
    nb = -(-t // MOBA_BLOCK)
    tp = nb * MOBA_BLOCK
    padw = ((0, 0), (0, tp - t), (0, 0), (0, 0))
    kp = jnp.pad(k, padw)
    vp = jnp.pad(v, padw)
    n_sel = min(MOBA_TOPK, nb - 1)
    qblk = jnp.arange(t) // MOBA_BLOCK
    bi = jnp.arange(b)[:, None, None, None]
    hi = jnp.arange(h)[None, None, :, None]
    if n_sel > 0:
        kb = kp.reshape(b, nb, MOBA_BLOCK, h, dh)
        kmean = jnp.mean(kb.astype(jnp.float32), axis=2)
        gate = jnp.einsum('bthd,bnhd->bthn', q.astype(jnp.float32), kmean)
        past = jnp.arange(nb)[None, :] < qblk[:, None]
        gate = jnp.where(past[None, :, None, :], gate, -jnp.inf)
        _, sel = lax.top_k(gate, n_sel)
        sel_ok = sel < qblk[None, :, None, None]
        kbh = kb.transpose(0, 3, 1, 2, 4)
        vbh = vp.reshape(b, nb, MOBA_BLOCK, h, dh).transpose(0, 3, 1, 2, 4)

    def chunk(c):
        t0 = c * Q_CHUNK
        qc = lax.dynamic_slice_in_dim(q, t0, Q_CHUNK, axis=1)
        b0 = (t0 // MOBA_BLOCK) * MOBA_BLOCK
        k_own = lax.dynamic_slice_in_dim(kp, b0, MOBA_BLOCK, axis=1)
        v_own = lax.dynamic_slice_in_dim(vp, b0, MOBA_BLOCK, axis=1)
        own_mask = (b0 + jnp.arange(MOBA_BLOCK))[None, :] <= (t0 + jnp.arange(Q_CHUNK))[:, None]
        if n_sel > 0:
            idx = lax.dynamic_slice_in_dim(sel, t0, Q_CHUNK, axis=1)
            ok = lax.dynamic_slice_in_dim(sel_ok, t0, Q_CHUNK, axis=1)
            return moba_attend(qc, k_own, v_own, own_mask, kbh[bi, hi, idx], vbh[bi, hi, idx], ok)
        return moba_attend(qc, k_own, v_own, own_mask, None, None, None)

    out = lax.map(chunk, jnp.arange(t // Q_CHUNK))
    return out.transpose(1, 0, 2, 3, 4).reshape(b, t, h, dh)


def moba_sample(q, k, v, cache_k, cache_v, page_table):
    bd, s, h, dh = q.shape
    ppb = MOBA_BLOCK // PAGE_SIZE
    n_full = PAST_LEN // MOBA_BLOCK
    rem = PAST_LEN - n_full * MOBA_BLOCK
    n_sel = min(MOBA_TOPK, n_full)
    if rem > 0:
        pt_rem = page_table[:, n_full * ppb:]
        k_own = jnp.concatenate([cache_k[pt_rem].reshape(bd, rem, h, dh).astype(k.dtype), k], axis=1)
        v_own = jnp.concatenate([cache_v[pt_rem].reshape(bd, rem, h, dh).astype(v.dtype), v], axis=1)
    else:
        k_own, v_own = k, v
    own_mask = jnp.arange(rem + s)[None, :] <= (rem + jnp.arange(s))[:, None]
    if n_sel == 0:
        return moba_attend(q, k_own, v_own, own_mask, None, None, None)
    pt_full = page_table[:, :n_full * ppb]
    kmean = jnp.mean(cache_k[pt_full].reshape(bd, n_full, MOBA_BLOCK, h, dh).astype(jnp.float32), axis=2)
    gate = jnp.einsum('bshd,bnhd->bshn', q.astype(jnp.float32), kmean)
    _, sel = lax.top_k(gate, n_sel)
    logical = sel[..., None] * ppb + jnp.arange(ppb)
    phys = page_table[jnp.arange(bd)[:, None, None, None, None], logical]
    hi = jnp.arange(h)[None, None, :, None, None]
    k_sel = cache_k[phys, :, hi, :].reshape(bd, s, h, n_sel, MOBA_BLOCK, dh).astype(k.dtype)
    v_sel = cache_v[phys, :, hi, :].reshape(bd, s, h, n_sel, MOBA_BLOCK, dh).astype(v.dtype)
    ok = jnp.ones((bd, s, h, n_sel), dtype=bool)
    return moba_attend(q, k_own, v_own, own_mask, k_sel, v_sel, ok)


def peer_ffn(x, w_pq, sub_keys, expert_u, expert_v):
    n, d = x.shape
    q = (x @ w_pq).reshape(n, PEER_HEADS, 2, PEER_DQ // 2).astype(jnp.float32)
    s = jnp.einsum('nhpd,pkd->nhpk', q, sub_keys.astype(jnp.float32))
    sv, si = lax.top_k(s, PEER_TOPK)
    comb = (sv[:, :, 0, :, None] + sv[:, :, 1, None, :]).reshape(n, PEER_HEADS, PEER_TOPK * PEER_TOPK)
    cs, ci = lax.top_k(comb, PEER_TOPK)
    e1 = jnp.take_along_axis(si[:, :, 0], ci // PEER_TOPK, axis=-1)
    e2 = jnp.take_along_axis(si[:, :, 1], ci % PEER_TOPK, axis=-1)
    idx = e1 * PEER_NKEYS + e2
    g = jax.nn.softmax(cs, axis=-1)
    c = min(PEER_CHUNK, n)
    n_pad = -(-n // c) * c
    nc = n_pad // c
    xp = jnp.pad(x, ((0, n_pad - n), (0, 0))).reshape(nc, c, d)
    ip = jnp.pad(idx, ((0, n_pad - n), (0, 0), (0, 0))).reshape(nc, c, PEER_HEADS, PEER_TOPK)
    gp = jnp.pad(g, ((0, n_pad - n), (0, 0), (0, 0))).reshape(nc, c, PEER_HEADS, PEER_TOPK)

    def body(args):
        xc, ic, gc = args
        act = jax.nn.gelu(jnp.einsum('cd,chkd->chk', xc, expert_u[ic]).astype(jnp.float32), approximate=False)
        return jnp.einsum('chk,chkd->cd', (gc * act).astype(x.dtype), expert_v[ic])

    out = lax.map(body, (xp, ip, gp))
    return out.reshape(n_pad, d)[:n]


def merge_and_ffn(x, r_out, a_out, w_out, norm_ffn, w_pq, sub_keys, expert_u, expert_v):
    b, t, d = x.shape
    h = x + jnp.concatenate([r_out, a_out.reshape(b, t, W_ATT)], axis=-1) @ w_out
    f = peer_ffn(rms_norm(h, norm_ffn).reshape(b * t, d), w_pq, sub_keys, expert_u, expert_v)
    return h + f.reshape(b, t, d)


def setup_inputs(seed: int = 0) -> dict:
    key = jax.random.key(seed)
    ks = jax.random.split(key, 32)
    f32 = jnp.float32
    n_pages = PAST_LEN // PAGE_SIZE
    n_used = DEC_BATCH * n_pages
    n_pool = n_used + n_used // 4
    L = DEPTH
    nrm = lambda k, shape, sc: jax.random.normal(k, shape, f32) * sc
    return {
        'x_prompt': nrm(ks[0], (BATCH, SEQ, D_MODEL), 1.0),
        'x_sample': nrm(ks[1], (DEC_BATCH, DEC_SEQ, D_MODEL), 1.0),
        'cache_k': nrm(ks[2], (L, n_pool, PAGE_SIZE, H_ATT, D_HEAD), 1.0),
        'cache_v': nrm(ks[3], (L, n_pool, PAGE_SIZE, H_ATT, D_HEAD), 1.0),
        'page_table': jax.random.permutation(ks[4], n_pool)[:n_used].reshape(DEC_BATCH, n_pages).astype(jnp.int32),
        'state_wkv': nrm(ks[5], (L, DEC_BATCH, H_RWKV, D_HEAD, D_HEAD), 0.3),
        'state_shift': nrm(ks[6], (L, DEC_BATCH, W_RWKV_IN), 1.0),
        'norm_mix': 1.0 + nrm(ks[7], (L, D_MODEL), 0.02),
        'w_in': nrm(ks[8], (L, D_MODEL, W_IN), D_MODEL ** -0.5),
        'mu_shift': jax.random.uniform(ks[9], (L, W_RWKV_IN), f32),
        'w0': jax.random.uniform(ks[10], (L, W_RWKV), f32, minval=-5.0, maxval=0.0),
        'w_decay_up': nrm(ks[11], (L, D_DECAY_LORA, W_RWKV), 0.1),
        'a0': nrm(ks[12], (L, W_RWKV), 0.1),
        'w_aaa_up': nrm(ks[13], (L, D_AAA_LORA, W_RWKV), 0.1),
        'w_gate_up': nrm(ks[14], (L, D_GATE_LORA, W_RWKV), D_GATE_LORA ** -0.5),
        'k_k': 0.85 + nrm(ks[15], (L, W_RWKV), 0.02),
        'k_a': 1.0 + nrm(ks[16], (L, W_RWKV), 0.02),
        'r_k': nrm(ks[17], (L, H_RWKV, D_HEAD), 0.1),
        'gn_gain': 1.0 + nrm(ks[18], (L, W_RWKV), 0.02),
        'gn_bias': nrm(ks[19], (L, W_RWKV), 0.02),
        'w_out': nrm(ks[20], (L, W_RWKV + W_ATT, D_MODEL), (W_RWKV + W_ATT) ** -0.5),
        'norm_ffn': 1.0 + nrm(ks[21], (L, D_MODEL), 0.02),
        'w_pq': nrm(ks[22], (L, D_MODEL, PEER_HEADS * PEER_DQ), D_MODEL ** -0.5),
        'peer_sub_keys': nrm(ks[23], (L, 2, PEER_NKEYS, PEER_DQ // 2), (PEER_DQ // 2) ** -0.5),
        'expert_u': nrm(ks[24], (L, PEER_EXPERTS, D_MODEL), D_MODEL ** -0.5),
        'expert_v': nrm(ks[25], (L, PEER_EXPERTS, D_MODEL), 0.3),
        'norm_final': 1.0 + nrm(ks[26], (D_MODEL,), 0.02),
    }


def reference(x_prompt, x_sample, cache_k, cache_v, page_table, state_wkv, state_shift, norm_mix, w_in, mu_shift, w0, w_decay_up, a0, w_aaa_up, w_gate_up, k_k, k_a, r_k, gn_gain, gn_bias, w_out, norm_ffn, w_pq, peer_sub_keys, expert_u, expert_v, norm_final):
    pos_p = jnp.arange(x_prompt.shape[1], dtype=jnp.int32)
    pos_s = PAST_LEN + jnp.arange(x_sample.shape[1], dtype=jnp.int32)
    xp, xs = x_prompt, x_sample
    kp_l, vp_l, sp_l, hp_l = [], [], [], []
    ks_l, vs_l, ss_l, hs_l = [], [], [], []
    for l in range(DEPTH):
        rw = (mu_shift[l], w0[l], w_decay_up[l], a0[l], w_aaa_up[l], w_gate_up[l], k_k[l], k_a[l], r_k[l], gn_gain[l], gn_bias[l])
        ffn = (w_out[l], norm_ffn[l], w_pq[l], peer_sub_keys[l], expert_u[l], expert_v[l])
        pr, q, k, v = project(xp, norm_mix[l], w_in[l], pos_p)
        bsz = xp.shape[0]
        r_out, s_new, sh_new = rwkv_group(pr, jnp.zeros((bsz, W_RWKV_IN), pr.dtype), jnp.zeros((bsz, H_RWKV, D_HEAD, D_HEAD), jnp.float32), *rw)
        a_out = moba_prompt(q, k, v)
        xp = merge_and_ffn(xp, r_out, a_out, *ffn)
        kp_l.append(k); vp_l.append(v); sp_l.append(s_new.astype(state_wkv.dtype)); hp_l.append(sh_new.astype(state_shift.dtype))
        pr, q, k, v = project(xs, norm_mix[l], w_in[l], pos_s)
        r_out, s_new, sh_new = rwkv_group(pr, state_shift[l], state_wkv[l], *rw)
        a_out = moba_sample(q, k, v, cache_k[l], cache_v[l], page_table)
        xs = merge_and_ffn(xs, r_out, a_out, *ffn)
        ks_l.append(k); vs_l.append(v); ss_l.append(s_new.astype(state_wkv.dtype)); hs_l.append(sh_new.astype(state_shift.dtype))
    y_prompt = rms_norm(xp, norm_final)
    y_sample = rms_norm(xs, norm_final)
    return (y_prompt, y_sample, jnp.stack(kp_l), jnp.stack(vp_l), jnp.stack(sp_l), jnp.stack(hp_l), jnp.stack(ks_l), jnp.stack(vs_l), jnp.stack(ss_l), jnp.stack(hs_l))
```

```python
import functools

import jax
import jax.numpy as jnp
from jax import lax
from jax.experimental import pallas as pl
from jax.experimental.pallas import tpu as pltpu

D_MODEL = 1024
DEPTH = 1
PAST_LEN = 16384
PAGE_SIZE = 128
D_HEAD = 64
H_RWKV = 8
H_ATT = 8
W_RWKV = H_RWKV * D_HEAD
W_ATT = H_ATT * D_HEAD
D_DECAY_LORA = 64
D_AAA_LORA = 64
D_GATE_LORA = 128
W_RWKV_IN = 3 * W_RWKV + D_DECAY_LORA + D_AAA_LORA + D_GATE_LORA
W_IN = W_RWKV_IN + 3 * W_ATT
GN_EPS = 64e-5
NORM_EPS = 1e-6
MOBA_BLOCK = 256
MOBA_TOPK = 3
Q_CHUNK = 64
ROT_DIM = D_HEAD // 4
ROPE_THETA = 500000.0
PEER_HEADS = 8
PEER_NKEYS = 128
PEER_TOPK = 16
PEER_DQ = 256
PEER_CHUNK = 256


def rms_norm(x, g):
    x32 = x.astype(jnp.float32)
    y = x32 * lax.rsqrt(jnp.mean(x32 * x32, axis=-1, keepdims=True) + NORM_EPS)
    return (y * g.astype(jnp.float32)).astype(x.dtype)


def _final_norm_kernel(x_ref, g_ref, o_ref):
    x = x_ref[...]
    y = x * lax.rsqrt(jnp.mean(x * x, axis=-1, keepdims=True) + NORM_EPS)
    o_ref[...] = y * g_ref[...]


def final_norm(x, g):
    shape = x.shape
    x2 = x.reshape(-1, shape[-1])
    n, d = x2.shape
    tm = min(n, 512)
    out = pl.pallas_call(
        _final_norm_kernel,
        grid=(n // tm,),
        in_specs=[pl.BlockSpec((tm, d), lambda i: (i, 0)), pl.BlockSpec((1, d), lambda i: (0, 0))],
        out_specs=pl.BlockSpec((tm, d), lambda i: (i, 0)),
        out_shape=jax.ShapeDtypeStruct((n, d), x.dtype),
    )(x2, g.reshape(1, d))
    return out.reshape(shape)


def partial_rope(x, pos):
    half = ROT_DIM // 2
    inv = 1.0 / (ROPE_THETA ** (jnp.arange(half, dtype=jnp.float32) * 2.0 / ROT_DIM))
    ang = pos.astype(jnp.float32)[:, None] * inv[None, :]
    cos = jnp.cos(ang)[None, :, None, :]
    sin = jnp.sin(ang)[None, :, None, :]
    x32 = x.astype(jnp.float32)
    x1 = x32[..., :half]
    x2 = x32[..., half:ROT_DIM]
    out = jnp.concatenate([x1 * cos - x2 * sin, x2 * cos + x1 * sin, x32[..., ROT_DIM:]], axis=-1)
    return out.astype(x.dtype)


def project(x, norm_g, w_in, pos):
    b, t, _ = x.shape
    p = rms_norm(x, norm_g) @ w_in
    q, k, v = jnp.split(p[..., W_RWKV_IN:], 3, axis=-1)
    heads = lambda z: z.reshape(b, t, H_ATT, D_HEAD)
    return p[..., :W_RWKV_IN], partial_rope(heads(q), pos), partial_rope(heads(k), pos), heads(v)


def rwkv_group(p, prev0, s0, mu, w0, w_dec, a0, w_aaa, w_gate, k_k, k_a, r_k, gn_g, gn_b):
    f32 = jnp.float32
    b, t, _ = p.shape
    prev = jnp.concatenate([prev0[:, None, :].astype(p.dtype), p[:, :-1]], axis=1)
    ps = p + (prev - p) * mu
    cut = [W_RWKV, 2 * W_RWKV, 3 * W_RWKV, 3 * W_RWKV + D_DECAY_LORA, 3 * W_RWKV + D_DECAY_LORA + D_AAA_LORA]
    r, k, v, wd, ad, gd = jnp.split(ps, cut, axis=-1)
    logw = -jnp.exp(-jax.nn.softplus(-(w0 + jnp.tanh(wd) @ w_dec).astype(f32)) - 0.5)
    a = jax.nn.sigmoid((a0 + ad @ w_aaa).astype(f32))
    g = (jax.nn.sigmoid(gd) @ w_gate).astype(f32)
    hd = lambda z: z.astype(f32).reshape(b, t, H_RWKV, D_HEAD)
    r, k, v, a, decay = hd(r), hd(k), hd(v), hd(a), jnp.exp(hd(logw))
    kk = k * k_k.astype(f32).reshape(H_RWKV, D_HEAD)
    kk = kk / jnp.maximum(jnp.sqrt(jnp.sum(kk * kk, axis=-1, keepdims=True)), 1e-12)
    k = k * (1.0 + (a - 1.0) * k_a.astype(f32).reshape(H_RWKV, D_HEAD))

    def step(s, xs):
        r_t, w_t, k_t, v_t, kk_t, a_t = xs
        sa = jnp.einsum('bhvk,bhk->bhv', s, -kk_t)
        s = s * w_t[:, :, None, :] + sa[..., None] * (kk_t * a_t)[:, :, None, :] + v_t[..., None] * k_t[:, :, None, :]
        return s, jnp.einsum('bhvk,bhk->bhv', s, r_t)

    tm = lambda z: jnp.moveaxis(z, 1, 0)
    s_fin, ys = lax.scan(step, s0.astype(f32), (tm(r), tm(decay), tm(k), tm(v), tm(kk), tm(a)))
    y = jnp.moveaxis(ys, 0, 1)
    mean = jnp.mean(y, axis=-1, keepdims=True)
    var = jnp.mean(jnp.square(y - mean), axis=-1, keepdims=True)
    yn = ((y - mean) * lax.rsqrt(var + GN_EPS)).reshape(b, t, W_RWKV) * gn_g.astype(f32) + gn_b.astype(f32)
    bonus = (jnp.sum(r * k * r_k.astype(f32), axis=-1, keepdims=True) * v).reshape(b, t, W_RWKV)
    out = ((yn + bonus) * g).astype(p.dtype)
    return out, s_fin, p[:, -1]


def moba_attend(q, k_own, v_own, own_mask, k_sel, v_sel, sel_mask):
    scale = D_HEAD ** -0.5
    s_own = jnp.einsum('bqhd,bkhd->bqhk', q, k_own).astype(jnp.float32) * scale
    s_own = jnp.where(own_mask[None, :, None, :], s_own, -jnp.inf)
    if k_sel is None:
        p = jax.nn.softmax(s_own, axis=-1).astype(v_own.dtype)
        return jnp.einsum('bqhk,bkhd->bqhd', p, v_own)
    s_sel = jnp.einsum('bqhd,bqhskd->bqhsk', q, k_sel).astype(jnp.float32) * scale
    s_sel = jnp.where(sel_mask[..., None], s_sel, -jnp.inf)
    n_s = s_sel.shape[3] * s_sel.shape[4]
    scores = jnp.concatenate([s_sel.reshape(s_sel.shape[:3] + (n_s,)), s_own], axis=-1)
    p = jax.nn.softmax(scores, axis=-1).astype(v_own.dtype)
    p_sel = p[..., :n_s].reshape(s_sel.shape)
    return jnp.einsum('bqhsk,bqhskd->bqhd', p_sel, v_sel) + jnp.einsum('bqhk,bkhd->bqhd', p[..., n_s:], v_own)


def moba_prompt(q, k, v):
    b, t, h, dh = q.shape
    nb = -(-t // MOBA_BLOCK)
    tp = nb * MOBA_BLOCK
    padw = ((0, 0), (0, tp - t), (0, 0), (0, 0))
    kp = jnp.pad(k, padw)
    vp = jnp.pad(v, padw)
    n_sel = min(MOBA_TOPK, nb - 1)
    qblk = jnp.arange(t) // MOBA_BLOCK
    bi = jnp.arange(b)[:, None, None, None]
    hi = jnp.arange(h)[None, None, :, None]
    if n_sel > 0:
        kb = kp.reshape(b, nb, MOBA_BLOCK, h, dh)
        kmean = jnp.mean(kb.astype(jnp.float32), axis=2)
        gate = jnp.einsum('bthd,bnhd->bthn', q.astype(jnp.float32), kmean)
        past = jnp.arange(nb)[None, :] < qblk[:, None]
        gate = jnp.where(past[None, :, None, :], gate, -jnp.inf)
        _, sel = lax.top_k(gate, n_sel)
        sel_ok = sel < qblk[None, :, None, None]
        kbh = kb.transpose(0, 3, 1, 2, 4)
        vbh = vp.reshape(b, nb, MOBA_BLOCK, h, dh).transpose(0, 3, 1, 2, 4)

    def chunk(c):
        t0 = c * Q_CHUNK
        qc = lax.dynamic_slice_in_dim(q, t0, Q_CHUNK, axis=1)
        b0 = (t0 // MOBA_BLOCK) * MOBA_BLOCK
        k_own = lax.dynamic_slice_in_dim(kp, b0, MOBA_BLOCK, axis=1)
        v_own = lax.dynamic_slice_in_dim(vp, b0, MOBA_BLOCK, axis=1)
        own_mask = (b0 + jnp.arange(MOBA_BLOCK))[None, :] <= (t0 + jnp.arange(Q_CHUNK))[:, None]
        if n_sel > 0:
            idx = lax.dynamic_slice_in_dim(sel, t0, Q_CHUNK, axis=1)
            ok = lax.dynamic_slice_in_dim(sel_ok, t0, Q_CHUNK, axis=1)
            return moba_attend(qc, k_own, v_own, own_mask, kbh[bi, hi, idx], vbh[bi, hi, idx], ok)
        return moba_attend(qc, k_own, v_own, own_mask, None, None, None)

    out = lax.map(chunk, jnp.arange(t // Q_CHUNK))
    return out.transpose(1, 0, 2, 3, 4).reshape(b, t, h, dh)


def moba_sample(q, k, v, cache_k, cache_v, page_table):
    bd, s, h, dh = q.shape
    ppb = MOBA_BLOCK // PAGE_SIZE
    n_full = PAST_LEN // MOBA_BLOCK
    rem = PAST_LEN - n_full * MOBA_BLOCK
    n_sel = min(MOBA_TOPK, n_full)
    k_own, v_own = k, v
    own_mask = jnp.arange(rem + s)[None, :] <= (rem + jnp.arange(s))[:, None]
    pt_full = page_table[:, :n_full * ppb]
    kmean = jnp.mean(cache_k[pt_full].reshape(bd, n_full, MOBA_BLOCK, h, dh).astype(jnp.float32), axis=2)
    gate = jnp.einsum('bshd,bnhd->bshn', q.astype(jnp.float32), kmean)
    _, sel = lax.top_k(gate, n_sel)
    logical = sel[..., None] * ppb + jnp.arange(ppb)
    phys = page_table[jnp.arange(bd)[:, None, None, None, None], logical]
    hi = jnp.arange(h)[None, None, :, None, None]
    k_sel = cache_k[phys, :, hi, :].reshape(bd, s, h, n_sel, MOBA_BLOCK, dh).astype(k.dtype)
    v_sel = cache_v[phys, :, hi, :].reshape(bd, s, h, n_sel, MOBA_BLOCK, dh).astype(v.dtype)
    ok = jnp.ones((bd, s, h, n_sel), dtype=bool)
    return moba_attend(q, k_own, v_own, own_mask, k_sel, v_sel, ok)


def peer_ffn(x, w_pq, sub_keys, expert_u, expert_v):
    n, d = x.shape
    q = (x @ w_pq).reshape(n, PEER_HEADS, 2, PEER_DQ // 2).astype(jnp.float32)
    s = jnp.einsum('nhpd,pkd->nhpk', q, sub_keys.astype(jnp.float32))
    sv, si = lax.top_k(s, PEER_TOPK)
    comb = (sv[:, :, 0, :, None] + sv[:, :, 1, None, :]).reshape(n, PEER_HEADS, PEER_TOPK * PEER_TOPK)
    cs, ci = lax.top_k(comb, PEER_TOPK)
    e1 = jnp.take_along_axis(si[:, :, 0], ci // PEER_TOPK, axis=-1)
    e2 = jnp.take_along_axis(si[:, :, 1], ci % PEER_TOPK, axis=-1)
    idx = e1 * PEER_NKEYS + e2
    g = jax.nn.softmax(cs, axis=-1)
    c = min(PEER_CHUNK, n)
    n_pad = -(-n // c) * c
    nc = n_pad // c
    xp = jnp.pad(x, ((0, n_pad - n), (0, 0))).reshape(nc, c, d)
    ip = jnp.pad(idx, ((0, n_pad - n), (0, 0), (0, 0))).reshape(nc, c, PEER_HEADS, PEER_TOPK)
    gp = jnp.pad(g, ((0, n_pad - n), (0, 0), (0, 0))).reshape(nc, c, PEER_HEADS, PEER_TOPK)

    def body(args):
        xc, ic, gc = args
        act = jax.nn.gelu(jnp.einsum('cd,chkd->chk', xc, expert_u[ic]).astype(jnp.float32), approximate=False)
        return jnp.einsum('chk,chkd->cd', (gc * act).astype(x.dtype), expert_v[ic])

    out = lax.map(body, (xp, ip, gp))
    return out.reshape(n_pad, d)[:n]


def merge_and_ffn(x, r_out, a_out, w_out, norm_ffn, w_pq, sub_keys, expert_u, expert_v):
    b, t, d = x.shape
    h = x + jnp.concatenate([r_out, a_out.reshape(b, t, W_ATT)], axis=-1) @ w_out
    f = peer_ffn(rms_norm(h, norm_ffn).reshape(b * t, d), w_pq, sub_keys, expert_u, expert_v)
    return h + f.reshape(b, t, d)


def kernel(x_prompt, x_sample, cache_k, cache_v, page_table, state_wkv, state_shift, norm_mix, w_in, mu_shift, w0, w_decay_up, a0, w_aaa_up, w_gate_up, k_k, k_a, r_k, gn_gain, gn_bias, w_out, norm_ffn, w_pq, peer_sub_keys, expert_u, expert_v, norm_final):
    pos_p = jnp.arange(x_prompt.shape[1], dtype=jnp.int32)
    pos_s = PAST_LEN + jnp.arange(x_sample.shape[1], dtype=jnp.int32)
    xp, xs = x_prompt, x_sample
    l = 0
    rw = (mu_shift[l], w0[l], w_decay_up[l], a0[l], w_aaa_up[l], w_gate_up[l], k_k[l], k_a[l], r_k[l], gn_gain[l], gn_bias[l])
    ffn = (w_out[l], norm_ffn[l], w_pq[l], peer_sub_keys[l], expert_u[l], expert_v[l])
    pr, q, k, v = project(xp, norm_mix[l], w_in[l], pos_p)
    bsz = xp.shape[0]
    r_out, s_new, sh_new = rwkv_group(pr, jnp.zeros((bsz, W_RWKV_IN), pr.dtype), jnp.zeros((bsz, H_RWKV, D_HEAD, D_HEAD), jnp.float32), *rw)
    a_out = moba_prompt(q, k, v)
    xp = merge_and_ffn(xp, r_out, a_out, *ffn)
    kp_, vp_, sp_, hp_ = k[None], v[None], s_new[None], sh_new[None]
    pr, q, k, v = project(xs, norm_mix[l], w_in[l], pos_s)
    r_out, s_new, sh_new = rwkv_group(pr, state_shift[l], state_wkv[l], *rw)
    a_out = moba_sample(q, k, v, cache_k[l], cache_v[l], page_table)
    xs = merge_and_ffn(xs, r_out, a_out, *ffn)
    y_prompt = final_norm(xp, norm_final)
    y_sample = final_norm(xs, norm_final)
    return (y_prompt, y_sample, kp_, vp_, sp_, hp_, k[None], v[None], s_new[None], sh_new[None])
```

```python
import functools
import math

import jax
import jax.numpy as jnp
from jax import lax
from jax.experimental import pallas as pl
from jax.experimental.pallas import tpu as pltpu

D_MODEL = 1024
DEPTH = 1
PAST_LEN = 16384
PAGE_SIZE = 128
D_HEAD = 64
H_RWKV = 8
H_ATT = 8
W_RWKV = H_RWKV * D_HEAD
W_ATT = H_ATT * D_HEAD
D_DECAY_LORA = 64
D_AAA_LORA = 64
D_GATE_LORA = 128
W_RWKV_IN = 3 * W_RWKV + D_DECAY_LORA + D_AAA_LORA + D_GATE_LORA
W_IN = W_RWKV_IN + 3 * W_ATT
GN_EPS = 64e-5
NORM_EPS = 1e-6
MOBA_BLOCK = 256
MOBA_TOPK = 3
Q_CHUNK = 64
ROT_DIM = D_HEAD // 4
ROPE_THETA = 500000.0
PEER_HEADS = 8
PEER_NKEYS = 128
PEER_TOPK = 16
PEER_DQ = 256
PEER_EXPERTS = PEER_NKEYS * PEER_NKEYS
PEER_TM = 512
PEER_TM_SMALL = 256
PEER_TE = 1024
NEG_INF = float("-inf")
VMEM_LIMIT = 56 * 1024 * 1024
LANES = 128
NT_DIMS = (((1,), (1,)), ((), ()))


def rms_norm(x, g):
    x32 = x.astype(jnp.float32)
    y = x32 * lax.rsqrt(jnp.mean(x32 * x32, axis=-1, keepdims=True) + NORM_EPS)
    return (y * g.astype(jnp.float32)).astype(x.dtype)


def _final_norm_kernel(x_ref, g_ref, o_ref):
    x = x_ref[...]
    y = x * lax.rsqrt(jnp.mean(x * x, axis=-1, keepdims=True) + NORM_EPS)
    o_ref[...] = y * g_ref[...]


def final_norm(x, g):
    shape = x.shape
    x2 = x.reshape(-1, shape[-1])
    n, d = x2.shape
    tm = min(n, 512)
    out = pl.pallas_call(
        _final_norm_kernel,
        grid=(n // tm,),
        in_specs=[pl.BlockSpec((tm, d), lambda i: (i, 0)), pl.BlockSpec((1, d), lambda i: (0, 0))],
        out_specs=pl.BlockSpec((tm, d), lambda i: (i, 0)),
        out_shape=jax.ShapeDtypeStruct((n, d), x.dtype),
    )(x2, g.reshape(1, d))
    return out.reshape(shape)


def partial_rope(x, pos):
    half = ROT_DIM // 2
    inv = 1.0 / (ROPE_THETA ** (jnp.arange(half, dtype=jnp.float32) * 2.0 / ROT_DIM))
    ang = pos.astype(jnp.float32)[:, None] * inv[None, :]
    cos = jnp.cos(ang)[None, :, None, :]
    sin = jnp.sin(ang)[None, :, None, :]
    x32 = x.astype(jnp.float32)
    x1 = x32[..., :half]
    x2 = x32[..., half:ROT_DIM]
    out = jnp.concatenate([x1 * cos - x2 * sin, x2 * cos + x1 * sin, x32[..., ROT_DIM:]], axis=-1)
    return out.astype(x.dtype)


def project(x, norm_g, w_in, pos):
    b, t, _ = x.shape
    p = rms_norm(x, norm_g) @ w_in
    q, k, v = jnp.split(p[..., W_RWKV_IN:], 3, axis=-1)
    heads = lambda z: z.reshape(b, t, H_ATT, D_HEAD)
    return p[..., :W_RWKV_IN], partial_rope(heads(q), pos), partial_rope(heads(k), pos), heads(v)


def rwkv_group(p, prev0, s0, mu, w0, w_dec, a0, w_aaa, w_gate, k_k, k_a, r_k, gn_g, gn_b):
    f32 = jnp.float32
    b, t, _ = p.shape
    prev = jnp.concatenate([prev0[:, None, :].astype(p.dtype), p[:, :-1]], axis=1)
    ps = p + (prev - p) * mu
    cut = [W_RWKV, 2 * W_RWKV, 3 * W_RWKV, 3 * W_RWKV + D_DECAY_LORA, 3 * W_RWKV + D_DECAY_LORA + D_AAA_LORA]
    r, k, v, wd, ad, gd = jnp.split(ps, cut, axis=-1)
    logw = -jnp.exp(-jax.nn.softplus(-(w0 + jnp.tanh(wd) @ w_dec).astype(f32)) - 0.5)
    a = jax.nn.sigmoid((a0 + ad @ w_aaa).astype(f32))
    g = (jax.nn.sigmoid(gd) @ w_gate).astype(f32)
    hd = lambda z: z.astype(f32).reshape(b, t, H_RWKV, D_HEAD)
    r, k, v, a, decay = hd(r), hd(k), hd(v), hd(a), jnp.exp(hd(logw))
    kk = k * k_k.astype(f32).reshape(H_RWKV, D_HEAD)
    kk = kk / jnp.maximum(jnp.sqrt(jnp.sum(kk * kk, axis=-1, keepdims=True)), 1e-12)
    k = k * (1.0 + (a - 1.0) * k_a.astype(f32).reshape(H_RWKV, D_HEAD))

    def step(s, xs):
        r_t, w_t, k_t, v_t, kk_t, a_t = xs
        sa = jnp.einsum('bhvk,bhk->bhv', s, -kk_t)
        s = s * w_t[:, :, None, :] + sa[..., None] * (kk_t * a_t)[:, :, None, :] + v_t[..., None] * k_t[:, :, None, :]
        return s, jnp.einsum('bhvk,bhk->bhv', s, r_t)

    tm = lambda z: jnp.moveaxis(z, 1, 0)
    s_fin, ys = lax.scan(step, s0.astype(f32), (tm(r), tm(decay), tm(k), tm(v), tm(kk), tm(a)))
    y = jnp.moveaxis(ys, 0, 1)
    mean = jnp.mean(y, axis=-1, keepdims=True)
    var = jnp.mean(jnp.square(y - mean), axis=-1, keepdims=True)
    yn = ((y - mean) * lax.rsqrt(var + GN_EPS)).reshape(b, t, W_RWKV) * gn_g.astype(f32) + gn_b.astype(f32)
    bonus = (jnp.sum(r * k * r_k.astype(f32), axis=-1, keepdims=True) * v).reshape(b, t, W_RWKV)
    out = ((yn + bonus) * g).astype(p.dtype)
    return out, s_fin, p[:, -1]


def moba_attend(q, k_own, v_own, own_mask, k_sel, v_sel, sel_mask):
    scale = D_HEAD ** -0.5
    s_own = jnp.einsum('bqhd,bkhd->bqhk', q, k_own).astype(jnp.float32) * scale
    s_own = jnp.where(own_mask[None, :, None, :], s_own, -jnp.inf)
    if k_sel is None:
        p = jax.nn.softmax(s_own, axis=-1).astype(v_own.dtype)
        return jnp.einsum('bqhk,bkhd->bqhd', p, v_own)
    s_sel = jnp.einsum('bqhd,bqhskd->bqhsk', q, k_sel).astype(jnp.float32) * scale
    s_sel = jnp.where(sel_mask[..., None], s_sel, -jnp.inf)
    n_s = s_sel.shape[3] * s_sel.shape[4]
    scores = jnp.concatenate([s_sel.reshape(s_sel.shape[:3] + (n_s,)), s_own], axis=-1)
    p = jax.nn.softmax(scores, axis=-1).astype(v_own.dtype)
    p_sel = p[..., :n_s].reshape(s_sel.shape)
    return jnp.einsum('bqhsk,bqhskd->bqhd', p_sel, v_sel) + jnp.einsum('bqhk,bkhd->bqhd', p[..., n_s:], v_own)


def _moba_prompt_kernel(q_ref, k_ref, v_ref, o_ref, vt_ref, kb_ref, sel_ref, m_ref, l_ref, acc_ref):
    qi = pl.program_id(2)
    t_len = k_ref.shape[1]
    nb = t_len // MOBA_BLOCK
    scale = D_HEAD ** -0.5

    @pl.when(qi == 0)
    def _():
        vt_ref[...] = v_ref[0].T.astype(jnp.bfloat16)
        kb_ref[...] = k_ref[0].astype(jnp.bfloat16)

    q2 = q_ref[0]
    lane = lax.broadcasted_iota(jnp.int32, q2.shape, 1)
    kmean = jnp.mean(k_ref[0].reshape(nb, MOBA_BLOCK, LANES), axis=1)
    blk_row = lax.broadcasted_iota(jnp.int32, (nb, MOBA_BLOCK), 0)
    key_row = lax.broadcasted_iota(jnp.int32, (MOBA_BLOCK, MOBA_BLOCK), 0)
    qry_col = lax.broadcasted_iota(jnp.int32, (MOBA_BLOCK, MOBA_BLOCK), 1)
    qms = []
    for s in range(2):
        qm = jnp.where((lane // D_HEAD) == s, q2, 0.0)
        qms.append(qm.astype(jnp.bfloat16))
        gate = lax.dot_general(kmean, qm, NT_DIMS, precision=lax.Precision.HIGHEST,
                               preferred_element_type=jnp.float32)
        rank = jnp.zeros(gate.shape, jnp.float32)
        for jp in range(nb):
            g_jp = gate[jp:jp + 1, :]
            ahead = (g_jp > gate) | ((g_jp == gate) & (jp < blk_row))
            rank = rank + jnp.where(ahead & (jp < qi), 1.0, 0.0)
        sel_ref[s] = jnp.where((rank < float(MOBA_TOPK)) & (blk_row < qi), 1.0, 0.0)

        own = pl.ds(pl.multiple_of(qi * MOBA_BLOCK, MOBA_BLOCK), MOBA_BLOCK)
        st = lax.dot_general(kb_ref[own, :], qms[s], NT_DIMS, preferred_element_type=jnp.float32) * scale
        st = jnp.where(key_row <= qry_col, st, NEG_INF)
        m0 = jnp.max(st, axis=0, keepdims=True)
        p = jnp.exp(st - m0)
        m_ref[s] = m0
        l_ref[s] = jnp.sum(p, axis=0, keepdims=True)
        acc_ref[s] = jnp.dot(vt_ref[:, own], p.astype(jnp.bfloat16), preferred_element_type=jnp.float32)

    def past_block(j, carry):
        blk = pl.ds(pl.multiple_of(j * MOBA_BLOCK, MOBA_BLOCK), MOBA_BLOCK)
        k_j = kb_ref[blk, :]
        vt_j = vt_ref[:, blk]
        for s in range(2):
            st = lax.dot_general(k_j, qms[s], NT_DIMS, preferred_element_type=jnp.float32) * scale
            st = jnp.where(sel_ref[s, pl.ds(j, 1), :] > 0.0, st, NEG_INF)
            m_old = m_ref[s]
            m_new = jnp.maximum(m_old, jnp.max(st, axis=0, keepdims=True))
            alpha = jnp.exp(m_old - m_new)
            p = jnp.exp(st - m_new)
            m_ref[s] = m_new
            l_ref[s] = alpha * l_ref[s] + jnp.sum(p, axis=0, keepdims=True)
            acc_ref[s] = alpha * acc_ref[s] + jnp.dot(vt_j, p.astype(jnp.bfloat16), preferred_element_type=jnp.float32)
        return carry

    lax.fori_loop(0, qi, past_block, 0)

    row = lax.broadcasted_iota(jnp.int32, (LANES, MOBA_BLOCK), 0)
    out_t = jnp.where((row // D_HEAD) == 0, acc_ref[0] / l_ref[0], acc_ref[1] / l_ref[1])
    o_ref[0] = out_t.T


def moba_prompt(q, k, v):
    b, t, w = q.shape
    nb = t // MOBA_BLOCK
    return pl.pallas_call(
        _moba_prompt_kernel,
        grid=(b, w // LANES, nb),
        in_specs=[
            pl.BlockSpec((1, MOBA_BLOCK, LANES), lambda bi, hp, qi: (bi, qi, hp)),
            pl.BlockSpec((1, t, LANES), lambda bi, hp, qi: (bi, 0, hp)),
            pl.BlockSpec((1, t, LANES), lambda bi, hp, qi: (bi, 0, hp)),
        ],
        out_specs=pl.BlockSpec((1, MOBA_BLOCK, LANES), lambda bi, hp, qi: (bi, qi, hp)),
        out_shape=jax.ShapeDtypeStruct((b, t, w), jnp.float32),
        scratch_shapes=[
            pltpu.VMEM((LANES, t), jnp.bfloat16),
            pltpu.VMEM((t, LANES), jnp.bfloat16),
            pltpu.VMEM((2, nb, MOBA_BLOCK), jnp.float32),
            pltpu.VMEM((2, 1, MOBA_BLOCK), jnp.float32),
            pltpu.VMEM((2, 1, MOBA_BLOCK), jnp.float32),
            pltpu.VMEM((2, LANES, MOBA_BLOCK), jnp.float32),
        ],
        compiler_params=pltpu.CompilerParams(dimension_semantics=("parallel", "parallel", "arbitrary"),
                                             vmem_limit_bytes=VMEM_LIMIT),
        name="moba_prompt",
    )(q, k, v)


def moba_sample(q, k, v, cache_k, cache_v, page_table):
    bd, s, h, dh = q.shape
    ppb = MOBA_BLOCK // PAGE_SIZE
    n_full = PAST_LEN // MOBA_BLOCK
    rem = PAST_LEN - n_full * MOBA_BLOCK
    n_sel = min(MOBA_TOPK, n_full)
    k_own, v_own = k, v
    own_mask = jnp.arange(rem + s)[None, :] <= (rem + jnp.arange(s))[:, None]
    pt_full = page_table[:, :n_full * ppb]
    kmean = jnp.mean(cache_k[pt_full].reshape(bd, n_full, MOBA_BLOCK, h, dh).astype(jnp.float32), axis=2)
    gate = jnp.einsum('bshd,bnhd->bshn', q.astype(jnp.float32), kmean)
    _, sel = lax.top_k(gate, n_sel)
    logical = sel[..., None] * ppb + jnp.arange(ppb)
    phys = page_table[jnp.arange(bd)[:, None, None, None, None], logical]
    hi = jnp.arange(h)[None, None, :, None, None]
    k_sel = cache_k[phys, :, hi, :].reshape(bd, s, h, n_sel, MOBA_BLOCK, dh).astype(k.dtype)
    v_sel = cache_v[phys, :, hi, :].reshape(bd, s, h, n_sel, MOBA_BLOCK, dh).astype(v.dtype)
    ok = jnp.ones((bd, s, h, n_sel), dtype=bool)
    return moba_attend(q, k_own, v_own, own_mask, k_sel, v_sel, ok)


PEER_CANDS = [(a, b) for a in range(PEER_TOPK) for b in range(PEER_TOPK) if (a + 1) * (b + 1) <= PEER_TOPK]
PEER_NCAND = -(-len(PEER_CANDS) // 8) * 8


def _extract_topk(x, n_rows, k):
    rows = lax.broadcasted_iota(jnp.int32, x.shape, 0).astype(jnp.float32)
    vals = []
    for _ in range(k):
        m = jnp.max(x, axis=0, keepdims=True)
        first = jnp.min(jnp.where(x == m, rows, float(n_rows)), axis=0, keepdims=True)
        x = jnp.where(rows == first, NEG_INF, x)
        vals.append(m)
    return x, vals


def _peer_route_kernel(xn_ref, wpq_ref, keys_ref, s1_ref, s2_ref, e1_ref, e2_ref, thr_ref, sv_ref, comb_ref):
    q = jnp.dot(xn_ref[...], wpq_ref[...], preferred_element_type=jnp.float32)
    half = PEER_DQ // 2
    for h in range(PEER_HEADS):
        masked = []
        for p in range(2):
            qs = q[:, (2 * h + p) * half:(2 * h + p + 1) * half].astype(jnp.bfloat16)
            st = lax.dot_general(keys_ref[p], qs, (((1,), (1,)), ((), ())),
                                 preferred_element_type=jnp.float32)
            rem, vals = _extract_topk(st, PEER_NKEYS, PEER_TOPK)
            for i, v in enumerate(vals):
                sv_ref[p, i:i + 1, :] = v
            masked.append(jnp.where(rem == NEG_INF, st, NEG_INF))
        comb_ref[...] = jnp.full(comb_ref.shape, NEG_INF, jnp.float32)
        for c, (a, b) in enumerate(PEER_CANDS):
            comb_ref[c:c + 1, :] = sv_ref[0, a:a + 1, :] + sv_ref[1, b:b + 1, :]
        _, cvals = _extract_topk(comb_ref[...], PEER_NCAND, PEER_TOPK)
        cmax = cvals[0]
        z = jnp.zeros_like(cmax)
        for v in cvals:
            z = z + jnp.exp(v - cmax)
        thr_ref[h:h + 1, :] = cvals[-1]
        s1_ref[h] = masked[0]
        s2_ref[h] = masked[1]
        e1_ref[h] = jnp.exp(masked[0] - sv_ref[0, 0:1, :])
        e2_ref[h] = jnp.exp(masked[1] - sv_ref[1, 0:1, :]) / z


def peer_route(xn_bf16, wpq_bf16, keys_bf16, tm):
    n = xn_bf16.shape[0]
    tab = jax.ShapeDtypeStruct((PEER_HEADS, PEER_NKEYS, n), jnp.float32)
    tab_spec = pl.BlockSpec((PEER_HEADS, PEER_NKEYS, tm), lambda i: (0, 0, i))
    return pl.pallas_call(
        _peer_route_kernel,
        grid=(n // tm,),
        in_specs=[
            pl.BlockSpec((tm, D_MODEL), lambda i: (i, 0)),
            pl.BlockSpec((D_MODEL, PEER_HEADS * PEER_DQ), lambda i: (0, 0)),
            pl.BlockSpec((2, PEER_NKEYS, PEER_DQ // 2), lambda i: (0, 0, 0)),
        ],
        out_specs=[tab_spec, tab_spec, tab_spec, tab_spec,
                   pl.BlockSpec((PEER_HEADS, tm), lambda i: (0, i))],
        out_shape=[tab, tab, tab, tab, jax.ShapeDtypeStruct((PEER_HEADS, n), jnp.float32)],
        scratch_shapes=[pltpu.VMEM((2, PEER_TOPK, tm), jnp.float32),
                        pltpu.VMEM((PEER_NCAND, tm), jnp.float32)],
        compiler_params=pltpu.CompilerParams(dimension_semantics=("parallel",),
                                             vmem_limit_bytes=VMEM_LIMIT),
        name="peer_route",
    )(xn_bf16, wpq_bf16, keys_bf16)


def _gelu_exact(x):
    return 0.5 * x * (1.0 + lax.erf(x * (1.0 / math.sqrt(2.0))))


def _peer_dense_kernel(xn_ref, u_ref, vt_ref, s1_ref, s2_ref, e1_ref, e2_ref, thr_ref, o_ref, acc_ref, p_ref):
    j = pl.program_id(1)
    n_e1 = u_ref.shape[0] // PEER_NKEYS

    @pl.when(j == 0)
    def _():
        acc_ref[...] = jnp.zeros_like(acc_ref)

    for el in range(n_e1):
        rows = slice(el * PEER_NKEYS, (el + 1) * PEER_NKEYS)
        at = lax.dot_general(u_ref[rows, :], xn_ref[...], (((1,), (1,)), ((), ())),
                             preferred_element_type=jnp.float32)
        wt = jnp.zeros_like(at)
        for h in range(PEER_HEADS):
            comb = s1_ref[h, el:el + 1, :] + s2_ref[h]
            w = e1_ref[h, el:el + 1, :] * e2_ref[h]
            wt = wt + jnp.where(comb >= thr_ref[h:h + 1, :], w, 0.0)
        p_ref[rows, :] = (wt * _gelu_exact(at)).astype(jnp.bfloat16)
    acc_ref[...] += jnp.dot(vt_ref[...], p_ref[...], preferred_element_type=jnp.float32)

    @pl.when(j == pl.num_programs(1) - 1)
    def _():
        o_ref[...] = acc_ref[...].T


def peer_dense(xn_bf16, u_bf16, vt_bf16, s1, s2, e1, e2, thr, tm, te):
    n = xn_bf16.shape[0]
    n_e1 = te // PEER_NKEYS
    return pl.pallas_call(
        _peer_dense_kernel,
        grid=(n // tm, PEER_EXPERTS // te),
        in_specs=[
            pl.BlockSpec((tm, D_MODEL), lambda i, j: (i, 0)),
            pl.BlockSpec((te, D_MODEL), lambda i, j: (j, 0)),
            pl.BlockSpec((D_MODEL, te), lambda i, j: (0, j)),
            pl.BlockSpec((PEER_HEADS, n_e1, tm), lambda i, j: (0, j, i)),
            pl.BlockSpec((PEER_HEADS, PEER_NKEYS, tm), lambda i, j: (0, 0, i)),
            pl.BlockSpec((PEER_HEADS, n_e1, tm), lambda i, j: (0, j, i)),
            pl.BlockSpec((PEER_HEADS, PEER_NKEYS, tm), lambda i, j: (0, 0, i)),
            pl.BlockSpec((PEER_HEADS, tm), lambda i, j: (0, i)),
        ],
        out_specs=pl.BlockSpec((tm, D_MODEL), lambda i, j: (i, 0)),
        out_shape=jax.ShapeDtypeStruct((n, D_MODEL), jnp.float32),
        scratch_shapes=[pltpu.VMEM((D_MODEL, tm), jnp.float32), pltpu.VMEM((te, tm), jnp.bfloat16)],
        compiler_params=pltpu.CompilerParams(dimension_semantics=("parallel", "arbitrary"),
                                             vmem_limit_bytes=VMEM_LIMIT),
        name="peer_dense",
    )(xn_bf16, u_bf16, vt_bf16, s1, s2, e1, e2, thr)


def peer_ffn(x, w_pq, sub_keys, expert_u, expert_v):
    n = x.shape[0]
    tm = PEER_TM if n >= PEER_TM else PEER_TM_SMALL
    n_pad = -(-n // tm) * tm
    xb = jnp.pad(x, ((0, n_pad - n), (0, 0))).astype(jnp.bfloat16)
    s1, s2, e1, e2, thr = peer_route(xb, w_pq.astype(jnp.bfloat16), sub_keys.astype(jnp.bfloat16), PEER_TM_SMALL)
    out = peer_dense(xb, expert_u.astype(jnp.bfloat16), expert_v.T.astype(jnp.bfloat16), s1, s2, e1, e2, thr, tm, PEER_TE)
    return out[:n]


def merge_and_ffn(x, r_out, a_out, w_out, norm_ffn, w_pq, sub_keys, expert_u, expert_v):
    b, t, d = x.shape
    h = x + jnp.concatenate([r_out, a_out.reshape(b, t, W_ATT)], axis=-1) @ w_out
    f = peer_ffn(rms_norm(h, norm_ffn).reshape(b * t, d), w_pq, sub_keys, expert_u, expert_v)
    return h + f.reshape(b, t, d)


def kernel(x_prompt, x_sample, cache_k, cache_v, page_table, state_wkv, state_shift, norm_mix, w_in, mu_shift, w0, w_decay_up, a0, w_aaa_up, w_gate_up, k_k, k_a, r_k, gn_gain, gn_bias, w_out, norm_ffn, w_pq, peer_sub_keys, expert_u, expert_v, norm_final):
    pos_p = jnp.arange(x_prompt.shape[1], dtype=jnp.int32)
    pos_s = PAST_LEN + jnp.arange(x_sample.shape[1], dtype=jnp.int32)
    xp, xs = x_prompt, x_sample
    l = 0
    rw = (mu_shift[l], w0[l], w_decay_up[l], a0[l], w_aaa_up[l], w_gate_up[l], k_k[l], k_a[l], r_k[l], gn_gain[l], gn_bias[l])
    ffn = (w_out[l], norm_ffn[l], w_pq[l], peer_sub_keys[l], expert_u[l], expert_v[l])
    pr, q, k, v = project(xp, norm_mix[l], w_in[l], pos_p)
    bsz = xp.shape[0]
    r_out, s_new, sh_new = rwkv_group(pr, jnp.zeros((bsz, W_RWKV_IN), pr.dtype), jnp.zeros((bsz, H_RWKV, D_HEAD, D_HEAD), jnp.float32), *rw)
    flat = lambda z: z.reshape(z.shape[0], z.shape[1], W_ATT)
    a_out = moba_prompt(flat(q), flat(k), flat(v))
    xp = merge_and_ffn(xp, r_out, a_out, *ffn)
    kp_, vp_, sp_, hp_ = k[None], v[None], s_new[None], sh_new[None]
    pr, q, k, v = project(xs, norm_mix[l], w_in[l], pos_s)
    r_out, s_new, sh_new = rwkv_group(pr, state_shift[l], state_wkv[l], *rw)
    a_out = moba_sample(q, k, v, cache_k[l], cache_v[l], page_table)
    xs = merge_and_ffn(xs, r_out, a_out, *ffn)
    y_prompt = final_norm(xp, norm_final)
    y_sample = final_norm(xs, norm_final)
    return (y_prompt, y_sample, kp_, vp_, sp_, hp_, k[None], v[None], s_new[None], sh_new[None])
```

```python
import functools
import math

import jax
import jax.numpy as jnp
from jax import lax
from jax.experimental import pallas as pl
from jax.experimental.pallas import tpu as pltpu

D_MODEL = 1024
DEPTH = 1
PAST_LEN = 16384
PAGE_SIZE = 128
D_HEAD = 64
H_RWKV = 8
H_ATT = 8
W_RWKV = H_RWKV * D_HEAD
W_ATT = H_ATT * D_HEAD
D_DECAY_LORA = 64
D_AAA_LORA = 64
D_GATE_LORA = 128
W_RWKV_IN = 3 * W_RWKV + D_DECAY_LORA + D_AAA_LORA + D_GATE_LORA
W_IN = W_RWKV_IN + 3 * W_ATT
GN_EPS = 64e-5
NORM_EPS = 1e-6
MOBA_BLOCK = 256
MOBA_TOPK = 3
Q_CHUNK = 64
ROT_DIM = D_HEAD // 4
ROPE_THETA = 500000.0
PEER_HEADS = 8
PEER_NKEYS = 128
PEER_TOPK = 16
PEER_DQ = 256
PEER_EXPERTS = PEER_NKEYS * PEER_NKEYS
PEER_TM = 512
PEER_TM_SMALL = 256
PEER_TE = 1024
NEG_INF = float("-inf")
VMEM_LIMIT = 56 * 1024 * 1024
LANES = 128
SUBLANES = 8
RWKV_NB = 4
NT_DIMS = (((1,), (1,)), ((), ()))


def rms_norm(x, g):
    x32 = x.astype(jnp.float32)
    y = x32 * lax.rsqrt(jnp.mean(x32 * x32, axis=-1, keepdims=True) + NORM_EPS)
    return (y * g.astype(jnp.float32)).astype(x.dtype)


def _final_norm_kernel(x_ref, g_ref, o_ref):
    x = x_ref[...]
    y = x * lax.rsqrt(jnp.mean(x * x, axis=-1, keepdims=True) + NORM_EPS)
    o_ref[...] = y * g_ref[...]


def final_norm(x, g):
    shape = x.shape
    x2 = x.reshape(-1, shape[-1])
    n, d = x2.shape
    tm = min(n, 512)
    out = pl.pallas_call(
        _final_norm_kernel,
        grid=(n // tm,),
        in_specs=[pl.BlockSpec((tm, d), lambda i: (i, 0)), pl.BlockSpec((1, d), lambda i: (0, 0))],
        out_specs=pl.BlockSpec((tm, d), lambda i: (i, 0)),
        out_shape=jax.ShapeDtypeStruct((n, d), x.dtype),
    )(x2, g.reshape(1, d))
    return out.reshape(shape)


def partial_rope(x, pos):
    half = ROT_DIM // 2
    inv = 1.0 / (ROPE_THETA ** (jnp.arange(half, dtype=jnp.float32) * 2.0 / ROT_DIM))
    ang = pos.astype(jnp.float32)[:, None] * inv[None, :]
    cos = jnp.cos(ang)[None, :, None, :]
    sin = jnp.sin(ang)[None, :, None, :]
    x32 = x.astype(jnp.float32)
    x1 = x32[..., :half]
    x2 = x32[..., half:ROT_DIM]
    out = jnp.concatenate([x1 * cos - x2 * sin, x2 * cos + x1 * sin, x32[..., ROT_DIM:]], axis=-1)
    return out.astype(x.dtype)


def project(x, norm_g, w_in, pos):
    b, t, _ = x.shape
    p = rms_norm(x, norm_g) @ w_in
    q, k, v = jnp.split(p[..., W_RWKV_IN:], 3, axis=-1)
    heads = lambda z: z.reshape(b, t, H_ATT, D_HEAD)
    return p[..., :W_RWKV_IN], partial_rope(heads(q), pos), partial_rope(heads(k), pos), heads(v)


N_PAIR = W_RWKV // LANES
HI = lax.Precision.HIGHEST


def _sigmoid(x):
    return 1.0 / (1.0 + jnp.exp(-x))


def _softplus(x):
    return jnp.maximum(x, 0.0) + jnp.log(1.0 + jnp.exp(-jnp.abs(x)))


def _rwkv_prep_kernel(p_ref, prev0_ref, mu_ref, w0_ref, wdec_ref, a0_ref, waaa_ref, wgate_ref, kk_w_ref, ka_ref,
                      rk_ref, seg_ref, r_o, w_o, k_o, kk_o, b_o, v_o, g_o, bonus_o, carry_ref):
    ti = pl.program_id(1)

    @pl.when(ti == 0)
    def _():
        carry_ref[...] = prev0_ref[0]

    p = p_ref[0]
    row = lax.broadcasted_iota(jnp.int32, p.shape, 0)
    if p.shape[0] == 1:
        prev = carry_ref[...]
    else:
        prev = jnp.where(row == 0, carry_ref[...], pltpu.roll(p, 1, 0))
    carry_ref[...] = p[p.shape[0] - 1:, :]
    ps = p + (prev - p) * mu_ref[...]
    r = ps[:, 0:W_RWKV]
    k = ps[:, W_RWKV:2 * W_RWKV]
    v = ps[:, 2 * W_RWKV:3 * W_RWKV]
    lora = ps[:, 3 * W_RWKV:3 * W_RWKV + LANES]
    gd = ps[:, 3 * W_RWKV + LANES:]
    dec = jnp.dot(jnp.tanh(lora).astype(jnp.bfloat16), wdec_ref[...], preferred_element_type=jnp.float32)
    logw = -jnp.exp(-_softplus(-(w0_ref[...] + dec)) - 0.5)
    a = _sigmoid(a0_ref[...] + jnp.dot(lora.astype(jnp.bfloat16), waaa_ref[...], preferred_element_type=jnp.float32))
    g = jnp.dot(_sigmoid(gd).astype(jnp.bfloat16), wgate_ref[...], preferred_element_type=jnp.float32)
    kk = k * kk_w_ref[...]
    sumsq = jnp.dot(kk * kk, seg_ref[...], precision=HI, preferred_element_type=jnp.float32)
    kk = kk / jnp.maximum(jnp.sqrt(sumsq), 1e-12)
    k2 = k * (1.0 + (a - 1.0) * ka_ref[...])
    rkk = jnp.dot(r * k2 * rk_ref[...], seg_ref[...], precision=HI, preferred_element_type=jnp.float32)
    r_o[0] = r
    w_o[0] = jnp.exp(logw)
    k_o[0] = k2
    kk_o[0] = kk
    b_o[0] = kk * a
    v_o[0] = v
    g_o[0] = g
    bonus_o[0] = rkk * v


def rwkv_prep(p, prev0, mu, w0, w_dec, a0, w_aaa, w_gate, k_k, k_a, r_k, tt):
    b, t, _ = p.shape
    f32 = jnp.float32
    wdec_pad = jnp.concatenate([w_dec, jnp.zeros_like(w_aaa)], axis=0).astype(jnp.bfloat16)
    waaa_pad = jnp.concatenate([jnp.zeros_like(w_dec), w_aaa], axis=0).astype(jnp.bfloat16)
    head = jnp.arange(W_RWKV) // D_HEAD
    seg = (head[:, None] == head[None, :]).astype(f32)
    row = lambda z: z.reshape(1, -1).astype(f32)
    vec_spec = lambda n: pl.BlockSpec((1, n), lambda bi, ti: (0, 0))
    mat_spec = lambda m, n: pl.BlockSpec((m, n), lambda bi, ti: (0, 0))
    out_spec = pl.BlockSpec((1, tt, W_RWKV), lambda bi, ti: (bi, ti, 0))
    out = jax.ShapeDtypeStruct((b, t, W_RWKV), f32)
    return pl.pallas_call(
        _rwkv_prep_kernel,
        grid=(b, t // tt),
        in_specs=[
            pl.BlockSpec((1, tt, W_RWKV_IN), lambda bi, ti: (bi, ti, 0)),
            pl.BlockSpec((1, 1, W_RWKV_IN), lambda bi, ti: (bi, 0, 0)),
            vec_spec(W_RWKV_IN), vec_spec(W_RWKV), mat_spec(LANES, W_RWKV), vec_spec(W_RWKV),
            mat_spec(LANES, W_RWKV), mat_spec(D_GATE_LORA, W_RWKV), vec_spec(W_RWKV), vec_spec(W_RWKV),
            vec_spec(W_RWKV), mat_spec(W_RWKV, W_RWKV),
        ],
        out_specs=[out_spec] * 8,
        out_shape=[out] * 8,
        scratch_shapes=[pltpu.VMEM((1, W_RWKV_IN), f32)],
        compiler_params=pltpu.CompilerParams(dimension_semantics=("parallel", "arbitrary"),
                                             vmem_limit_bytes=VMEM_LIMIT),
        name="rwkv_prep",
    )(p, prev0.reshape(b, 1, W_RWKV_IN), row(mu), row(w0), wdec_pad, row(a0), waaa_pad, w_gate.astype(jnp.bfloat16),
      row(k_k), row(k_a), row(r_k), seg)


def _rwkv_scan_kernel(r_ref, w_ref, k_ref, kk_ref, b_ref, v_ref, s0_ref, y_ref, sT_ref, s_ref):
    ci = pl.program_id(1)
    nb, tc = r_ref.shape[0], r_ref.shape[1]
    tiles = [(bi, p) for bi in range(nb) for p in range(N_PAIR)]

    @pl.when(ci == 0)
    def _():
        for i, (bi, p) in enumerate(tiles):
            s_ref[i] = s0_ref[bi, p]

    lane = lax.broadcasted_iota(jnp.int32, (D_HEAD, LANES), 1)
    sub = lax.broadcasted_iota(jnp.int32, (D_HEAD, LANES), 0)
    first = lane < D_HEAD
    eye2 = (sub == (lane % D_HEAD))
    sub128 = lax.broadcasted_iota(jnp.int32, (LANES, LANES), 0)
    lane128 = lax.broadcasted_iota(jnp.int32, (LANES, LANES), 1)
    eye128 = sub128 == lane128
    seg = jnp.where((sub128 // D_HEAD) == (lane128 // D_HEAD), 1.0, 0.0).astype(jnp.bfloat16)

    def head_sums(xs):
        parts = []
        for x in xs:
            hi = x.astype(jnp.bfloat16)
            parts += [hi, (x - hi.astype(jnp.float32)).astype(jnp.bfloat16)]
        both = jnp.dot(jnp.concatenate(parts, axis=0), seg, preferred_element_type=jnp.float32)
        return [both[i * LANES:i * LANES + D_HEAD] + both[i * LANES + D_HEAD:(i + 1) * LANES] for i in range(len(xs))]

    grp = min(tc, SUBLANES)

    def steps(gi, carry):
        rows = pl.ds(pl.multiple_of(gi * grp, grp), grp)
        lanes = lambda p: slice(p * LANES, (p + 1) * LANES)
        load = lambda ref: [ref[bi, rows, lanes(p)] for bi, p in tiles]
        r_g, w_g, k_g, kk_g, b_g, v_g = load(r_ref), load(w_ref), load(k_ref), load(kk_ref), load(b_ref), load(v_ref)
        states = [s_ref[i] for i in range(len(tiles))]
        yrows = [[] for _ in tiles]
        for j in range(grp):
            row = slice(j, j + 1)
            sks = head_sums([s * kk[row] for s, kk in zip(states, kk_g)])
            for i in range(len(tiles)):
                vcol = jnp.sum(jnp.where(eye128, v_g[i][row], 0.0), axis=1, keepdims=True)
                vmat = jnp.where(first, vcol[0:D_HEAD], vcol[D_HEAD:LANES])
                states[i] = states[i] * w_g[i][row] - sks[i] * b_g[i][row] + vmat * k_g[i][row]
            ys = head_sums([s * r[row] for s, r in zip(states, r_g)])
            for i in range(len(tiles)):
                yrows[i].append(jnp.sum(jnp.where(eye2, ys[i], 0.0), axis=0, keepdims=True))
        for i, (bi, p) in enumerate(tiles):
            y_ref[bi, rows, lanes(p)] = yrows[i][0] if grp == 1 else jnp.concatenate(yrows[i], axis=0)
            s_ref[i] = states[i]
        return carry

    lax.fori_loop(0, tc // grp, steps, 0)

    @pl.when(ci == pl.num_programs(1) - 1)
    def _():
        for i, (bi, p) in enumerate(tiles):
            sT_ref[bi, p] = s_ref[i]


def rwkv_scan(r, w, k, kk, bb, v, s0, nb, tc):
    b, t, _ = r.shape
    f32 = jnp.float32
    pair = lambda s: s.reshape(b, N_PAIR, 2, D_HEAD, D_HEAD).transpose(0, 1, 3, 2, 4).reshape(b, N_PAIR, D_HEAD, LANES)
    unpair = lambda s: s.reshape(b, N_PAIR, D_HEAD, 2, D_HEAD).transpose(0, 1, 3, 2, 4).reshape(b, H_RWKV, D_HEAD, D_HEAD)
    seq_spec = pl.BlockSpec((nb, tc, W_RWKV), lambda bi, ci: (bi, ci, 0))
    st_spec = pl.BlockSpec((nb, N_PAIR, D_HEAD, LANES), lambda bi, ci: (bi, 0, 0, 0))
    y, s_fin = pl.pallas_call(
        _rwkv_scan_kernel,
        grid=(b // nb, t // tc),
        in_specs=[seq_spec] * 6 + [st_spec],
        out_specs=[seq_spec, st_spec],
        out_shape=[jax.ShapeDtypeStruct((b, t, W_RWKV), f32), jax.ShapeDtypeStruct((b, N_PAIR, D_HEAD, LANES), f32)],
        scratch_shapes=[pltpu.VMEM((nb * N_PAIR, D_HEAD, LANES), f32)],
        compiler_params=pltpu.CompilerParams(dimension_semantics=("parallel", "arbitrary"),
                                             vmem_limit_bytes=VMEM_LIMIT),
        name="rwkv_scan",
    )(r, w, k, kk, bb, v, pair(s0.astype(f32)))
    return y, unpair(s_fin)


def _rwkv_post_kernel(y_ref, bonus_ref, g_ref, gng_ref, gnb_ref, seg_ref, o_ref):
    y = y_ref[...]
    inv = 1.0 / D_HEAD
    mean = jnp.dot(y, seg_ref[...], precision=HI, preferred_element_type=jnp.float32) * inv
    d = y - mean
    var = jnp.dot(d * d, seg_ref[...], precision=HI, preferred_element_type=jnp.float32) * inv
    yn = d * lax.rsqrt(var + GN_EPS) * gng_ref[...] + gnb_ref[...]
    o_ref[...] = (yn + bonus_ref[...]) * g_ref[...]


def rwkv_post(y, bonus, g, gn_g, gn_b, tt):
    n = y.shape[0]
    head = jnp.arange(W_RWKV) // D_HEAD
    seg = (head[:, None] == head[None, :]).astype(jnp.float32)
    row_spec = pl.BlockSpec((tt, W_RWKV), lambda i: (i, 0))
    vec_spec = pl.BlockSpec((1, W_RWKV), lambda i: (0, 0))
    return pl.pallas_call(
        _rwkv_post_kernel,
        grid=(n // tt,),
        in_specs=[row_spec, row_spec, row_spec, vec_spec, vec_spec, pl.BlockSpec((W_RWKV, W_RWKV), lambda i: (0, 0))],
        out_specs=row_spec,
        out_shape=jax.ShapeDtypeStruct((n, W_RWKV), jnp.float32),
        compiler_params=pltpu.CompilerParams(dimension_semantics=("parallel",), vmem_limit_bytes=VMEM_LIMIT),
        name="rwkv_post",
    )(y, bonus, g, gn_g.reshape(1, -1), gn_b.reshape(1, -1), seg)


def rwkv_group(p, prev0, s0, mu, w0, w_dec, a0, w_aaa, w_gate, k_k, k_a, r_k, gn_g, gn_b):
    b, t, _ = p.shape
    tt = min(t, 256)
    tc = min(t, 128)
    r, w, k2, kk, bb, v, g, bonus = rwkv_prep(p, prev0, mu, w0, w_dec, a0, w_aaa, w_gate, k_k, k_a, r_k, tt)
    y, s_fin = rwkv_scan(r, w, k2, kk, bb, v, s0, RWKV_NB, tc)
    n = b * t
    tp = min(n, 256)
    out = rwkv_post(y.reshape(n, W_RWKV), bonus.reshape(n, W_RWKV), g.reshape(n, W_RWKV), gn_g, gn_b, tp)
    return out.reshape(b, t, W_RWKV), s_fin, p[:, -1, :W_RWKV_IN]


def moba_attend(q, k_own, v_own, own_mask, k_sel, v_sel, sel_mask):
    scale = D_HEAD ** -0.5
    s_own = jnp.einsum('bqhd,bkhd->bqhk', q, k_own).astype(jnp.float32) * scale
    s_own = jnp.where(own_mask[None, :, None, :], s_own, -jnp.inf)
    if k_sel is None:
        p = jax.nn.softmax(s_own, axis=-1).astype(v_own.dtype)
        return jnp.einsum('bqhk,bkhd->bqhd', p, v_own)
    s_sel = jnp.einsum('bqhd,bqhskd->bqhsk', q, k_sel).astype(jnp.float32) * scale
    s_sel = jnp.where(sel_mask[..., None], s_sel, -jnp.inf)
    n_s = s_sel.shape[3] * s_sel.shape[4]
    scores = jnp.concatenate([s_sel.reshape(s_sel.shape[:3] + (n_s,)), s_own], axis=-1)
    p = jax.nn.softmax(scores, axis=-1).astype(v_own.dtype)
    p_sel = p[..., :n_s].reshape(s_sel.shape)
    return jnp.einsum('bqhsk,bqhskd->bqhd', p_sel, v_sel) + jnp.einsum('bqhk,bkhd->bqhd', p[..., n_s:], v_own)


def _moba_prompt_kernel(q_ref, k_ref, v_ref, o_ref, vt_ref, kb_ref, sel_ref, m_ref, l_ref, acc_ref):
    qi = pl.program_id(2)
    t_len = k_ref.shape[1]
    nb = t_len // MOBA_BLOCK
    scale = D_HEAD ** -0.5

    @pl.when(qi == 0)
    def _():
        vt_ref[...] = v_ref[0].T.astype(jnp.bfloat16)
        kb_ref[...] = k_ref[0].astype(jnp.bfloat16)

    q2 = q_ref[0]
    lane = lax.broadcasted_iota(jnp.int32, q2.shape, 1)
    kmean = jnp.mean(k_ref[0].reshape(nb, MOBA_BLOCK, LANES), axis=1)
    blk_row = lax.broadcasted_iota(jnp.int32, (nb, MOBA_BLOCK), 0)
    key_row = lax.broadcasted_iota(jnp.int32, (MOBA_BLOCK, MOBA_BLOCK), 0)
    qry_col = lax.broadcasted_iota(jnp.int32, (MOBA_BLOCK, MOBA_BLOCK), 1)
    qms = []
    for s in range(2):
        qm = jnp.where((lane // D_HEAD) == s, q2, 0.0)
        qms.append(qm.astype(jnp.bfloat16))
        gate = lax.dot_general(kmean, qm, NT_DIMS, precision=lax.Precision.HIGHEST,
                               preferred_element_type=jnp.float32)
        rank = jnp.zeros(gate.shape, jnp.float32)
        for jp in range(nb):
            g_jp = gate[jp:jp + 1, :]
            ahead = (g_jp > gate) | ((g_jp == gate) & (jp < blk_row))
            rank = rank + jnp.where(ahead & (jp < qi), 1.0, 0.0)
        sel_ref[s] = jnp.where((rank < float(MOBA_TOPK)) & (blk_row < qi), 1.0, 0.0)

        own = pl.ds(pl.multiple_of(qi * MOBA_BLOCK, MOBA_BLOCK), MOBA_BLOCK)
        st = lax.dot_general(kb_ref[own, :], qms[s], NT_DIMS, preferred_element_type=jnp.float32) * scale
        st = jnp.where(key_row <= qry_col, st, NEG_INF)
        m0 = jnp.max(st, axis=0, keepdims=True)
        p = jnp.exp(st - m0)
        m_ref[s] = m0
        l_ref[s] = jnp.sum(p, axis=0, keepdims=True)
        acc_ref[s] = jnp.dot(vt_ref[:, own], p.astype(jnp.bfloat16), preferred_element_type=jnp.float32)

    def past_block(j, carry):
        blk = pl.ds(pl.multiple_of(j * MOBA_BLOCK, MOBA_BLOCK), MOBA_BLOCK)
        k_j = kb_ref[blk, :]
        vt_j = vt_ref[:, blk]
        for s in range(2):
            st = lax.dot_general(k_j, qms[s], NT_DIMS, preferred_element_type=jnp.float32) * scale
            st = jnp.where(sel_ref[s, pl.ds(j, 1), :] > 0.0, st, NEG_INF)
            m_old = m_ref[s]
            m_new = jnp.maximum(m_old, jnp.max(st, axis=0, keepdims=True))
            alpha = jnp.exp(m_old - m_new)
            p = jnp.exp(st - m_new)
            m_ref[s] = m_new
            l_ref[s] = alpha * l_ref[s] + jnp.sum(p, axis=0, keepdims=True)
            acc_ref[s] = alpha * acc_ref[s] + jnp.dot(vt_j, p.astype(jnp.bfloat16), preferred_element_type=jnp.float32)
        return carry

    lax.fori_loop(0, qi, past_block, 0)

    row = lax.broadcasted_iota(jnp.int32, (LANES, MOBA_BLOCK), 0)
    out_t = jnp.where((row // D_HEAD) == 0, acc_ref[0] / l_ref[0], acc_ref[1] / l_ref[1])
    o_ref[0] = out_t.T


def moba_prompt(q, k, v):
    b, t, w = q.shape
    nb = t // MOBA_BLOCK
    return pl.pallas_call(
        _moba_prompt_kernel,
        grid=(b, w // LANES, nb),
        in_specs=[
            pl.BlockSpec((1, MOBA_BLOCK, LANES), lambda bi, hp, qi: (bi, qi, hp)),
            pl.BlockSpec((1, t, LANES), lambda bi, hp, qi: (bi, 0, hp)),
            pl.BlockSpec((1, t, LANES), lambda bi, hp, qi: (bi, 0, hp)),
        ],
        out_specs=pl.BlockSpec((1, MOBA_BLOCK, LANES), lambda bi, hp, qi: (bi, qi, hp)),
        out_shape=jax.ShapeDtypeStruct((b, t, w), jnp.float32),
        scratch_shapes=[
            pltpu.VMEM((LANES, t), jnp.bfloat16),
            pltpu.VMEM((t, LANES), jnp.bfloat16),
            pltpu.VMEM((2, nb, MOBA_BLOCK), jnp.float32),
            pltpu.VMEM((2, 1, MOBA_BLOCK), jnp.float32),
            pltpu.VMEM((2, 1, MOBA_BLOCK), jnp.float32),
            pltpu.VMEM((2, LANES, MOBA_BLOCK), jnp.float32),
        ],
        compiler_params=pltpu.CompilerParams(dimension_semantics=("parallel", "parallel", "arbitrary"),
                                             vmem_limit_bytes=VMEM_LIMIT),
        name="moba_prompt",
    )(q, k, v)


def moba_sample(q, k, v, cache_k, cache_v, page_table):
    bd, s, h, dh = q.shape
    ppb = MOBA_BLOCK // PAGE_SIZE
    n_full = PAST_LEN // MOBA_BLOCK
    rem = PAST_LEN - n_full * MOBA_BLOCK
    n_sel = min(MOBA_TOPK, n_full)
    k_own, v_own = k, v
    own_mask = jnp.arange(rem + s)[None, :] <= (rem + jnp.arange(s))[:, None]
    pt_full = page_table[:, :n_full * ppb]
    kmean = jnp.mean(cache_k[pt_full].reshape(bd, n_full, MOBA_BLOCK, h, dh).astype(jnp.float32), axis=2)
    gate = jnp.einsum('bshd,bnhd->bshn', q.astype(jnp.float32), kmean)
    _, sel = lax.top_k(gate, n_sel)
    logical = sel[..., None] * ppb + jnp.arange(ppb)
    phys = page_table[jnp.arange(bd)[:, None, None, None, None], logical]
    hi = jnp.arange(h)[None, None, :, None, None]
    k_sel = cache_k[phys, :, hi, :].reshape(bd, s, h, n_sel, MOBA_BLOCK, dh).astype(k.dtype)
    v_sel = cache_v[phys, :, hi, :].reshape(bd, s, h, n_sel, MOBA_BLOCK, dh).astype(v.dtype)
    ok = jnp.ones((bd, s, h, n_sel), dtype=bool)
    return moba_attend(q, k_own, v_own, own_mask, k_sel, v_sel, ok)


PEER_CANDS = [(a, b) for a in range(PEER_TOPK) for b in range(PEER_TOPK) if (a + 1) * (b + 1) <= PEER_TOPK]
PEER_NCAND = -(-len(PEER_CANDS) // 8) * 8


def _extract_topk(x, n_rows, k):
    rows = lax.broadcasted_iota(jnp.int32, x.shape, 0).astype(jnp.float32)
    vals = []
    for _ in range(k):
        m = jnp.max(x, axis=0, keepdims=True)
        first = jnp.min(jnp.where(x == m, rows, float(n_rows)), axis=0, keepdims=True)
        x = jnp.where(rows == first, NEG_INF, x)
        vals.append(m)
    return x, vals


def _peer_route_kernel(xn_ref, wpq_ref, keys_ref, s1_ref, s2_ref, e1_ref, e2_ref, thr_ref, sv_ref, comb_ref):
    q = jnp.dot(xn_ref[...], wpq_ref[...], preferred_element_type=jnp.float32)
    half = PEER_DQ // 2
    for h in range(PEER_HEADS):
        masked = []
        for p in range(2):
            qs = q[:, (2 * h + p) * half:(2 * h + p + 1) * half].astype(jnp.bfloat16)
            st = lax.dot_general(keys_ref[p], qs, (((1,), (1,)), ((), ())),
                                 preferred_element_type=jnp.float32)
            rem, vals = _extract_topk(st, PEER_NKEYS, PEER_TOPK)
            for i, v in enumerate(vals):
                sv_ref[p, i:i + 1, :] = v
            masked.append(jnp.where(rem == NEG_INF, st, NEG_INF))
        comb_ref[...] = jnp.full(comb_ref.shape, NEG_INF, jnp.float32)
        for c, (a, b) in enumerate(PEER_CANDS):
            comb_ref[c:c + 1, :] = sv_ref[0, a:a + 1, :] + sv_ref[1, b:b + 1, :]
        _, cvals = _extract_topk(comb_ref[...], PEER_NCAND, PEER_TOPK)
        cmax = cvals[0]
        z = jnp.zeros_like(cmax)
        for v in cvals:
            z = z + jnp.exp(v - cmax)
        thr_ref[h:h + 1, :] = cvals[-1]
        s1_ref[h] = masked[0]
        s2_ref[h] = masked[1]
        e1_ref[h] = jnp.exp(masked[0] - sv_ref[0, 0:1, :])
        e2_ref[h] = jnp.exp(masked[1] - sv_ref[1, 0:1, :]) / z


def peer_route(xn_bf16, wpq_bf16, keys_bf16, tm):
    n = xn_bf16.shape[0]
    tab = jax.ShapeDtypeStruct((PEER_HEADS, PEER_NKEYS, n), jnp.float32)
    tab_spec = pl.BlockSpec((PEER_HEADS, PEER_NKEYS, tm), lambda i: (0, 0, i))
    return pl.pallas_call(
        _peer_route_kernel,
        grid=(n // tm,),
        in_specs=[
            pl.BlockSpec((tm, D_MODEL), lambda i: (i, 0)),
            pl.BlockSpec((D_MODEL, PEER_HEADS * PEER_DQ), lambda i: (0, 0)),
            pl.BlockSpec((2, PEER_NKEYS, PEER_DQ // 2), lambda i: (0, 0, 0)),
        ],
        out_specs=[tab_spec, tab_spec, tab_spec, tab_spec,
                   pl.BlockSpec((PEER_HEADS, tm), lambda i: (0, i))],
        out_shape=[tab, tab, tab, tab, jax.ShapeDtypeStruct((PEER_HEADS, n), jnp.float32)],
        scratch_shapes=[pltpu.VMEM((2, PEER_TOPK, tm), jnp.float32),
                        pltpu.VMEM((PEER_NCAND, tm), jnp.float32)],
        compiler_params=pltpu.CompilerParams(dimension_semantics=("parallel",),
                                             vmem_limit_bytes=VMEM_LIMIT),
        name="peer_route",
    )(xn_bf16, wpq_bf16, keys_bf16)


def _gelu_exact(x):
    return 0.5 * x * (1.0 + lax.erf(x * (1.0 / math.sqrt(2.0))))


def _peer_dense_kernel(xn_ref, u_ref, vt_ref, s1_ref, s2_ref, e1_ref, e2_ref, thr_ref, o_ref, acc_ref, p_ref):
    j = pl.program_id(1)
    n_e1 = u_ref.shape[0] // PEER_NKEYS

    @pl.when(j == 0)
    def _():
        acc_ref[...] = jnp.zeros_like(acc_ref)

    for el in range(n_e1):
        rows = slice(el * PEER_NKEYS, (el + 1) * PEER_NKEYS)
        at = lax.dot_general(u_ref[rows, :], xn_ref[...], (((1,), (1,)), ((), ())),
                             preferred_element_type=jnp.float32)
        wt = jnp.zeros_like(at)
        for h in range(PEER_HEADS):
            comb = s1_ref[h, el:el + 1, :] + s2_ref[h]
            w = e1_ref[h, el:el + 1, :] * e2_ref[h]
            wt = wt + jnp.where(comb >= thr_ref[h:h + 1, :], w, 0.0)
        p_ref[rows, :] = (wt * _gelu_exact(at)).astype(jnp.bfloat16)
    acc_ref[...] += jnp.dot(vt_ref[...], p_ref[...], preferred_element_type=jnp.float32)

    @pl.when(j == pl.num_programs(1) - 1)
    def _():
        o_ref[...] = acc_ref[...].T


def peer_dense(xn_bf16, u_bf16, vt_bf16, s1, s2, e1, e2, thr, tm, te):
    n = xn_bf16.shape[0]
    n_e1 = te // PEER_NKEYS
    return pl.pallas_call(
        _peer_dense_kernel,
        grid=(n // tm, PEER_EXPERTS // te),
        in_specs=[
            pl.BlockSpec((tm, D_MODEL), lambda i, j: (i, 0)),
            pl.BlockSpec((te, D_MODEL), lambda i, j: (j, 0)),
            pl.BlockSpec((D_MODEL, te), lambda i, j: (0, j)),
            pl.BlockSpec((PEER_HEADS, n_e1, tm), lambda i, j: (0, j, i)),
            pl.BlockSpec((PEER_HEADS, PEER_NKEYS, tm), lambda i, j: (0, 0, i)),
            pl.BlockSpec((PEER_HEADS, n_e1, tm), lambda i, j: (0, j, i)),
            pl.BlockSpec((PEER_HEADS, PEER_NKEYS, tm), lambda i, j: (0, 0, i)),
            pl.BlockSpec((PEER_HEADS, tm), lambda i, j: (0, i)),
        ],
        out_specs=pl.BlockSpec((tm, D_MODEL), lambda i, j: (i, 0)),
        out_shape=jax.ShapeDtypeStruct((n, D_MODEL), jnp.float32),
        scratch_shapes=[pltpu.VMEM((D_MODEL, tm), jnp.float32), pltpu.VMEM((te, tm), jnp.bfloat16)],
        compiler_params=pltpu.CompilerParams(dimension_semantics=("parallel", "arbitrary"),
                                             vmem_limit_bytes=VMEM_LIMIT),
        name="peer_dense",
    )(xn_bf16, u_bf16, vt_bf16, s1, s2, e1, e2, thr)


def peer_ffn(x, w_pq, sub_keys, expert_u, expert_v):
    n = x.shape[0]
    tm = PEER_TM if n >= PEER_TM else PEER_TM_SMALL
    n_pad = -(-n // tm) * tm
    xb = jnp.pad(x, ((0, n_pad - n), (0, 0))).astype(jnp.bfloat16)
    s1, s2, e1, e2, thr = peer_route(xb, w_pq.astype(jnp.bfloat16), sub_keys.astype(jnp.bfloat16), PEER_TM_SMALL)
    out = peer_dense(xb, expert_u.astype(jnp.bfloat16), expert_v.T.astype(jnp.bfloat16), s1, s2, e1, e2, thr, tm, PEER_TE)
    return out[:n]


def merge_and_ffn(x, r_out, a_out, w_out, norm_ffn, w_pq, sub_keys, expert_u, expert_v):
    b, t, d = x.shape
    h = x + jnp.concatenate([r_out, a_out.reshape(b, t, W_ATT)], axis=-1) @ w_out
    f = peer_ffn(rms_norm(h, norm_ffn).reshape(b * t, d), w_pq, sub_keys, expert_u, expert_v)
    return h + f.reshape(b, t, d)


def kernel(x_prompt, x_sample, cache_k, cache_v, page_table, state_wkv, state_shift, norm_mix, w_in, mu_shift, w0, w_decay_up, a0, w_aaa_up, w_gate_up, k_k, k_a, r_k, gn_gain, gn_bias, w_out, norm_ffn, w_pq, peer_sub_keys, expert_u, expert_v, norm_final):
    pos_p = jnp.arange(x_prompt.shape[1], dtype=jnp.int32)
    pos_s = PAST_LEN + jnp.arange(x_sample.shape[1], dtype=jnp.int32)
    xp, xs = x_prompt, x_sample
    l = 0
    rw = (mu_shift[l], w0[l], w_decay_up[l], a0[l], w_aaa_up[l], w_gate_up[l], k_k[l], k_a[l], r_k[l], gn_gain[l], gn_bias[l])
    ffn = (w_out[l], norm_ffn[l], w_pq[l], peer_sub_keys[l], expert_u[l], expert_v[l])
    pr, q, k, v = project(xp, norm_mix[l], w_in[l], pos_p)
    bsz = xp.shape[0]
    r_out, s_new, sh_new = rwkv_group(pr, jnp.zeros((bsz, W_RWKV_IN), pr.dtype), jnp.zeros((bsz, H_RWKV, D_HEAD, D_HEAD), jnp.float32), *rw)
    flat = lambda z: z.reshape(z.shape[0], z.shape[1], W_ATT)
    a_out = moba_prompt(flat(q), flat(k), flat(v))
    xp = merge_and_ffn(xp, r_out, a_out, *ffn)
    kp_, vp_, sp_, hp_ = k[None], v[None], s_new[None], sh_new[None]
    pr, q, k, v = project(xs, norm_mix[l], w_in[l], pos_s)
    r_out, s_new, sh_new = rwkv_group(pr, state_shift[l], state_wkv[l], *rw)
    a_out = moba_sample(q, k, v, cache_k[l], cache_v[l], page_table)
    xs = merge_and_ffn(xs, r_out, a_out, *ffn)
    y_prompt = final_norm(xp, norm_final)
    y_sample = final_norm(xs, norm_final)
    return (y_prompt, y_sample, kp_, vp_, sp_, hp_, k[None], v[None], s_new[None], sh_new[None])
```

```python
import functools
import math

import jax
import jax.numpy as jnp
from jax import lax
from jax.experimental import pallas as pl
from jax.experimental.pallas import tpu as pltpu

D_MODEL = 1024
DEPTH = 1
PAST_LEN = 16384
PAGE_SIZE = 128
D_HEAD = 64
H_RWKV = 8
H_ATT = 8
W_RWKV = H_RWKV * D_HEAD
W_ATT = H_ATT * D_HEAD
D_DECAY_LORA = 64
D_AAA_LORA = 64
D_GATE_LORA = 128
W_RWKV_IN = 3 * W_RWKV + D_DECAY_LORA + D_AAA_LORA + D_GATE_LORA
W_IN = W_RWKV_IN + 3 * W_ATT
GN_EPS = 64e-5
NORM_EPS = 1e-6
MOBA_BLOCK = 256
MOBA_TOPK = 3
Q_CHUNK = 64
ROT_DIM = D_HEAD // 4
ROPE_THETA = 500000.0
PEER_HEADS = 8
PEER_NKEYS = 128
PEER_TOPK = 16
PEER_DQ = 256
PEER_EXPERTS = PEER_NKEYS * PEER_NKEYS
PEER_TM = 512
PEER_TM_SMALL = 256
PEER_TE = 1024
PEER_ELB = 4
PEER_SGR = 32
NEG_INF = float("-inf")
VMEM_LIMIT = 56 * 1024 * 1024
LANES = 128
SUBLANES = 8
PROJ_TT = 512
RWKV_NB = 4
NT_DIMS = (((1,), (1,)), ((), ()))


def _rms_norm(x, g):
    return x * lax.rsqrt(jnp.mean(x * x, axis=-1, keepdims=True) + NORM_EPS) * g


def rope_tables(pos):
    half = ROT_DIM // 2
    inv = 1.0 / (ROPE_THETA ** (jnp.arange(half, dtype=jnp.float32) * 2.0 / ROT_DIM))
    ang = pos.astype(jnp.float32)[:, None] * inv[None, :]
    lane = jnp.arange(LANES) % D_HEAD
    idx = lane % half
    cos = jnp.where(lane[None, :] < ROT_DIM, jnp.cos(ang)[:, idx], 1.0)
    sin = jnp.sin(ang)[:, idx]
    sin_lo = jnp.where(lane[None, :] < half, -sin, 0.0)
    sin_hi = jnp.where((lane[None, :] >= half) & (lane[None, :] < ROT_DIM), sin, 0.0)
    return cos, sin_lo, sin_hi


def _project_kernel(x_ref, g_ref, w_ref, cos_ref, slo_ref, shi_ref, pr_o, q_o, k_o, v_o):
    xn = _rms_norm(x_ref[0], g_ref[...])
    p = jnp.dot(xn.astype(jnp.bfloat16), w_ref[...], preferred_element_type=jnp.float32)
    pr_o[0] = p[:, :W_RWKV_IN]
    reps = W_ATT // LANES
    cos = jnp.concatenate([cos_ref[...]] * reps, axis=1)
    slo = jnp.concatenate([slo_ref[...]] * reps, axis=1)
    shi = jnp.concatenate([shi_ref[...]] * reps, axis=1)
    half = ROT_DIM // 2

    def rope(z):
        ahead = pltpu.roll(z, W_ATT - half, 1)
        behind = pltpu.roll(z, half, 1)
        return z * cos + ahead * slo + behind * shi

    q_o[0] = rope(p[:, W_RWKV_IN:W_RWKV_IN + W_ATT])
    k_o[0] = rope(p[:, W_RWKV_IN + W_ATT:W_RWKV_IN + 2 * W_ATT])
    v_o[0] = p[:, W_RWKV_IN + 2 * W_ATT:]


def project(x, norm_g, w_in_bf16, pos, tt):
    b, t, d = x.shape
    cos, slo, shi = rope_tables(pos)
    f32 = jnp.float32
    tab_spec = pl.BlockSpec((tt, LANES), lambda bi, ti: (ti, 0))
    out_spec = lambda w: pl.BlockSpec((1, tt, w), lambda bi, ti: (bi, ti, 0))
    return pl.pallas_call(
        _project_kernel,
        grid=(b, t // tt),
        in_specs=[
            pl.BlockSpec((1, tt, d), lambda bi, ti: (bi, ti, 0)),
            pl.BlockSpec((1, d), lambda bi, ti: (0, 0)),
            pl.BlockSpec((d, W_IN), lambda bi, ti: (0, 0)),
            tab_spec, tab_spec, tab_spec,
        ],
        out_specs=[out_spec(W_RWKV_IN), out_spec(W_ATT), out_spec(W_ATT), out_spec(W_ATT)],
        out_shape=[jax.ShapeDtypeStruct((b, t, W_RWKV_IN), f32)] + [jax.ShapeDtypeStruct((b, t, W_ATT), f32)] * 3,
        compiler_params=pltpu.CompilerParams(dimension_semantics=("parallel", "parallel"),
                                             vmem_limit_bytes=VMEM_LIMIT),
        name="project",
    )(x, norm_g.reshape(1, d), w_in_bf16, cos, slo, shi)


N_PAIR = W_RWKV // LANES
HI = lax.Precision.HIGHEST


def _sigmoid(x):
    return 1.0 / (1.0 + jnp.exp(-x))


def _softplus(x):
    return jnp.maximum(x, 0.0) + jnp.log(1.0 + jnp.exp(-jnp.abs(x)))


def _rwkv_prep_kernel(p_ref, prev0_ref, mu_ref, w0_ref, wdec_ref, a0_ref, waaa_ref, wgate_ref, kk_w_ref, ka_ref,
                      rk_ref, seg_ref, r_o, w_o, k_o, kk_o, b_o, v_o, g_o, bonus_o, carry_ref):
    ti = pl.program_id(1)

    @pl.when(ti == 0)
    def _():
        carry_ref[...] = prev0_ref[0]

    p = p_ref[0]
    row = lax.broadcasted_iota(jnp.int32, p.shape, 0)
    if p.shape[0] == 1:
        prev = carry_ref[...]
    else:
        prev = jnp.where(row == 0, carry_ref[...], pltpu.roll(p, 1, 0))
    carry_ref[...] = p[p.shape[0] - 1:, :]
    ps = p + (prev - p) * mu_ref[...]
    r = ps[:, 0:W_RWKV]
    k = ps[:, W_RWKV:2 * W_RWKV]
    v = ps[:, 2 * W_RWKV:3 * W_RWKV]
    lora = ps[:, 3 * W_RWKV:3 * W_RWKV + LANES]
    gd = ps[:, 3 * W_RWKV + LANES:]
    dec = jnp.dot(jnp.tanh(lora).astype(jnp.bfloat16), wdec_ref[...], preferred_element_type=jnp.float32)
    logw = -jnp.exp(-_softplus(-(w0_ref[...] + dec)) - 0.5)
    a = _sigmoid(a0_ref[...] + jnp.dot(lora.astype(jnp.bfloat16), waaa_ref[...], preferred_element_type=jnp.float32))
    g = jnp.dot(_sigmoid(gd).astype(jnp.bfloat16), wgate_ref[...], preferred_element_type=jnp.float32)
    kk = k * kk_w_ref[...]
    sumsq = jnp.dot(kk * kk, seg_ref[...], precision=HI, preferred_element_type=jnp.float32)
    kk = kk / jnp.maximum(jnp.sqrt(sumsq), 1e-12)
    k2 = k * (1.0 + (a - 1.0) * ka_ref[...])
    rkk = jnp.dot(r * k2 * rk_ref[...], seg_ref[...], precision=HI, preferred_element_type=jnp.float32)
    r_o[0] = r
    w_o[0] = jnp.exp(logw)
    k_o[0] = k2
    kk_o[0] = kk
    b_o[0] = kk * a
    v_o[0] = v
    g_o[0] = g
    bonus_o[0] = rkk * v


def rwkv_prep(p, prev0, mu, w0, w_dec, a0, w_aaa, w_gate, k_k, k_a, r_k, tt):
    b, t, _ = p.shape
    f32 = jnp.float32
    wdec_pad = jnp.concatenate([w_dec, jnp.zeros_like(w_aaa)], axis=0).astype(jnp.bfloat16)
    waaa_pad = jnp.concatenate([jnp.zeros_like(w_dec), w_aaa], axis=0).astype(jnp.bfloat16)
    head = jnp.arange(W_RWKV) // D_HEAD
    seg = (head[:, None] == head[None, :]).astype(f32)
    row = lambda z: z.reshape(1, -1).astype(f32)
    vec_spec = lambda n: pl.BlockSpec((1, n), lambda bi, ti: (0, 0))
    mat_spec = lambda m, n: pl.BlockSpec((m, n), lambda bi, ti: (0, 0))
    out_spec = pl.BlockSpec((1, tt, W_RWKV), lambda bi, ti: (bi, ti, 0))
    out = jax.ShapeDtypeStruct((b, t, W_RWKV), f32)
    return pl.pallas_call(
        _rwkv_prep_kernel,
        grid=(b, t // tt),
        in_specs=[
            pl.BlockSpec((1, tt, W_RWKV_IN), lambda bi, ti: (bi, ti, 0)),
            pl.BlockSpec((1, 1, W_RWKV_IN), lambda bi, ti: (bi, 0, 0)),
            vec_spec(W_RWKV_IN), vec_spec(W_RWKV), mat_spec(LANES, W_RWKV), vec_spec(W_RWKV),
            mat_spec(LANES, W_RWKV), mat_spec(D_GATE_LORA, W_RWKV), vec_spec(W_RWKV), vec_spec(W_RWKV),
            vec_spec(W_RWKV), mat_spec(W_RWKV, W_RWKV),
        ],
        out_specs=[out_spec] * 8,
        out_shape=[out] * 8,
        scratch_shapes=[pltpu.VMEM((1, W_RWKV_IN), f32)],
        compiler_params=pltpu.CompilerParams(dimension_semantics=("parallel", "arbitrary"),
                                             vmem_limit_bytes=VMEM_LIMIT),
        name="rwkv_prep",
    )(p, prev0.reshape(b, 1, W_RWKV_IN), row(mu), row(w0), wdec_pad, row(a0), waaa_pad, w_gate.astype(jnp.bfloat16),
      row(k_k), row(k_a), row(r_k), seg)


def _rwkv_scan_kernel(r_ref, w_ref, k_ref, kk_ref, b_ref, v_ref, s0_ref, y_ref, sT_ref, s_ref):
    ci = pl.program_id(1)
    nb, tc = r_ref.shape[0], r_ref.shape[1]
    tiles = [(bi, p) for bi in range(nb) for p in range(N_PAIR)]

    @pl.when(ci == 0)
    def _():
        for i, (bi, p) in enumerate(tiles):
            s_ref[i] = s0_ref[bi, p]

    lane = lax.broadcasted_iota(jnp.int32, (D_HEAD, LANES), 1)
    sub = lax.broadcasted_iota(jnp.int32, (D_HEAD, LANES), 0)
    first = lane < D_HEAD
    eye2 = (sub == (lane % D_HEAD))
    sub128 = lax.broadcasted_iota(jnp.int32, (LANES, LANES), 0)
    lane128 = lax.broadcasted_iota(jnp.int32, (LANES, LANES), 1)
    eye128 = sub128 == lane128
    seg = jnp.where((sub128 // D_HEAD) == (lane128 // D_HEAD), 1.0, 0.0).astype(jnp.bfloat16)

    def head_sums(xs):
        parts = []
        for x in xs:
            hi = x.astype(jnp.bfloat16)
            parts += [hi, (x - hi.astype(jnp.float32)).astype(jnp.bfloat16)]
        both = jnp.dot(jnp.concatenate(parts, axis=0), seg, preferred_element_type=jnp.float32)
        return [both[i * LANES:i * LANES + D_HEAD] + both[i * LANES + D_HEAD:(i + 1) * LANES] for i in range(len(xs))]

    grp = min(tc, SUBLANES)

    def steps(gi, carry):
        rows = pl.ds(pl.multiple_of(gi * grp, grp), grp)
        lanes = lambda p: slice(p * LANES, (p + 1) * LANES)
        load = lambda ref: [ref[bi, rows, lanes(p)] for bi, p in tiles]
        r_g, w_g, k_g, kk_g, b_g, v_g = load(r_ref), load(w_ref), load(k_ref), load(kk_ref), load(b_ref), load(v_ref)
        states = [s_ref[i] for i in range(len(tiles))]
        yrows = [[] for _ in tiles]
        for j in range(grp):
            row = slice(j, j + 1)
            sks = head_sums([s * kk[row] for s, kk in zip(states, kk_g)])
            for i in range(len(tiles)):
                vcol = jnp.sum(jnp.where(eye128, v_g[i][row], 0.0), axis=1, keepdims=True)
                vmat = jnp.where(first, vcol[0:D_HEAD], vcol[D_HEAD:LANES])
                states[i] = states[i] * w_g[i][row] - sks[i] * b_g[i][row] + vmat * k_g[i][row]
            ys = head_sums([s * r[row] for s, r in zip(states, r_g)])
            for i in range(len(tiles)):
                yrows[i].append(jnp.sum(jnp.where(eye2, ys[i], 0.0), axis=0, keepdims=True))
        for i, (bi, p) in enumerate(tiles):
            y_ref[bi, rows, lanes(p)] = yrows[i][0] if grp == 1 else jnp.concatenate(yrows[i], axis=0)
            s_ref[i] = states[i]
        return carry

    lax.fori_loop(0, tc // grp, steps, 0)

    @pl.when(ci == pl.num_programs(1) - 1)
    def _():
        for i, (bi, p) in enumerate(tiles):
            sT_ref[bi, p] = s_ref[i]


def rwkv_scan(r, w, k, kk, bb, v, s0, nb, tc):
    b, t, _ = r.shape
    f32 = jnp.float32
    pair = lambda s: s.reshape(b, N_PAIR, 2, D_HEAD, D_HEAD).transpose(0, 1, 3, 2, 4).reshape(b, N_PAIR, D_HEAD, LANES)
    unpair = lambda s: s.reshape(b, N_PAIR, D_HEAD, 2, D_HEAD).transpose(0, 1, 3, 2, 4).reshape(b, H_RWKV, D_HEAD, D_HEAD)
    seq_spec = pl.BlockSpec((nb, tc, W_RWKV), lambda bi, ci: (bi, ci, 0))
    st_spec = pl.BlockSpec((nb, N_PAIR, D_HEAD, LANES), lambda bi, ci: (bi, 0, 0, 0))
    y, s_fin = pl.pallas_call(
        _rwkv_scan_kernel,
        grid=(b // nb, t // tc),
        in_specs=[seq_spec] * 6 + [st_spec],
        out_specs=[seq_spec, st_spec],
        out_shape=[jax.ShapeDtypeStruct((b, t, W_RWKV), f32), jax.ShapeDtypeStruct((b, N_PAIR, D_HEAD, LANES), f32)],
        scratch_shapes=[pltpu.VMEM((nb * N_PAIR, D_HEAD, LANES), f32)],
        compiler_params=pltpu.CompilerParams(dimension_semantics=("parallel", "arbitrary"),
                                             vmem_limit_bytes=VMEM_LIMIT),
        name="rwkv_scan",
    )(r, w, k, kk, bb, v, pair(s0.astype(f32)))
    return y, unpair(s_fin)


def _merge_kernel(y_ref, bonus_ref, g_ref, gng_ref, gnb_ref, seg_ref, a_ref, x_ref, wr_ref, wa_ref, nf_ref, h_o, xn_o):
    y = y_ref[...]
    inv = 1.0 / D_HEAD
    mean = jnp.dot(y, seg_ref[...], precision=HI, preferred_element_type=jnp.float32) * inv
    dlt = y - mean
    var = jnp.dot(dlt * dlt, seg_ref[...], precision=HI, preferred_element_type=jnp.float32) * inv
    yn = dlt * lax.rsqrt(var + GN_EPS) * gng_ref[...] + gnb_ref[...]
    r_out = (yn + bonus_ref[...]) * g_ref[...]
    h = (x_ref[...] + jnp.dot(r_out.astype(jnp.bfloat16), wr_ref[...], preferred_element_type=jnp.float32)
         + jnp.dot(a_ref[...].astype(jnp.bfloat16), wa_ref[...], preferred_element_type=jnp.float32))
    h_o[...] = h
    xn_o[...] = _rms_norm(h, nf_ref[...]).astype(jnp.bfloat16)


def merge(y, bonus, g, gn_g, gn_b, a_out, x, w_out_bf16, norm_ffn, tt):
    n, d = x.shape
    head = jnp.arange(W_RWKV) // D_HEAD
    seg = (head[:, None] == head[None, :]).astype(jnp.float32)
    row = lambda w: pl.BlockSpec((tt, w), lambda i: (i, 0))
    vec = lambda w: pl.BlockSpec((1, w), lambda i: (0, 0))
    mat = lambda r, c: pl.BlockSpec((r, c), lambda i: (0, 0))
    return pl.pallas_call(
        _merge_kernel,
        grid=(n // tt,),
        in_specs=[row(W_RWKV), row(W_RWKV), row(W_RWKV), vec(W_RWKV), vec(W_RWKV), mat(W_RWKV, W_RWKV),
                  row(W_ATT), row(d), mat(W_RWKV, d), mat(W_ATT, d), vec(d)],
        out_specs=[row(d), row(d)],
        out_shape=[jax.ShapeDtypeStruct((n, d), jnp.float32), jax.ShapeDtypeStruct((n, d), jnp.bfloat16)],
        compiler_params=pltpu.CompilerParams(dimension_semantics=("parallel",), vmem_limit_bytes=VMEM_LIMIT),
        name="merge",
    )(y, bonus, g, gn_g.reshape(1, -1), gn_b.reshape(1, -1), seg, a_out, x, w_out_bf16[:W_RWKV], w_out_bf16[W_RWKV:],
      norm_ffn.reshape(1, d))


def moba_attend(q, k_own, v_own, own_mask, k_sel, v_sel, sel_mask):
    scale = D_HEAD ** -0.5
    s_own = jnp.einsum('bqhd,bkhd->bqhk', q, k_own).astype(jnp.float32) * scale
    s_own = jnp.where(own_mask[None, :, None, :], s_own, -jnp.inf)
    if k_sel is None:
        p = jax.nn.softmax(s_own, axis=-1).astype(v_own.dtype)
        return jnp.einsum('bqhk,bkhd->bqhd', p, v_own)
    s_sel = jnp.einsum('bqhd,bqhskd->bqhsk', q, k_sel).astype(jnp.float32) * scale
    s_sel = jnp.where(sel_mask[..., None], s_sel, -jnp.inf)
    n_s = s_sel.shape[3] * s_sel.shape[4]
    scores = jnp.concatenate([s_sel.reshape(s_sel.shape[:3] + (n_s,)), s_own], axis=-1)
    p = jax.nn.softmax(scores, axis=-1).astype(v_own.dtype)
    p_sel = p[..., :n_s].reshape(s_sel.shape)
    return jnp.einsum('bqhsk,bqhskd->bqhd', p_sel, v_sel) + jnp.einsum('bqhk,bkhd->bqhd', p[..., n_s:], v_own)


def _moba_prompt_kernel(q_ref, k_ref, v_ref, o_ref, vt_ref, kb_ref, sel_ref, m_ref, l_ref, acc_ref):
    qi = pl.program_id(2)
    t_len = k_ref.shape[1]
    nb = t_len // MOBA_BLOCK
    scale = D_HEAD ** -0.5

    @pl.when(qi == 0)
    def _():
        vt_ref[...] = v_ref[0].T.astype(jnp.bfloat16)
        kb_ref[...] = k_ref[0].astype(jnp.bfloat16)

    q2 = q_ref[0]
    lane = lax.broadcasted_iota(jnp.int32, q2.shape, 1)
    kmean = jnp.mean(k_ref[0].reshape(nb, MOBA_BLOCK, LANES), axis=1)
    blk_row = lax.broadcasted_iota(jnp.int32, (nb, MOBA_BLOCK), 0)
    key_row = lax.broadcasted_iota(jnp.int32, (MOBA_BLOCK, MOBA_BLOCK), 0)
    qry_col = lax.broadcasted_iota(jnp.int32, (MOBA_BLOCK, MOBA_BLOCK), 1)
    qms = []
    for s in range(2):
        qm = jnp.where((lane // D_HEAD) == s, q2, 0.0)
        qms.append(qm.astype(jnp.bfloat16))
        gate = lax.dot_general(kmean, qm, NT_DIMS, precision=lax.Precision.HIGHEST,
                               preferred_element_type=jnp.float32)
        rank = jnp.zeros(gate.shape, jnp.float32)
        for jp in range(nb):
            g_jp = gate[jp:jp + 1, :]
            ahead = (g_jp > gate) | ((g_jp == gate) & (jp < blk_row))
            rank = rank + jnp.where(ahead & (jp < qi), 1.0, 0.0)
        sel_ref[s] = jnp.where((rank < float(MOBA_TOPK)) & (blk_row < qi), 1.0, 0.0)

        own = pl.ds(pl.multiple_of(qi * MOBA_BLOCK, MOBA_BLOCK), MOBA_BLOCK)
        st = lax.dot_general(kb_ref[own, :], qms[s], NT_DIMS, preferred_element_type=jnp.float32) * scale
        st = jnp.where(key_row <= qry_col, st, NEG_INF)
        m0 = jnp.max(st, axis=0, keepdims=True)
        p = jnp.exp(st - m0)
        m_ref[s] = m0
        l_ref[s] = jnp.sum(p, axis=0, keepdims=True)
        acc_ref[s] = jnp.dot(vt_ref[:, own], p.astype(jnp.bfloat16), preferred_element_type=jnp.float32)

    def past_block(j, carry):
        blk = pl.ds(pl.multiple_of(j * MOBA_BLOCK, MOBA_BLOCK), MOBA_BLOCK)
        k_j = kb_ref[blk, :]
        vt_j = vt_ref[:, blk]
        for s in range(2):
            st = lax.dot_general(k_j, qms[s], NT_DIMS, preferred_element_type=jnp.float32) * scale
            st = jnp.where(sel_ref[s, pl.ds(j, 1), :] > 0.0, st, NEG_INF)
            m_old = m_ref[s]
            m_new = jnp.maximum(m_old, jnp.max(st, axis=0, keepdims=True))
            alpha = jnp.exp(m_old - m_new)
            p = jnp.exp(st - m_new)
            m_ref[s] = m_new
            l_ref[s] = alpha * l_ref[s] + jnp.sum(p, axis=0, keepdims=True)
            acc_ref[s] = alpha * acc_ref[s] + jnp.dot(vt_j, p.astype(jnp.bfloat16), preferred_element_type=jnp.float32)
        return carry

    lax.fori_loop(0, qi, past_block, 0)

    row = lax.broadcasted_iota(jnp.int32, (LANES, MOBA_BLOCK), 0)
    out_t = jnp.where((row // D_HEAD) == 0, acc_ref[0] / l_ref[0], acc_ref[1] / l_ref[1])
    o_ref[0] = out_t.T


def moba_prompt(q, k, v):
    b, t, w = q.shape
    nb = t // MOBA_BLOCK
    return pl.pallas_call(
        _moba_prompt_kernel,
        grid=(b, w // LANES, nb),
        in_specs=[
            pl.BlockSpec((1, MOBA_BLOCK, LANES), lambda bi, hp, qi: (bi, qi, hp)),
            pl.BlockSpec((1, t, LANES), lambda bi, hp, qi: (bi, 0, hp)),
            pl.BlockSpec((1, t, LANES), lambda bi, hp, qi: (bi, 0, hp)),
        ],
        out_specs=pl.BlockSpec((1, MOBA_BLOCK, LANES), lambda bi, hp, qi: (bi, qi, hp)),
        out_shape=jax.ShapeDtypeStruct((b, t, w), jnp.float32),
        scratch_shapes=[
            pltpu.VMEM((LANES, t), jnp.bfloat16),
            pltpu.VMEM((t, LANES), jnp.bfloat16),
            pltpu.VMEM((2, nb, MOBA_BLOCK), jnp.float32),
            pltpu.VMEM((2, 1, MOBA_BLOCK), jnp.float32),
            pltpu.VMEM((2, 1, MOBA_BLOCK), jnp.float32),
            pltpu.VMEM((2, LANES, MOBA_BLOCK), jnp.float32),
        ],
        compiler_params=pltpu.CompilerParams(dimension_semantics=("parallel", "parallel", "arbitrary"),
                                             vmem_limit_bytes=VMEM_LIMIT),
        name="moba_prompt",
    )(q, k, v)


def moba_sample(q, k, v, cache_k, cache_v, page_table):
    bd, s, h, dh = q.shape
    ppb = MOBA_BLOCK // PAGE_SIZE
    n_full = PAST_LEN // MOBA_BLOCK
    rem = PAST_LEN - n_full * MOBA_BLOCK
    n_sel = min(MOBA_TOPK, n_full)
    k_own, v_own = k, v
    own_mask = jnp.arange(rem + s)[None, :] <= (rem + jnp.arange(s))[:, None]
    pt_full = page_table[:, :n_full * ppb]
    kmean = jnp.mean(cache_k[pt_full].reshape(bd, n_full, MOBA_BLOCK, h, dh).astype(jnp.float32), axis=2)
    gate = jnp.einsum('bshd,bnhd->bshn', q.astype(jnp.float32), kmean)
    _, sel = lax.top_k(gate, n_sel)
    logical = sel[..., None] * ppb + jnp.arange(ppb)
    phys = page_table[jnp.arange(bd)[:, None, None, None, None], logical]
    hi = jnp.arange(h)[None, None, :, None, None]
    k_sel = cache_k[phys, :, hi, :].reshape(bd, s, h, n_sel, MOBA_BLOCK, dh).astype(k.dtype)
    v_sel = cache_v[phys, :, hi, :].reshape(bd, s, h, n_sel, MOBA_BLOCK, dh).astype(v.dtype)
    ok = jnp.ones((bd, s, h, n_sel), dtype=bool)
    return moba_attend(q, k_own, v_own, own_mask, k_sel, v_sel, ok)


PEER_CANDS = [(a, b) for a in range(PEER_TOPK) for b in range(PEER_TOPK) if (a + 1) * (b + 1) <= PEER_TOPK]
PEER_NCAND = -(-len(PEER_CANDS) // 8) * 8


def _extract_topk(x, n_rows, k):
    rows = lax.broadcasted_iota(jnp.int32, x.shape, 0).astype(jnp.float32)
    vals = []
    for _ in range(k):
        m = jnp.max(x, axis=0, keepdims=True)
        first = jnp.min(jnp.where(x == m, rows, float(n_rows)), axis=0, keepdims=True)
        x = jnp.where(rows == first, NEG_INF, x)
        vals.append(m)
    return x, vals


def _peer_route_kernel(xn_ref, wpq_ref, keys_ref, t1_ref, s2_ref, e1_ref, e2_ref, sv_ref, comb_ref):
    q = jnp.dot(xn_ref[...], wpq_ref[...], preferred_element_type=jnp.float32)
    half = PEER_DQ // 2
    for h in range(PEER_HEADS):
        masked = []
        for p in range(2):
            qs = q[:, (2 * h + p) * half:(2 * h + p + 1) * half].astype(jnp.bfloat16)
            st = lax.dot_general(keys_ref[p], qs, (((1,), (1,)), ((), ())),
                                 preferred_element_type=jnp.float32)
            rem, vals = _extract_topk(st, PEER_NKEYS, PEER_TOPK)
            for i, v in enumerate(vals):
                sv_ref[p, i:i + 1, :] = v
            masked.append(jnp.where(rem == NEG_INF, st, NEG_INF))
        comb_ref[...] = jnp.full(comb_ref.shape, NEG_INF, jnp.float32)
        for c, (a, b) in enumerate(PEER_CANDS):
            comb_ref[c:c + 1, :] = sv_ref[0, a:a + 1, :] + sv_ref[1, b:b + 1, :]
        _, cvals = _extract_topk(comb_ref[...], PEER_NCAND, PEER_TOPK + 1)
        cmax = cvals[0]
        z = jnp.zeros_like(cmax)
        for v in cvals[:PEER_TOPK]:
            z = z + jnp.exp(v - cmax)
        cut = 0.5 * (cvals[PEER_TOPK - 1] + cvals[PEER_TOPK])
        t1_ref[h] = cut - masked[0]
        s2_ref[h] = masked[1]
        e1_ref[h] = jnp.exp(masked[0] - sv_ref[0, 0:1, :])
        e2_ref[h] = jnp.exp(masked[1] - sv_ref[1, 0:1, :]) / z


def peer_route(xn_bf16, wpq_bf16, keys_bf16, tm):
    n = xn_bf16.shape[0]
    tab = jax.ShapeDtypeStruct((PEER_HEADS, PEER_NKEYS, n), jnp.float32)
    tab_spec = pl.BlockSpec((PEER_HEADS, PEER_NKEYS, tm), lambda i: (0, 0, i))
    return pl.pallas_call(
        _peer_route_kernel,
        grid=(n // tm,),
        in_specs=[
            pl.BlockSpec((tm, D_MODEL), lambda i: (i, 0)),
            pl.BlockSpec((D_MODEL, PEER_HEADS * PEER_DQ), lambda i: (0, 0)),
            pl.BlockSpec((2, PEER_NKEYS, PEER_DQ // 2), lambda i: (0, 0, 0)),
        ],
        out_specs=[tab_spec] * 4,
        out_shape=[tab] * 4,
        scratch_shapes=[pltpu.VMEM((2, PEER_TOPK, tm), jnp.float32),
                        pltpu.VMEM((PEER_NCAND, tm), jnp.float32)],
        compiler_params=pltpu.CompilerParams(dimension_semantics=("parallel",),
                                             vmem_limit_bytes=VMEM_LIMIT),
        name="peer_route",
    )(xn_bf16, wpq_bf16, keys_bf16)


def _gelu_exact(x):
    return 0.5 * x * (1.0 + lax.erf(x * (1.0 / math.sqrt(2.0))))


def _peer_dense_kernel(xn_ref, u_ref, vt_ref, t1_ref, s2_ref, e1_ref, e2_ref, h_ref, nfin_ref, o_ref, acc_ref, p_ref):
    j = pl.program_id(1)
    n_e1 = u_ref.shape[0] // PEER_NKEYS
    tm = xn_ref.shape[0]

    @pl.when(j == 0)
    def _():
        acc_ref[...] = jnp.zeros_like(acc_ref)

    for eb in range(n_e1 // PEER_ELB):
        erows = slice(eb * PEER_ELB * PEER_NKEYS, (eb + 1) * PEER_ELB * PEER_NKEYS)
        at = lax.dot_general(u_ref[erows, :], xn_ref[...], NT_DIMS,
                             preferred_element_type=jnp.float32)
        for c in range(tm // LANES):
            cols = slice(c * LANES, (c + 1) * LANES)
            for r0 in range(0, PEER_NKEYS, PEER_SGR):
                accs = [jnp.zeros((PEER_SGR, LANES), jnp.float32) for _ in range(PEER_ELB)]
                for h in range(PEER_HEADS):
                    s2t = s2_ref[h, r0:r0 + PEER_SGR, cols]
                    e2t = e2_ref[h, r0:r0 + PEER_SGR, cols]
                    for i in range(PEER_ELB):
                        el = eb * PEER_ELB + i
                        picked = s2t >= t1_ref[h, el:el + 1, cols]
                        accs[i] = accs[i] + jnp.where(picked, e2t, 0.0) * e1_ref[h, el:el + 1, cols]
                for i in range(PEER_ELB):
                    lo = i * PEER_NKEYS + r0
                    act = _gelu_exact(at[lo:lo + PEER_SGR, cols])
                    p_ref[eb, lo:lo + PEER_SGR, cols] = (accs[i] * act).astype(jnp.bfloat16)
        acc_ref[...] += jnp.dot(vt_ref[:, erows], p_ref[eb], preferred_element_type=jnp.float32)

    @pl.when(j == pl.num_programs(1) - 1)
    def _():
        o_ref[...] = _rms_norm(h_ref[...] + acc_ref[...].T, nfin_ref[...])


def peer_dense(xn_bf16, u_bf16, vt_bf16, t1, s2, e1, e2, h, norm_final, tm, te):
    n = xn_bf16.shape[0]
    n_e1 = te // PEER_NKEYS
    return pl.pallas_call(
        _peer_dense_kernel,
        grid=(n // tm, PEER_EXPERTS // te),
        in_specs=[
            pl.BlockSpec((tm, D_MODEL), lambda i, j: (i, 0)),
            pl.BlockSpec((te, D_MODEL), lambda i, j: (j, 0)),
            pl.BlockSpec((D_MODEL, te), lambda i, j: (0, j)),
            pl.BlockSpec((PEER_HEADS, n_e1, tm), lambda i, j: (0, j, i)),
            pl.BlockSpec((PEER_HEADS, PEER_NKEYS, tm), lambda i, j: (0, 0, i)),
            pl.BlockSpec((PEER_HEADS, n_e1, tm), lambda i, j: (0, j, i)),
            pl.BlockSpec((PEER_HEADS, PEER_NKEYS, tm), lambda i, j: (0, 0, i)),
            pl.BlockSpec((tm, D_MODEL), lambda i, j: (i, 0)),
            pl.BlockSpec((1, D_MODEL), lambda i, j: (0, 0)),
        ],
        out_specs=pl.BlockSpec((tm, D_MODEL), lambda i, j: (i, 0)),
        out_shape=jax.ShapeDtypeStruct((n, D_MODEL), jnp.float32),
        scratch_shapes=[pltpu.VMEM((D_MODEL, tm), jnp.float32),
                        pltpu.VMEM((n_e1 // PEER_ELB, PEER_ELB * PEER_NKEYS, tm), jnp.bfloat16)],
        compiler_params=pltpu.CompilerParams(dimension_semantics=("parallel", "arbitrary"),
                                             vmem_limit_bytes=VMEM_LIMIT),
        name="peer_dense",
    )(xn_bf16, u_bf16, vt_bf16, t1, s2, e1, e2, h, norm_final.reshape(1, D_MODEL))


def peer_block(h, xn_bf16, wpq_bf16, keys_bf16, u_bf16, vt_bf16, norm_final):
    n = h.shape[0]
    tm = PEER_TM if n >= PEER_TM else PEER_TM_SMALL
    pad = -n % tm
    hp, xb = jnp.pad(h, ((0, pad), (0, 0))), jnp.pad(xn_bf16, ((0, pad), (0, 0)))
    t1, s2, e1, e2 = peer_route(xb, wpq_bf16, keys_bf16, PEER_TM_SMALL)
    return peer_dense(xb, u_bf16, vt_bf16, t1, s2, e1, e2, hp, norm_final, tm, PEER_TE)[:n]


def layer(x, pos, prev0, s0, attend, norm_mix, w_in_bf16, rw, w_out_bf16, norm_ffn, peer, norm_final):
    mu, w0, w_dec, a0, w_aaa, w_gate, k_k, k_a, r_k, gn_g, gn_b = rw
    b, t, d = x.shape
    n = b * t
    if t == 1:
        xr, posr = x.reshape(1, b, d), jnp.broadcast_to(pos, (b,))
    else:
        xr, posr = x, pos
    seq = lambda z: z.reshape(b, t, z.shape[-1])
    pr, q, k, v = map(seq, project(xr, norm_mix, w_in_bf16, posr, min(xr.shape[1], PROJ_TT)))
    r, w, k2, kk, bb, vv, g, bonus = rwkv_prep(pr, prev0, mu, w0, w_dec, a0, w_aaa, w_gate, k_k, k_a, r_k, min(t, 256))
    y, s_new = rwkv_scan(r, w, k2, kk, bb, vv, s0, RWKV_NB, min(t, 128))
    a_out = attend(q, k, v)
    flat = lambda z: z.reshape(n, z.shape[-1])
    h, xn = merge(flat(y), flat(bonus), flat(g), gn_g, gn_b, flat(a_out), flat(x), w_out_bf16, norm_ffn, min(n, 512))
    out = peer_block(h, xn, *peer, norm_final)
    heads = lambda z: z.reshape(b, t, H_ATT, D_HEAD)
    return out.reshape(b, t, d), heads(k), heads(v), s_new, pr[:, -1]


def kernel(x_prompt, x_sample, cache_k, cache_v, page_table, state_wkv, state_shift, norm_mix, w_in, mu_shift, w0, w_decay_up, a0, w_aaa_up, w_gate_up, k_k, k_a, r_k, gn_gain, gn_bias, w_out, norm_ffn, w_pq, peer_sub_keys, expert_u, expert_v, norm_final):
    l = 0
    bf16 = jnp.bfloat16
    rw = (mu_shift[l], w0[l], w_decay_up[l], a0[l], w_aaa_up[l], w_gate_up[l], k_k[l], k_a[l], r_k[l], gn_gain[l], gn_bias[l])
    peer = (w_pq[l].astype(bf16), peer_sub_keys[l].astype(bf16), expert_u[l].astype(bf16), expert_v[l].T.astype(bf16))
    shared = (norm_mix[l], w_in[l].astype(bf16), rw, w_out[l].astype(bf16), norm_ffn[l], peer, norm_final)
    bp, tp, _ = x_prompt.shape
    bs, ts, _ = x_sample.shape
    y_p, k_p, v_p, s_p, sh_p = layer(
        x_prompt, jnp.arange(tp, dtype=jnp.int32), jnp.zeros((bp, W_RWKV_IN), jnp.float32),
        jnp.zeros((bp, H_RWKV, D_HEAD, D_HEAD), jnp.float32), moba_prompt, *shared)
    sample_attend = lambda q, k, v: moba_sample(
        q.reshape(bs, ts, H_ATT, D_HEAD), k.reshape(bs, ts, H_ATT, D_HEAD), v.reshape(bs, ts, H_ATT, D_HEAD),
        cache_k[l], cache_v[l], page_table).reshape(bs, ts, W_ATT)
    y_s, k_s, v_s, s_s, sh_s = layer(
        x_sample, PAST_LEN + jnp.arange(ts, dtype=jnp.int32), state_shift[l], state_wkv[l], sample_attend, *shared)
    return (y_p, y_s, k_p[None], v_p[None], s_p[None], sh_p[None], k_s[None], v_s[None], s_s[None], sh_s[None])
```

```python
import functools
import math

import jax
import jax.numpy as jnp
from jax import lax
from jax.experimental import pallas as pl
from jax.experimental.pallas import tpu as pltpu

D_MODEL = 1024
DEPTH = 1
PAST_LEN = 16384
PAGE_SIZE = 128
D_HEAD = 64
H_RWKV = 8
H_ATT = 8
W_RWKV = H_RWKV * D_HEAD
W_ATT = H_ATT * D_HEAD
D_DECAY_LORA = 64
D_AAA_LORA = 64
D_GATE_LORA = 128
W_RWKV_IN = 3 * W_RWKV + D_DECAY_LORA + D_AAA_LORA + D_GATE_LORA
W_IN = W_RWKV_IN + 3 * W_ATT
GN_EPS = 64e-5
NORM_EPS = 1e-6
MOBA_BLOCK = 256
MOBA_TOPK = 3
Q_CHUNK = 64
ROT_DIM = D_HEAD // 4
ROPE_THETA = 500000.0
PEER_HEADS = 8
PEER_NKEYS = 128
PEER_TOPK = 16
PEER_DQ = 256
PEER_EXPERTS = PEER_NKEYS * PEER_NKEYS
PEER_TM = 512
PEER_TM_SMALL = 256
PEER_TE = 1024
PEER_ELB = 4
PEER_SGR = 32
NEG_INF = float("-inf")
VMEM_LIMIT = 56 * 1024 * 1024
LANES = 128
SUBLANES = 8
PROJ_TT = 512
RWKV_NB = 4
NT_DIMS = (((1,), (1,)), ((), ()))


def _rms_norm(x, g):
    return x * lax.rsqrt(jnp.mean(x * x, axis=-1, keepdims=True) + NORM_EPS) * g


def rope_tables(pos):
    half = ROT_DIM // 2
    inv = 1.0 / (ROPE_THETA ** (jnp.arange(half, dtype=jnp.float32) * 2.0 / ROT_DIM))
    ang = pos.astype(jnp.float32)[:, None] * inv[None, :]
    lane = jnp.arange(LANES) % D_HEAD
    idx = lane % half
    cos = jnp.where(lane[None, :] < ROT_DIM, jnp.cos(ang)[:, idx], 1.0)
    sin = jnp.sin(ang)[:, idx]
    sin_lo = jnp.where(lane[None, :] < half, -sin, 0.0)
    sin_hi = jnp.where((lane[None, :] >= half) & (lane[None, :] < ROT_DIM), sin, 0.0)
    return cos, sin_lo, sin_hi


def _project_kernel(x_ref, g_ref, w_ref, cos_ref, slo_ref, shi_ref, pr_o, q_o, k_o, v_o):
    xn = _rms_norm(x_ref[0], g_ref[...])
    p = jnp.dot(xn.astype(jnp.bfloat16), w_ref[...], preferred_element_type=jnp.float32)
    pr_o[0] = p[:, :W_RWKV_IN]
    reps = W_ATT // LANES
    cos = jnp.concatenate([cos_ref[...]] * reps, axis=1)
    slo = jnp.concatenate([slo_ref[...]] * reps, axis=1)
    shi = jnp.concatenate([shi_ref[...]] * reps, axis=1)
    half = ROT_DIM // 2

    def rope(z):
        ahead = pltpu.roll(z, W_ATT - half, 1)
        behind = pltpu.roll(z, half, 1)
        return z * cos + ahead * slo + behind * shi

    q_o[0] = rope(p[:, W_RWKV_IN:W_RWKV_IN + W_ATT])
    k_o[0] = rope(p[:, W_RWKV_IN + W_ATT:W_RWKV_IN + 2 * W_ATT])
    v_o[0] = p[:, W_RWKV_IN + 2 * W_ATT:]


def project(x, norm_g, w_in_bf16, pos, tt):
    b, t, d = x.shape
    cos, slo, shi = rope_tables(pos)
    f32 = jnp.float32
    tab_spec = pl.BlockSpec((tt, LANES), lambda bi, ti: (ti, 0))
    out_spec = lambda w: pl.BlockSpec((1, tt, w), lambda bi, ti: (bi, ti, 0))
    return pl.pallas_call(
        _project_kernel,
        grid=(b, t // tt),
        in_specs=[
            pl.BlockSpec((1, tt, d), lambda bi, ti: (bi, ti, 0)),
            pl.BlockSpec((1, d), lambda bi, ti: (0, 0)),
            pl.BlockSpec((d, W_IN), lambda bi, ti: (0, 0)),
            tab_spec, tab_spec, tab_spec,
        ],
        out_specs=[out_spec(W_RWKV_IN), out_spec(W_ATT), out_spec(W_ATT), out_spec(W_ATT)],
        out_shape=[jax.ShapeDtypeStruct((b, t, W_RWKV_IN), f32)] + [jax.ShapeDtypeStruct((b, t, W_ATT), f32)] * 3,
        compiler_params=pltpu.CompilerParams(dimension_semantics=("parallel", "parallel"),
                                             vmem_limit_bytes=VMEM_LIMIT),
        name="project",
    )(x, norm_g.reshape(1, d), w_in_bf16, cos, slo, shi)


N_PAIR = W_RWKV // LANES
HI = lax.Precision.HIGHEST


def _sigmoid(x):
    return 1.0 / (1.0 + jnp.exp(-x))


def _softplus(x):
    return jnp.maximum(x, 0.0) + jnp.log(1.0 + jnp.exp(-jnp.abs(x)))


def _rwkv_prep_kernel(p_ref, prev0_ref, mu_ref, w0_ref, wdec_ref, a0_ref, waaa_ref, wgate_ref, kk_w_ref, ka_ref,
                      rk_ref, seg_ref, r_o, w_o, k_o, kk_o, b_o, v_o, g_o, bonus_o, carry_ref):
    ti = pl.program_id(1)

    @pl.when(ti == 0)
    def _():
        carry_ref[...] = prev0_ref[0]

    p = p_ref[0]
    row = lax.broadcasted_iota(jnp.int32, p.shape, 0)
    if p.shape[0] == 1:
        prev = carry_ref[...]
    else:
        prev = jnp.where(row == 0, carry_ref[...], pltpu.roll(p, 1, 0))
    carry_ref[...] = p[p.shape[0] - 1:, :]
    ps = p + (prev - p) * mu_ref[...]
    r = ps[:, 0:W_RWKV]
    k = ps[:, W_RWKV:2 * W_RWKV]
    v = ps[:, 2 * W_RWKV:3 * W_RWKV]
    lora = ps[:, 3 * W_RWKV:3 * W_RWKV + LANES]
    gd = ps[:, 3 * W_RWKV + LANES:]
    dec = jnp.dot(jnp.tanh(lora).astype(jnp.bfloat16), wdec_ref[...], preferred_element_type=jnp.float32)
    logw = -jnp.exp(-_softplus(-(w0_ref[...] + dec)) - 0.5)
    a = _sigmoid(a0_ref[...] + jnp.dot(lora.astype(jnp.bfloat16), waaa_ref[...], preferred_element_type=jnp.float32))
    g = jnp.dot(_sigmoid(gd).astype(jnp.bfloat16), wgate_ref[...], preferred_element_type=jnp.float32)
    kk = k * kk_w_ref[...]
    sumsq = jnp.dot(kk * kk, seg_ref[...], precision=HI, preferred_element_type=jnp.float32)
    kk = kk / jnp.maximum(jnp.sqrt(sumsq), 1e-12)
    k2 = k * (1.0 + (a - 1.0) * ka_ref[...])
    rkk = jnp.dot(r * k2 * rk_ref[...], seg_ref[...], precision=HI, preferred_element_type=jnp.float32)
    r_o[0] = r
    w_o[0] = jnp.exp(logw)
    k_o[0] = k2
    kk_o[0] = kk
    b_o[0] = kk * a
    v_o[0] = v
    g_o[0] = g
    bonus_o[0] = rkk * v


def rwkv_prep(p, prev0, mu, w0, w_dec, a0, w_aaa, w_gate, k_k, k_a, r_k, tt):
    b, t, _ = p.shape
    f32 = jnp.float32
    wdec_pad = jnp.concatenate([w_dec, jnp.zeros_like(w_aaa)], axis=0).astype(jnp.bfloat16)
    waaa_pad = jnp.concatenate([jnp.zeros_like(w_dec), w_aaa], axis=0).astype(jnp.bfloat16)
    head = jnp.arange(W_RWKV) // D_HEAD
    seg = (head[:, None] == head[None, :]).astype(f32)
    row = lambda z: z.reshape(1, -1).astype(f32)
    vec_spec = lambda n: pl.BlockSpec((1, n), lambda bi, ti: (0, 0))
    mat_spec = lambda m, n: pl.BlockSpec((m, n), lambda bi, ti: (0, 0))
    out_spec = pl.BlockSpec((1, tt, W_RWKV), lambda bi, ti: (bi, ti, 0))
    out = jax.ShapeDtypeStruct((b, t, W_RWKV), f32)
    return pl.pallas_call(
        _rwkv_prep_kernel,
        grid=(b, t // tt),
        in_specs=[
            pl.BlockSpec((1, tt, W_RWKV_IN), lambda bi, ti: (bi, ti, 0)),
            pl.BlockSpec((1, 1, W_RWKV_IN), lambda bi, ti: (bi, 0, 0)),
            vec_spec(W_RWKV_IN), vec_spec(W_RWKV), mat_spec(LANES, W_RWKV), vec_spec(W_RWKV),
            mat_spec(LANES, W_RWKV), mat_spec(D_GATE_LORA, W_RWKV), vec_spec(W_RWKV), vec_spec(W_RWKV),
            vec_spec(W_RWKV), mat_spec(W_RWKV, W_RWKV),
        ],
        out_specs=[out_spec] * 8,
        out_shape=[out] * 8,
        scratch_shapes=[pltpu.VMEM((1, W_RWKV_IN), f32)],
        compiler_params=pltpu.CompilerParams(dimension_semantics=("parallel", "arbitrary"),
                                             vmem_limit_bytes=VMEM_LIMIT),
        name="rwkv_prep",
    )(p, prev0.reshape(b, 1, W_RWKV_IN), row(mu), row(w0), wdec_pad, row(a0), waaa_pad, w_gate.astype(jnp.bfloat16),
      row(k_k), row(k_a), row(r_k), seg)


def _rwkv_scan_kernel(r_ref, w_ref, k_ref, kk_ref, b_ref, v_ref, s0_ref, y_ref, sT_ref, s_ref):
    ci = pl.program_id(1)
    nb, tc = r_ref.shape[0], r_ref.shape[1]
    tiles = [(bi, p) for bi in range(nb) for p in range(N_PAIR)]

    @pl.when(ci == 0)
    def _():
        for i, (bi, p) in enumerate(tiles):
            s_ref[i] = s0_ref[bi, p]

    lane = lax.broadcasted_iota(jnp.int32, (D_HEAD, LANES), 1)
    sub = lax.broadcasted_iota(jnp.int32, (D_HEAD, LANES), 0)
    first = lane < D_HEAD
    eye2 = (sub == (lane % D_HEAD))
    sub128 = lax.broadcasted_iota(jnp.int32, (LANES, LANES), 0)
    lane128 = lax.broadcasted_iota(jnp.int32, (LANES, LANES), 1)
    eye128 = sub128 == lane128
    seg = jnp.where((sub128 // D_HEAD) == (lane128 // D_HEAD), 1.0, 0.0).astype(jnp.bfloat16)

    def head_sums(xs):
        parts = []
        for x in xs:
            hi = x.astype(jnp.bfloat16)
            parts += [hi, (x - hi.astype(jnp.float32)).astype(jnp.bfloat16)]
        both = jnp.dot(jnp.concatenate(parts, axis=0), seg, preferred_element_type=jnp.float32)
        return [both[i * LANES:i * LANES + D_HEAD] + both[i * LANES + D_HEAD:(i + 1) * LANES] for i in range(len(xs))]

    grp = min(tc, SUBLANES)

    def steps(gi, carry):
        rows = pl.ds(pl.multiple_of(gi * grp, grp), grp)
        lanes = lambda p: slice(p * LANES, (p + 1) * LANES)
        load = lambda ref: [ref[bi, rows, lanes(p)] for bi, p in tiles]
        r_g, w_g, k_g, kk_g, b_g, v_g = load(r_ref), load(w_ref), load(k_ref), load(kk_ref), load(b_ref), load(v_ref)
        states = [s_ref[i] for i in range(len(tiles))]
        yrows = [[] for _ in tiles]
        for j in range(grp):
            row = slice(j, j + 1)
            sks = head_sums([s * kk[row] for s, kk in zip(states, kk_g)])
            for i in range(len(tiles)):
                vcol = jnp.sum(jnp.where(eye128, v_g[i][row], 0.0), axis=1, keepdims=True)
                vmat = jnp.where(first, vcol[0:D_HEAD], vcol[D_HEAD:LANES])
                states[i] = states[i] * w_g[i][row] - sks[i] * b_g[i][row] + vmat * k_g[i][row]
            ys = head_sums([s * r[row] for s, r in zip(states, r_g)])
            for i in range(len(tiles)):
                yrows[i].append(jnp.sum(jnp.where(eye2, ys[i], 0.0), axis=0, keepdims=True))
        for i, (bi, p) in enumerate(tiles):
            y_ref[bi, rows, lanes(p)] = yrows[i][0] if grp == 1 else jnp.concatenate(yrows[i], axis=0)
            s_ref[i] = states[i]
        return carry

    lax.fori_loop(0, tc // grp, steps, 0)

    @pl.when(ci == pl.num_programs(1) - 1)
    def _():
        for i, (bi, p) in enumerate(tiles):
            sT_ref[bi, p] = s_ref[i]


def rwkv_scan(r, w, k, kk, bb, v, s0, nb, tc):
    b, t, _ = r.shape
    f32 = jnp.float32
    pair = lambda s: s.reshape(b, N_PAIR, 2, D_HEAD, D_HEAD).transpose(0, 1, 3, 2, 4).reshape(b, N_PAIR, D_HEAD, LANES)
    unpair = lambda s: s.reshape(b, N_PAIR, D_HEAD, 2, D_HEAD).transpose(0, 1, 3, 2, 4).reshape(b, H_RWKV, D_HEAD, D_HEAD)
    seq_spec = pl.BlockSpec((nb, tc, W_RWKV), lambda bi, ci: (bi, ci, 0))
    st_spec = pl.BlockSpec((nb, N_PAIR, D_HEAD, LANES), lambda bi, ci: (bi, 0, 0, 0))
    y, s_fin = pl.pallas_call(
        _rwkv_scan_kernel,
        grid=(b // nb, t // tc),
        in_specs=[seq_spec] * 6 + [st_spec],
        out_specs=[seq_spec, st_spec],
        out_shape=[jax.ShapeDtypeStruct((b, t, W_RWKV), f32), jax.ShapeDtypeStruct((b, N_PAIR, D_HEAD, LANES), f32)],
        scratch_shapes=[pltpu.VMEM((nb * N_PAIR, D_HEAD, LANES), f32)],
        compiler_params=pltpu.CompilerParams(dimension_semantics=("parallel", "arbitrary"),
                                             vmem_limit_bytes=VMEM_LIMIT),
        name="rwkv_scan",
    )(r, w, k, kk, bb, v, pair(s0.astype(f32)))
    return y, unpair(s_fin)


def _merge_kernel(y_ref, bonus_ref, g_ref, gng_ref, gnb_ref, seg_ref, a_ref, x_ref, wr_ref, wa_ref, nf_ref, h_o, xn_o):
    y = y_ref[...]
    inv = 1.0 / D_HEAD
    mean = jnp.dot(y, seg_ref[...], precision=HI, preferred_element_type=jnp.float32) * inv
    dlt = y - mean
    var = jnp.dot(dlt * dlt, seg_ref[...], precision=HI, preferred_element_type=jnp.float32) * inv
    yn = dlt * lax.rsqrt(var + GN_EPS) * gng_ref[...] + gnb_ref[...]
    r_out = (yn + bonus_ref[...]) * g_ref[...]
    h = (x_ref[...] + jnp.dot(r_out.astype(jnp.bfloat16), wr_ref[...], preferred_element_type=jnp.float32)
         + jnp.dot(a_ref[...].astype(jnp.bfloat16), wa_ref[...], preferred_element_type=jnp.float32))
    h_o[...] = h
    xn_o[...] = _rms_norm(h, nf_ref[...]).astype(jnp.bfloat16)


def merge(y, bonus, g, gn_g, gn_b, a_out, x, w_out_bf16, norm_ffn, tt):
    n, d = x.shape
    head = jnp.arange(W_RWKV) // D_HEAD
    seg = (head[:, None] == head[None, :]).astype(jnp.float32)
    row = lambda w: pl.BlockSpec((tt, w), lambda i: (i, 0))
    vec = lambda w: pl.BlockSpec((1, w), lambda i: (0, 0))
    mat = lambda r, c: pl.BlockSpec((r, c), lambda i: (0, 0))
    return pl.pallas_call(
        _merge_kernel,
        grid=(n // tt,),
        in_specs=[row(W_RWKV), row(W_RWKV), row(W_RWKV), vec(W_RWKV), vec(W_RWKV), mat(W_RWKV, W_RWKV),
                  row(W_ATT), row(d), mat(W_RWKV, d), mat(W_ATT, d), vec(d)],
        out_specs=[row(d), row(d)],
        out_shape=[jax.ShapeDtypeStruct((n, d), jnp.float32), jax.ShapeDtypeStruct((n, d), jnp.bfloat16)],
        compiler_params=pltpu.CompilerParams(dimension_semantics=("parallel",), vmem_limit_bytes=VMEM_LIMIT),
        name="merge",
    )(y, bonus, g, gn_g.reshape(1, -1), gn_b.reshape(1, -1), seg, a_out, x, w_out_bf16[:W_RWKV], w_out_bf16[W_RWKV:],
      norm_ffn.reshape(1, d))


def _moba_prompt_kernel(q_ref, k_ref, v_ref, o_ref, vt_ref, kb_ref, sel_ref, m_ref, l_ref, acc_ref):
    qi = pl.program_id(2)
    t_len = k_ref.shape[1]
    nb = t_len // MOBA_BLOCK
    scale = D_HEAD ** -0.5

    @pl.when(qi == 0)
    def _():
        vt_ref[...] = v_ref[0].T.astype(jnp.bfloat16)
        kb_ref[...] = k_ref[0].astype(jnp.bfloat16)

    q2 = q_ref[0]
    lane = lax.broadcasted_iota(jnp.int32, q2.shape, 1)
    kmean = jnp.mean(k_ref[0].reshape(nb, MOBA_BLOCK, LANES), axis=1)
    blk_row = lax.broadcasted_iota(jnp.int32, (nb, MOBA_BLOCK), 0)
    key_row = lax.broadcasted_iota(jnp.int32, (MOBA_BLOCK, MOBA_BLOCK), 0)
    qry_col = lax.broadcasted_iota(jnp.int32, (MOBA_BLOCK, MOBA_BLOCK), 1)
    qms = []
    for s in range(2):
        qm = jnp.where((lane // D_HEAD) == s, q2, 0.0)
        qms.append(qm.astype(jnp.bfloat16))
        gate = lax.dot_general(kmean, qm, NT_DIMS, precision=lax.Precision.HIGHEST,
                               preferred_element_type=jnp.float32)
        rank = jnp.zeros(gate.shape, jnp.float32)
        for jp in range(nb):
            g_jp = gate[jp:jp + 1, :]
            ahead = (g_jp > gate) | ((g_jp == gate) & (jp < blk_row))
            rank = rank + jnp.where(ahead & (jp < qi), 1.0, 0.0)
        sel_ref[s] = jnp.where((rank < float(MOBA_TOPK)) & (blk_row < qi), 1.0, 0.0)

        own = pl.ds(pl.multiple_of(qi * MOBA_BLOCK, MOBA_BLOCK), MOBA_BLOCK)
        st = lax.dot_general(kb_ref[own, :], qms[s], NT_DIMS, preferred_element_type=jnp.float32) * scale
        st = jnp.where(key_row <= qry_col, st, NEG_INF)
        m0 = jnp.max(st, axis=0, keepdims=True)
        p = jnp.exp(st - m0)
        m_ref[s] = m0
        l_ref[s] = jnp.sum(p, axis=0, keepdims=True)
        acc_ref[s] = jnp.dot(vt_ref[:, own], p.astype(jnp.bfloat16), preferred_element_type=jnp.float32)

    def past_block(j, carry):
        blk = pl.ds(pl.multiple_of(j * MOBA_BLOCK, MOBA_BLOCK), MOBA_BLOCK)
        k_j = kb_ref[blk, :]
        vt_j = vt_ref[:, blk]
        for s in range(2):
            st = lax.dot_general(k_j, qms[s], NT_DIMS, preferred_element_type=jnp.float32) * scale
            st = jnp.where(sel_ref[s, pl.ds(j, 1), :] > 0.0, st, NEG_INF)
            m_old = m_ref[s]
            m_new = jnp.maximum(m_old, jnp.max(st, axis=0, keepdims=True))
            alpha = jnp.exp(m_old - m_new)
            p = jnp.exp(st - m_new)
            m_ref[s] = m_new
            l_ref[s] = alpha * l_ref[s] + jnp.sum(p, axis=0, keepdims=True)
            acc_ref[s] = alpha * acc_ref[s] + jnp.dot(vt_j, p.astype(jnp.bfloat16), preferred_element_type=jnp.float32)
        return carry

    lax.fori_loop(0, qi, past_block, 0)

    row = lax.broadcasted_iota(jnp.int32, (LANES, MOBA_BLOCK), 0)
    out_t = jnp.where((row // D_HEAD) == 0, acc_ref[0] / l_ref[0], acc_ref[1] / l_ref[1])
    o_ref[0] = out_t.T


def moba_prompt(q, k, v):
    b, t, w = q.shape
    nb = t // MOBA_BLOCK
    return pl.pallas_call(
        _moba_prompt_kernel,
        grid=(b, w // LANES, nb),
        in_specs=[
            pl.BlockSpec((1, MOBA_BLOCK, LANES), lambda bi, hp, qi: (bi, qi, hp)),
            pl.BlockSpec((1, t, LANES), lambda bi, hp, qi: (bi, 0, hp)),
            pl.BlockSpec((1, t, LANES), lambda bi, hp, qi: (bi, 0, hp)),
        ],
        out_specs=pl.BlockSpec((1, MOBA_BLOCK, LANES), lambda bi, hp, qi: (bi, qi, hp)),
        out_shape=jax.ShapeDtypeStruct((b, t, w), jnp.float32),
        scratch_shapes=[
            pltpu.VMEM((LANES, t), jnp.bfloat16),
            pltpu.VMEM((t, LANES), jnp.bfloat16),
            pltpu.VMEM((2, nb, MOBA_BLOCK), jnp.float32),
            pltpu.VMEM((2, 1, MOBA_BLOCK), jnp.float32),
            pltpu.VMEM((2, 1, MOBA_BLOCK), jnp.float32),
            pltpu.VMEM((2, LANES, MOBA_BLOCK), jnp.float32),
        ],
        compiler_params=pltpu.CompilerParams(dimension_semantics=("parallel", "parallel", "arbitrary"),
                                             vmem_limit_bytes=VMEM_LIMIT),
        name="moba_prompt",
    )(q, k, v)


PAGES_PER_BLOCK = MOBA_BLOCK // PAGE_SIZE
KMEAN_BLOCKS = 8


def _block_kmean_kernel(pt_ref, *refs):
    pages, o_ref = refs[:-1], refs[-1]
    rows = []
    for n in range(KMEAN_BLOCKS):
        tot = jnp.sum(pages[n * PAGES_PER_BLOCK][0], axis=0, keepdims=True)
        for i in range(1, PAGES_PER_BLOCK):
            tot = tot + jnp.sum(pages[n * PAGES_PER_BLOCK + i][0], axis=0, keepdims=True)
        rows.append(tot * (1.0 / MOBA_BLOCK))
    o_ref[0] = jnp.concatenate(rows, axis=0)


def block_kmean(cache_k, page_table):
    bd, n_pages = page_table.shape
    w = cache_k.shape[-1]
    per_step = KMEAN_BLOCKS * PAGES_PER_BLOCK
    page_spec = lambda i: pl.BlockSpec((1, PAGE_SIZE, w), lambda b, g, pt: (pt[b, g * per_step + i], 0, 0))
    return pl.pallas_call(
        _block_kmean_kernel,
        grid_spec=pltpu.PrefetchScalarGridSpec(
            num_scalar_prefetch=1,
            grid=(bd, n_pages // per_step),
            in_specs=[page_spec(i) for i in range(per_step)],
            out_specs=pl.BlockSpec((1, KMEAN_BLOCKS, w), lambda b, g, pt: (b, g, 0)),
        ),
        out_shape=jax.ShapeDtypeStruct((bd, n_pages // PAGES_PER_BLOCK, w), jnp.float32),
        compiler_params=pltpu.CompilerParams(dimension_semantics=("parallel", "arbitrary"),
                                             vmem_limit_bytes=VMEM_LIMIT),
        name="block_kmean",
    )(page_table, *([cache_k] * per_step))


def _sample_select_kernel(q_ref, km_ref, seg_ref, idx_ref):
    gate = jnp.dot(km_ref[0] * q_ref[0], seg_ref[...], precision=HI, preferred_element_type=jnp.float32)
    nb = gate.shape[0]
    rows = lax.broadcasted_iota(jnp.int32, gate.shape, 0).astype(jnp.float32)
    out = []
    for _ in range(MOBA_TOPK):
        m = jnp.max(gate, axis=0, keepdims=True)
        first = jnp.min(jnp.where(gate == m, rows, float(nb)), axis=0, keepdims=True)
        gate = jnp.where(rows == first, NEG_INF, gate)
        out.append(first)
    out += [jnp.zeros_like(out[0])] * (SUBLANES - MOBA_TOPK)
    idx_ref[0] = jnp.concatenate(out, axis=0)


def sample_select(q, kmean):
    bd, nb, w = kmean.shape
    head = jnp.arange(w) // D_HEAD
    seg = (head[:, None] == head[None, :]).astype(jnp.float32)
    idx = pl.pallas_call(
        _sample_select_kernel,
        grid=(bd,),
        in_specs=[pl.BlockSpec((1, 1, w), lambda b: (b, 0, 0)), pl.BlockSpec((1, nb, w), lambda b: (b, 0, 0)),
                  pl.BlockSpec((w, w), lambda b: (0, 0))],
        out_specs=pl.BlockSpec((1, SUBLANES, w), lambda b: (b, 0, 0)),
        out_shape=jax.ShapeDtypeStruct((bd, SUBLANES, w), jnp.float32),
        compiler_params=pltpu.CompilerParams(dimension_semantics=("parallel",), vmem_limit_bytes=VMEM_LIMIT),
        name="sample_select",
    )(q, kmean, seg)
    return idx[:, :MOBA_TOPK, ::D_HEAD].astype(jnp.int32).transpose(0, 2, 1)


def _sample_attend_kernel(q_ref, ko_ref, vo_ref, ks_ref, vs_ref, o_ref):
    scale = D_HEAD ** -0.5
    for h in range(H_ATT):
        qh = q_ref[0, h:h + 1, :]
        s_sel = jnp.sum(ks_ref[0, h] * qh, axis=1, keepdims=True) * scale
        s_own = jnp.sum(ko_ref[0, h:h + 1, :] * qh, axis=1, keepdims=True) * scale
        m = jnp.maximum(jnp.max(s_sel, axis=0, keepdims=True), s_own)
        p_sel = jnp.exp(s_sel - m)
        p_own = jnp.exp(s_own - m)
        denom = jnp.sum(p_sel, axis=0, keepdims=True) + p_own
        acc = jnp.sum(p_sel * vs_ref[0, h], axis=0, keepdims=True) + p_own * vo_ref[0, h:h + 1, :]
        o_ref[0, h:h + 1, :] = acc / denom


def sample_attend(q, k_own, v_own, k_sel, v_sel):
    bd, h, n_keys, dh = k_sel.shape
    tok = pl.BlockSpec((1, h, dh), lambda b: (b, 0, 0))
    sel = pl.BlockSpec((1, h, n_keys, dh), lambda b: (b, 0, 0, 0))
    return pl.pallas_call(
        _sample_attend_kernel,
        grid=(bd,),
        in_specs=[tok, tok, tok, sel, sel],
        out_specs=tok,
        out_shape=jax.ShapeDtypeStruct((bd, h, dh), jnp.float32),
        compiler_params=pltpu.CompilerParams(dimension_semantics=("parallel",), vmem_limit_bytes=VMEM_LIMIT),
        name="sample_attend",
    )(q, k_own, v_own, k_sel, v_sel)


def moba_sample(q, k, v, cache_k, cache_v, page_table):
    bd, s, w = q.shape
    assert s == 1 and PAST_LEN % MOBA_BLOCK == 0 and PAST_LEN // MOBA_BLOCK >= MOBA_TOPK
    n_pool = cache_k.shape[0]
    kmean = block_kmean(cache_k.reshape(n_pool, PAGE_SIZE, w), page_table)
    sel = sample_select(q, kmean)
    logical = sel[..., None] * PAGES_PER_BLOCK + jnp.arange(PAGES_PER_BLOCK)
    phys = page_table[jnp.arange(bd)[:, None, None, None], logical]
    hi = jnp.arange(H_ATT)[None, :, None, None]
    gather = lambda c: c[phys, :, hi, :].reshape(bd, H_ATT, MOBA_TOPK * MOBA_BLOCK, D_HEAD)
    heads = lambda z: z.reshape(bd, H_ATT, D_HEAD)
    out = sample_attend(heads(q), heads(k), heads(v), gather(cache_k), gather(cache_v))
    return out.reshape(bd, 1, w)


PEER_CANDS = [(a, b) for a in range(PEER_TOPK) for b in range(PEER_TOPK) if (a + 1) * (b + 1) <= PEER_TOPK]
PEER_NCAND = -(-len(PEER_CANDS) // 8) * 8


def _extract_topk(x, n_rows, k):
    rows = lax.broadcasted_iota(jnp.int32, x.shape, 0).astype(jnp.float32)
    vals = []
    for _ in range(k):
        m = jnp.max(x, axis=0, keepdims=True)
        first = jnp.min(jnp.where(x == m, rows, float(n_rows)), axis=0, keepdims=True)
        x = jnp.where(rows == first, NEG_INF, x)
        vals.append(m)
    return x, vals


def _peer_route_kernel(xn_ref, wpq_ref, keys_ref, t1_ref, s2_ref, e1_ref, e2_ref, sv_ref, comb_ref):
    q = jnp.dot(xn_ref[...], wpq_ref[...], preferred_element_type=jnp.float32)
    half = PEER_DQ // 2
    for h in range(PEER_HEADS):
        masked = []
        for p in range(2):
            qs = q[:, (2 * h + p) * half:(2 * h + p + 1) * half].astype(jnp.bfloat16)
            st = lax.dot_general(keys_ref[p], qs, (((1,), (1,)), ((), ())),
                                 preferred_element_type=jnp.float32)
            rem, vals = _extract_topk(st, PEER_NKEYS, PEER_TOPK)
            for i, v in enumerate(vals):
                sv_ref[p, i:i + 1, :] = v
            masked.append(jnp.where(rem == NEG_INF, st, NEG_INF))
        comb_ref[...] = jnp.full(comb_ref.shape, NEG_INF, jnp.float32)
        for c, (a, b) in enumerate(PEER_CANDS):
            comb_ref[c:c + 1, :] = sv_ref[0, a:a + 1, :] + sv_ref[1, b:b + 1, :]
        _, cvals = _extract_topk(comb_ref[...], PEER_NCAND, PEER_TOPK + 1)
        cmax = cvals[0]
        z = jnp.zeros_like(cmax)
        for v in cvals[:PEER_TOPK]:
            z = z + jnp.exp(v - cmax)
        cut = 0.5 * (cvals[PEER_TOPK - 1] + cvals[PEER_TOPK])
        t1_ref[h] = cut - masked[0]
        s2_ref[h] = masked[1]
        e1_ref[h] = jnp.exp(masked[0] - sv_ref[0, 0:1, :])
        e2_ref[h] = jnp.exp(masked[1] - sv_ref[1, 0:1, :]) / z


def peer_route(xn_bf16, wpq_bf16, keys_bf16, tm):
    n = xn_bf16.shape[0]
    tab = jax.ShapeDtypeStruct((PEER_HEADS, PEER_NKEYS, n), jnp.float32)
    tab_spec = pl.BlockSpec((PEER_HEADS, PEER_NKEYS, tm), lambda i: (0, 0, i))
    return pl.pallas_call(
        _peer_route_kernel,
        grid=(n // tm,),
        in_specs=[
            pl.BlockSpec((tm, D_MODEL), lambda i: (i, 0)),
            pl.BlockSpec((D_MODEL, PEER_HEADS * PEER_DQ), lambda i: (0, 0)),
            pl.BlockSpec((2, PEER_NKEYS, PEER_DQ // 2), lambda i: (0, 0, 0)),
        ],
        out_specs=[tab_spec] * 4,
        out_shape=[tab] * 4,
        scratch_shapes=[pltpu.VMEM((2, PEER_TOPK, tm), jnp.float32),
                        pltpu.VMEM((PEER_NCAND, tm), jnp.float32)],
        compiler_params=pltpu.CompilerParams(dimension_semantics=("parallel",),
                                             vmem_limit_bytes=VMEM_LIMIT),
        name="peer_route",
    )(xn_bf16, wpq_bf16, keys_bf16)


def _gelu_exact(x):
    return 0.5 * x * (1.0 + lax.erf(x * (1.0 / math.sqrt(2.0))))


def _peer_dense_kernel(xn_ref, u_ref, vt_ref, t1_ref, s2_ref, e1_ref, e2_ref, h_ref, nfin_ref, o_ref, acc_ref, p_ref):
    j = pl.program_id(1)
    n_e1 = u_ref.shape[0] // PEER_NKEYS
    tm = xn_ref.shape[0]

    @pl.when(j == 0)
    def _():
        acc_ref[...] = jnp.zeros_like(acc_ref)

    for eb in range(n_e1 // PEER_ELB):
        erows = slice(eb * PEER_ELB * PEER_NKEYS, (eb + 1) * PEER_ELB * PEER_NKEYS)
        at = lax.dot_general(u_ref[erows, :], xn_ref[...], NT_DIMS,
                             preferred_element_type=jnp.float32)
        for c in range(tm // LANES):
            cols = slice(c * LANES, (c + 1) * LANES)
            for r0 in range(0, PEER_NKEYS, PEER_SGR):
                accs = [jnp.zeros((PEER_SGR, LANES), jnp.float32) for _ in range(PEER_ELB)]
                for h in range(PEER_HEADS):
                    s2t = s2_ref[h, r0:r0 + PEER_SGR, cols]
                    e2t = e2_ref[h, r0:r0 + PEER_SGR, cols]
                    for i in range(PEER_ELB):
                        el = eb * PEER_ELB + i
                        picked = s2t >= t1_ref[h, el:el + 1, cols]
                        accs[i] = accs[i] + jnp.where(picked, e2t, 0.0) * e1_ref[h, el:el + 1, cols]
                for i in range(PEER_ELB):
                    lo = i * PEER_NKEYS + r0
                    act = _gelu_exact(at[lo:lo + PEER_SGR, cols])
                    p_ref[eb, lo:lo + PEER_SGR, cols] = (accs[i] * act).astype(jnp.bfloat16)
        acc_ref[...] += jnp.dot(vt_ref[:, erows], p_ref[eb], preferred_element_type=jnp.float32)

    @pl.when(j == pl.num_programs(1) - 1)
    def _():
        o_ref[...] = _rms_norm(h_ref[...] + acc_ref[...].T, nfin_ref[...])


def peer_dense(xn_bf16, u_bf16, vt_bf16, t1, s2, e1, e2, h, norm_final, tm, te):
    n = xn_bf16.shape[0]
    n_e1 = te // PEER_NKEYS
    return pl.pallas_call(
        _peer_dense_kernel,
        grid=(n // tm, PEER_EXPERTS // te),
        in_specs=[
            pl.BlockSpec((tm, D_MODEL), lambda i, j: (i, 0)),
            pl.BlockSpec((te, D_MODEL), lambda i, j: (j, 0)),
            pl.BlockSpec((D_MODEL, te), lambda i, j: (0, j)),
            pl.BlockSpec((PEER_HEADS, n_e1, tm), lambda i, j: (0, j, i)),
            pl.BlockSpec((PEER_HEADS, PEER_NKEYS, tm), lambda i, j: (0, 0, i)),
            pl.BlockSpec((PEER_HEADS, n_e1, tm), lambda i, j: (0, j, i)),
            pl.BlockSpec((PEER_HEADS, PEER_NKEYS, tm), lambda i, j: (0, 0, i)),
            pl.BlockSpec((tm, D_MODEL), lambda i, j: (i, 0)),
            pl.BlockSpec((1, D_MODEL), lambda i, j: (0, 0)),
        ],
        out_specs=pl.BlockSpec((tm, D_MODEL), lambda i, j: (i, 0)),
        out_shape=jax.ShapeDtypeStruct((n, D_MODEL), jnp.float32),
        scratch_shapes=[pltpu.VMEM((D_MODEL, tm), jnp.float32),
                        pltpu.VMEM((n_e1 // PEER_ELB, PEER_ELB * PEER_NKEYS, tm), jnp.bfloat16)],
        compiler_params=pltpu.CompilerParams(dimension_semantics=("parallel", "arbitrary"),
                                             vmem_limit_bytes=VMEM_LIMIT),
        name="peer_dense",
    )(xn_bf16, u_bf16, vt_bf16, t1, s2, e1, e2, h, norm_final.reshape(1, D_MODEL))


def peer_block(h, xn_bf16, wpq_bf16, keys_bf16, u_bf16, vt_bf16, norm_final):
    n = h.shape[0]
    tm = PEER_TM if n >= PEER_TM else PEER_TM_SMALL
    pad = -n % tm
    hp, xb = jnp.pad(h, ((0, pad), (0, 0))), jnp.pad(xn_bf16, ((0, pad), (0, 0)))
    t1, s2, e1, e2 = peer_route(xb, wpq_bf16, keys_bf16, PEER_TM_SMALL)
    return peer_dense(xb, u_bf16, vt_bf16, t1, s2, e1, e2, hp, norm_final, tm, PEER_TE)[:n]


def layer(x, pos, prev0, s0, attend, norm_mix, w_in_bf16, rw, w_out_bf16, norm_ffn, peer, norm_final):
    mu, w0, w_dec, a0, w_aaa, w_gate, k_k, k_a, r_k, gn_g, gn_b = rw
    b, t, d = x.shape
    n = b * t
    if t == 1:
        xr, posr = x.reshape(1, b, d), jnp.broadcast_to(pos, (b,))
    else:
        xr, posr = x, pos
    seq = lambda z: z.reshape(b, t, z.shape[-1])
    pr, q, k, v = map(seq, project(xr, norm_mix, w_in_bf16, posr, min(xr.shape[1], PROJ_TT)))
    r, w, k2, kk, bb, vv, g, bonus = rwkv_prep(pr, prev0, mu, w0, w_dec, a0, w_aaa, w_gate, k_k, k_a, r_k, min(t, 256))
    y, s_new = rwkv_scan(r, w, k2, kk, bb, vv, s0, RWKV_NB, min(t, 128))
    a_out = attend(q, k, v)
    flat = lambda z: z.reshape(n, z.shape[-1])
    h, xn = merge(flat(y), flat(bonus), flat(g), gn_g, gn_b, flat(a_out), flat(x), w_out_bf16, norm_ffn, min(n, 512))
    out = peer_block(h, xn, *peer, norm_final)
    heads = lambda z: z.reshape(b, t, H_ATT, D_HEAD)
    return out.reshape(b, t, d), heads(k), heads(v), s_new, pr[:, -1]


def kernel(x_prompt, x_sample, cache_k, cache_v, page_table, state_wkv, state_shift, norm_mix, w_in, mu_shift, w0, w_decay_up, a0, w_aaa_up, w_gate_up, k_k, k_a, r_k, gn_gain, gn_bias, w_out, norm_ffn, w_pq, peer_sub_keys, expert_u, expert_v, norm_final):
    l = 0
    bf16 = jnp.bfloat16
    rw = (mu_shift[l], w0[l], w_decay_up[l], a0[l], w_aaa_up[l], w_gate_up[l], k_k[l], k_a[l], r_k[l], gn_gain[l], gn_bias[l])
    peer = (w_pq[l].astype(bf16), peer_sub_keys[l].astype(bf16), expert_u[l].astype(bf16), expert_v[l].T.astype(bf16))
    shared = (norm_mix[l], w_in[l].astype(bf16), rw, w_out[l].astype(bf16), norm_ffn[l], peer, norm_final)
    bp, tp, _ = x_prompt.shape
    bs, ts, _ = x_sample.shape
    y_p, k_p, v_p, s_p, sh_p = layer(
        x_prompt, jnp.arange(tp, dtype=jnp.int32), jnp.zeros((bp, W_RWKV_IN), jnp.float32),
        jnp.zeros((bp, H_RWKV, D_HEAD, D_HEAD), jnp.float32), moba_prompt, *shared)
    attend_cache = lambda q, k, v: moba_sample(q, k, v, cache_k[l], cache_v[l], page_table)
    y_s, k_s, v_s, s_s, sh_s = layer(
        x_sample, PAST_LEN + jnp.arange(ts, dtype=jnp.int32), state_shift[l], state_wkv[l], attend_cache, *shared)
    return (y_p, y_s, k_p[None], v_p[None], s_p[None], sh_p[None], k_s[None], v_s[None], s_s[None], sh_s[None])
```

```python
import functools
import math

import jax
import jax.numpy as jnp
from jax import lax
from jax.experimental import pallas as pl
from jax.experimental.pallas import tpu as pltpu

D_MODEL = 1024
DEPTH = 1
PAST_LEN = 16384
PAGE_SIZE = 128
D_HEAD = 64
H_RWKV = 8
H_ATT = 8
W_RWKV = H_RWKV * D_HEAD
W_ATT = H_ATT * D_HEAD
D_DECAY_LORA = 64
D_AAA_LORA = 64
D_GATE_LORA = 128
W_RWKV_IN = 3 * W_RWKV + D_DECAY_LORA + D_AAA_LORA + D_GATE_LORA
W_IN = W_RWKV_IN + 3 * W_ATT
GN_EPS = 64e-5
NORM_EPS = 1e-6
MOBA_BLOCK = 256
MOBA_TOPK = 3
Q_CHUNK = 64
ROT_DIM = D_HEAD // 4
ROPE_THETA = 500000.0
PEER_HEADS = 8
PEER_NKEYS = 128
PEER_TOPK = 16
PEER_DQ = 256
PEER_EXPERTS = PEER_NKEYS * PEER_NKEYS
PEER_TM = 512
PEER_TM_SMALL = 256
PEER_TE = 1024
PEER_ELB = 4
PEER_SGR = 32
NEG_INF = float("-inf")
VMEM_LIMIT = 56 * 1024 * 1024
LANES = 128
SUBLANES = 8
PROJ_TT = 512
RWKV_NB = 4
NT_DIMS = (((1,), (1,)), ((), ()))


def _rms_norm(x, g):
    return x * lax.rsqrt(jnp.mean(x * x, axis=-1, keepdims=True) + NORM_EPS) * g


def rope_tables(pos):
    half = ROT_DIM // 2
    inv = 1.0 / (ROPE_THETA ** (jnp.arange(half, dtype=jnp.float32) * 2.0 / ROT_DIM))
    ang = pos.astype(jnp.float32)[:, None] * inv[None, :]
    lane = jnp.arange(LANES) % D_HEAD
    idx = lane % half
    cos = jnp.where(lane[None, :] < ROT_DIM, jnp.cos(ang)[:, idx], 1.0)
    sin = jnp.sin(ang)[:, idx]
    sin_lo = jnp.where(lane[None, :] < half, -sin, 0.0)
    sin_hi = jnp.where((lane[None, :] >= half) & (lane[None, :] < ROT_DIM), sin, 0.0)
    return cos, sin_lo, sin_hi


def _project_kernel(x_ref, g_ref, w_ref, cos_ref, slo_ref, shi_ref, pr_o, q_o, k_o, v_o):
    xn = _rms_norm(x_ref[0], g_ref[...])
    p = jnp.dot(xn.astype(jnp.bfloat16), w_ref[...], preferred_element_type=jnp.float32)
    pr_o[0] = p[:, :W_RWKV_IN]
    reps = W_ATT // LANES
    cos = jnp.concatenate([cos_ref[...]] * reps, axis=1)
    slo = jnp.concatenate([slo_ref[...]] * reps, axis=1)
    shi = jnp.concatenate([shi_ref[...]] * reps, axis=1)
    half = ROT_DIM // 2

    def rope(z):
        ahead = pltpu.roll(z, W_ATT - half, 1)
        behind = pltpu.roll(z, half, 1)
        return z * cos + ahead * slo + behind * shi

    q_o[0] = rope(p[:, W_RWKV_IN:W_RWKV_IN + W_ATT])
    k_o[0] = rope(p[:, W_RWKV_IN + W_ATT:W_RWKV_IN + 2 * W_ATT])
    v_o[0] = p[:, W_RWKV_IN + 2 * W_ATT:]


def project(x, norm_g, w_in_bf16, pos, tt):
    b, t, d = x.shape
    cos, slo, shi = rope_tables(pos)
    f32 = jnp.float32
    tab_spec = pl.BlockSpec((tt, LANES), lambda bi, ti: (ti, 0))
    out_spec = lambda w: pl.BlockSpec((1, tt, w), lambda bi, ti: (bi, ti, 0))
    return pl.pallas_call(
        _project_kernel,
        grid=(b, t // tt),
        in_specs=[
            pl.BlockSpec((1, tt, d), lambda bi, ti: (bi, ti, 0)),
            pl.BlockSpec((1, d), lambda bi, ti: (0, 0)),
            pl.BlockSpec((d, W_IN), lambda bi, ti: (0, 0)),
            tab_spec, tab_spec, tab_spec,
        ],
        out_specs=[out_spec(W_RWKV_IN), out_spec(W_ATT), out_spec(W_ATT), out_spec(W_ATT)],
        out_shape=[jax.ShapeDtypeStruct((b, t, W_RWKV_IN), f32)] + [jax.ShapeDtypeStruct((b, t, W_ATT), f32)] * 3,
        compiler_params=pltpu.CompilerParams(dimension_semantics=("parallel", "parallel"),
                                             vmem_limit_bytes=VMEM_LIMIT),
        name="project",
    )(x, norm_g.reshape(1, d), w_in_bf16, cos, slo, shi)


N_PAIR = W_RWKV // LANES
HI = lax.Precision.HIGHEST


def _sigmoid(x):
    return 1.0 / (1.0 + jnp.exp(-x))


def _softplus(x):
    return jnp.maximum(x, 0.0) + jnp.log(1.0 + jnp.exp(-jnp.abs(x)))


def _rwkv_prep_kernel(p_ref, prev0_ref, mu_ref, w0_ref, wdec_ref, a0_ref, waaa_ref, wgate_ref, kk_w_ref, ka_ref,
                      rk_ref, seg_ref, r_o, w_o, k_o, kk_o, b_o, v_o, g_o, bonus_o, carry_ref):
    ti = pl.program_id(1)

    @pl.when(ti == 0)
    def _():
        carry_ref[...] = prev0_ref[0]

    p = p_ref[0]
    row = lax.broadcasted_iota(jnp.int32, p.shape, 0)
    if p.shape[0] == 1:
        prev = carry_ref[...]
    else:
        prev = jnp.where(row == 0, carry_ref[...], pltpu.roll(p, 1, 0))
    carry_ref[...] = p[p.shape[0] - 1:, :]
    ps = p + (prev - p) * mu_ref[...]
    r = ps[:, 0:W_RWKV]
    k = ps[:, W_RWKV:2 * W_RWKV]
    v = ps[:, 2 * W_RWKV:3 * W_RWKV]
    lora = ps[:, 3 * W_RWKV:3 * W_RWKV + LANES]
    gd = ps[:, 3 * W_RWKV + LANES:]
    dec = jnp.dot(jnp.tanh(lora).astype(jnp.bfloat16), wdec_ref[...], preferred_element_type=jnp.float32)
    logw = -jnp.exp(-_softplus(-(w0_ref[...] + dec)) - 0.5)
    a = _sigmoid(a0_ref[...] + jnp.dot(lora.astype(jnp.bfloat16), waaa_ref[...], preferred_element_type=jnp.float32))
    g = jnp.dot(_sigmoid(gd).astype(jnp.bfloat16), wgate_ref[...], preferred_element_type=jnp.float32)
    kk = k * kk_w_ref[...]
    sumsq = jnp.dot(kk * kk, seg_ref[...], precision=HI, preferred_element_type=jnp.float32)
    kk = kk / jnp.maximum(jnp.sqrt(sumsq), 1e-12)
    k2 = k * (1.0 + (a - 1.0) * ka_ref[...])
    rkk = jnp.dot(r * k2 * rk_ref[...], seg_ref[...], precision=HI, preferred_element_type=jnp.float32)
    r_o[0] = r
    w_o[0] = jnp.exp(logw)
    k_o[0] = k2
    kk_o[0] = kk
    b_o[0] = kk * a
    v_o[0] = v
    g_o[0] = g
    bonus_o[0] = rkk * v


def rwkv_prep(p, prev0, mu, w0, w_dec, a0, w_aaa, w_gate, k_k, k_a, r_k, tt):
    b, t, _ = p.shape
    f32 = jnp.float32
    wdec_pad = jnp.concatenate([w_dec, jnp.zeros_like(w_aaa)], axis=0).astype(jnp.bfloat16)
    waaa_pad = jnp.concatenate([jnp.zeros_like(w_dec), w_aaa], axis=0).astype(jnp.bfloat16)
    head = jnp.arange(W_RWKV) // D_HEAD
    seg = (head[:, None] == head[None, :]).astype(f32)
    row = lambda z: z.reshape(1, -1).astype(f32)
    vec_spec = lambda n: pl.BlockSpec((1, n), lambda bi, ti: (0, 0))
    mat_spec = lambda m, n: pl.BlockSpec((m, n), lambda bi, ti: (0, 0))
    out_spec = pl.BlockSpec((1, tt, W_RWKV), lambda bi, ti: (bi, ti, 0))
    out = jax.ShapeDtypeStruct((b, t, W_RWKV), f32)
    return pl.pallas_call(
        _rwkv_prep_kernel,
        grid=(b, t // tt),
        in_specs=[
            pl.BlockSpec((1, tt, W_RWKV_IN), lambda bi, ti: (bi, ti, 0)),
            pl.BlockSpec((1, 1, W_RWKV_IN), lambda bi, ti: (bi, 0, 0)),
            vec_spec(W_RWKV_IN), vec_spec(W_RWKV), mat_spec(LANES, W_RWKV), vec_spec(W_RWKV),
            mat_spec(LANES, W_RWKV), mat_spec(D_GATE_LORA, W_RWKV), vec_spec(W_RWKV), vec_spec(W_RWKV),
            vec_spec(W_RWKV), mat_spec(W_RWKV, W_RWKV),
        ],
        out_specs=[out_spec] * 8,
        out_shape=[out] * 8,
        scratch_shapes=[pltpu.VMEM((1, W_RWKV_IN), f32)],
        compiler_params=pltpu.CompilerParams(dimension_semantics=("parallel", "arbitrary"),
                                             vmem_limit_bytes=VMEM_LIMIT),
        name="rwkv_prep",
    )(p, prev0.reshape(b, 1, W_RWKV_IN), row(mu), row(w0), wdec_pad, row(a0), waaa_pad, w_gate.astype(jnp.bfloat16),
      row(k_k), row(k_a), row(r_k), seg)


def _rwkv_scan_kernel(r_ref, w_ref, k_ref, kk_ref, b_ref, v_ref, s0_ref, y_ref, sT_ref, s_ref):
    ci = pl.program_id(1)
    nb, tc = r_ref.shape[0], r_ref.shape[1]
    tiles = [(bi, p) for bi in range(nb) for p in range(N_PAIR)]

    @pl.when(ci == 0)
    def _():
        for i, (bi, p) in enumerate(tiles):
            s_ref[i] = s0_ref[bi, p]

    lane = lax.broadcasted_iota(jnp.int32, (D_HEAD, LANES), 1)
    sub = lax.broadcasted_iota(jnp.int32, (D_HEAD, LANES), 0)
    first = lane < D_HEAD
    eye2 = (sub == (lane % D_HEAD))
    sub128 = lax.broadcasted_iota(jnp.int32, (LANES, LANES), 0)
    lane128 = lax.broadcasted_iota(jnp.int32, (LANES, LANES), 1)
    eye128 = sub128 == lane128
    seg = jnp.where((sub128 // D_HEAD) == (lane128 // D_HEAD), 1.0, 0.0).astype(jnp.bfloat16)

    def head_sums(xs):
        parts = []
        for x in xs:
            hi = x.astype(jnp.bfloat16)
            parts += [hi, (x - hi.astype(jnp.float32)).astype(jnp.bfloat16)]
        both = jnp.dot(jnp.concatenate(parts, axis=0), seg, preferred_element_type=jnp.float32)
        return [both[i * LANES:i * LANES + D_HEAD] + both[i * LANES + D_HEAD:(i + 1) * LANES] for i in range(len(xs))]

    grp = min(tc, SUBLANES)

    def steps(gi, carry):
        rows = pl.ds(pl.multiple_of(gi * grp, grp), grp)
        lanes = lambda p: slice(p * LANES, (p + 1) * LANES)
        load = lambda ref: [ref[bi, rows, lanes(p)] for bi, p in tiles]
        r_g, w_g, k_g, kk_g, b_g, v_g = load(r_ref), load(w_ref), load(k_ref), load(kk_ref), load(b_ref), load(v_ref)
        states = [s_ref[i] for i in range(len(tiles))]
        yrows = [[] for _ in tiles]
        for j in range(grp):
            row = slice(j, j + 1)
            sks = head_sums([s * kk[row] for s, kk in zip(states, kk_g)])
            for i in range(len(tiles)):
                vcol = jnp.sum(jnp.where(eye128, v_g[i][row], 0.0), axis=1, keepdims=True)
                vmat = jnp.where(first, vcol[0:D_HEAD], vcol[D_HEAD:LANES])
                states[i] = states[i] * w_g[i][row] - sks[i] * b_g[i][row] + vmat * k_g[i][row]
            ys = head_sums([s * r[row] for s, r in zip(states, r_g)])
            for i in range(len(tiles)):
                yrows[i].append(jnp.sum(jnp.where(eye2, ys[i], 0.0), axis=0, keepdims=True))
        for i, (bi, p) in enumerate(tiles):
            y_ref[bi, rows, lanes(p)] = yrows[i][0] if grp == 1 else jnp.concatenate(yrows[i], axis=0)
            s_ref[i] = states[i]
        return carry

    lax.fori_loop(0, tc // grp, steps, 0)

    @pl.when(ci == pl.num_programs(1) - 1)
    def _():
        for i, (bi, p) in enumerate(tiles):
            sT_ref[bi, p] = s_ref[i]


def rwkv_scan(r, w, k, kk, bb, v, s0, nb, tc):
    b, t, _ = r.shape
    f32 = jnp.float32
    pair = lambda s: s.reshape(b, N_PAIR, 2, D_HEAD, D_HEAD).transpose(0, 1, 3, 2, 4).reshape(b, N_PAIR, D_HEAD, LANES)
    unpair = lambda s: s.reshape(b, N_PAIR, D_HEAD, 2, D_HEAD).transpose(0, 1, 3, 2, 4).reshape(b, H_RWKV, D_HEAD, D_HEAD)
    seq_spec = pl.BlockSpec((nb, tc, W_RWKV), lambda bi, ci: (bi, ci, 0))
    st_spec = pl.BlockSpec((nb, N_PAIR, D_HEAD, LANES), lambda bi, ci: (bi, 0, 0, 0))
    y, s_fin = pl.pallas_call(
        _rwkv_scan_kernel,
        grid=(b // nb, t // tc),
        in_specs=[seq_spec] * 6 + [st_spec],
        out_specs=[seq_spec, st_spec],
        out_shape=[jax.ShapeDtypeStruct((b, t, W_RWKV), f32), jax.ShapeDtypeStruct((b, N_PAIR, D_HEAD, LANES), f32)],
        scratch_shapes=[pltpu.VMEM((nb * N_PAIR, D_HEAD, LANES), f32)],
        compiler_params=pltpu.CompilerParams(dimension_semantics=("parallel", "arbitrary"),
                                             vmem_limit_bytes=VMEM_LIMIT),
        name="rwkv_scan",
    )(r, w, k, kk, bb, v, pair(s0.astype(f32)))
    return y, unpair(s_fin)


def _merge_kernel(y_ref, bonus_ref, g_ref, gng_ref, gnb_ref, seg_ref, a_ref, x_ref, wr_ref, wa_ref, nf_ref, h_o, xn_o):
    y = y_ref[...]
    inv = 1.0 / D_HEAD
    mean = jnp.dot(y, seg_ref[...], precision=HI, preferred_element_type=jnp.float32) * inv
    dlt = y - mean
    var = jnp.dot(dlt * dlt, seg_ref[...], precision=HI, preferred_element_type=jnp.float32) * inv
    yn = dlt * lax.rsqrt(var + GN_EPS) * gng_ref[...] + gnb_ref[...]
    r_out = (yn + bonus_ref[...]) * g_ref[...]
    h = (x_ref[...] + jnp.dot(r_out.astype(jnp.bfloat16), wr_ref[...], preferred_element_type=jnp.float32)
         + jnp.dot(a_ref[...].astype(jnp.bfloat16), wa_ref[...], preferred_element_type=jnp.float32))
    h_o[...] = h
    xn_o[...] = _rms_norm(h, nf_ref[...]).astype(jnp.bfloat16)


def merge(y, bonus, g, gn_g, gn_b, a_out, x, w_out_bf16, norm_ffn, tt):
    n, d = x.shape
    head = jnp.arange(W_RWKV) // D_HEAD
    seg = (head[:, None] == head[None, :]).astype(jnp.float32)
    row = lambda w: pl.BlockSpec((tt, w), lambda i: (i, 0))
    vec = lambda w: pl.BlockSpec((1, w), lambda i: (0, 0))
    mat = lambda r, c: pl.BlockSpec((r, c), lambda i: (0, 0))
    return pl.pallas_call(
        _merge_kernel,
        grid=(n // tt,),
        in_specs=[row(W_RWKV), row(W_RWKV), row(W_RWKV), vec(W_RWKV), vec(W_RWKV), mat(W_RWKV, W_RWKV),
                  row(W_ATT), row(d), mat(W_RWKV, d), mat(W_ATT, d), vec(d)],
        out_specs=[row(d), row(d)],
        out_shape=[jax.ShapeDtypeStruct((n, d), jnp.float32), jax.ShapeDtypeStruct((n, d), jnp.bfloat16)],
        compiler_params=pltpu.CompilerParams(dimension_semantics=("parallel",), vmem_limit_bytes=VMEM_LIMIT),
        name="merge",
    )(y, bonus, g, gn_g.reshape(1, -1), gn_b.reshape(1, -1), seg, a_out, x, w_out_bf16[:W_RWKV], w_out_bf16[W_RWKV:],
      norm_ffn.reshape(1, d))


def _moba_prompt_kernel(q_ref, k_ref, v_ref, o_ref, vt_ref, kb_ref, sel_ref, m_ref, l_ref, acc_ref):
    qi = pl.program_id(2)
    t_len = k_ref.shape[1]
    nb = t_len // MOBA_BLOCK
    scale = D_HEAD ** -0.5

    @pl.when(qi == 0)
    def _():
        vt_ref[...] = v_ref[0].T.astype(jnp.bfloat16)
        kb_ref[...] = k_ref[0].astype(jnp.bfloat16)

    q2 = q_ref[0]
    lane = lax.broadcasted_iota(jnp.int32, q2.shape, 1)
    kmean = jnp.mean(k_ref[0].reshape(nb, MOBA_BLOCK, LANES), axis=1)
    blk_row = lax.broadcasted_iota(jnp.int32, (nb, MOBA_BLOCK), 0)
    key_row = lax.broadcasted_iota(jnp.int32, (MOBA_BLOCK, MOBA_BLOCK), 0)
    qry_col = lax.broadcasted_iota(jnp.int32, (MOBA_BLOCK, MOBA_BLOCK), 1)
    qms = []
    for s in range(2):
        qm = jnp.where((lane // D_HEAD) == s, q2, 0.0)
        qms.append(qm.astype(jnp.bfloat16))
        gate = lax.dot_general(kmean, qm, NT_DIMS, precision=lax.Precision.HIGHEST,
                               preferred_element_type=jnp.float32)
        rank = jnp.zeros(gate.shape, jnp.float32)
        for jp in range(nb):
            g_jp = gate[jp:jp + 1, :]
            ahead = (g_jp > gate) | ((g_jp == gate) & (jp < blk_row))
            rank = rank + jnp.where(ahead & (jp < qi), 1.0, 0.0)
        sel_ref[s] = jnp.where((rank < float(MOBA_TOPK)) & (blk_row < qi), 1.0, 0.0)

        own = pl.ds(pl.multiple_of(qi * MOBA_BLOCK, MOBA_BLOCK), MOBA_BLOCK)
        st = lax.dot_general(kb_ref[own, :], qms[s], NT_DIMS, preferred_element_type=jnp.float32) * scale
        st = jnp.where(key_row <= qry_col, st, NEG_INF)
        m0 = jnp.max(st, axis=0, keepdims=True)
        p = jnp.exp(st - m0)
        m_ref[s] = m0
        l_ref[s] = jnp.sum(p, axis=0, keepdims=True)
        acc_ref[s] = jnp.dot(vt_ref[:, own], p.astype(jnp.bfloat16), preferred_element_type=jnp.float32)

    def past_block(j, carry):
        blk = pl.ds(pl.multiple_of(j * MOBA_BLOCK, MOBA_BLOCK), MOBA_BLOCK)
        k_j = kb_ref[blk, :]
        vt_j = vt_ref[:, blk]
        for s in range(2):
            st = lax.dot_general(k_j, qms[s], NT_DIMS, preferred_element_type=jnp.float32) * scale
            st = jnp.where(sel_ref[s, pl.ds(j, 1), :] > 0.0, st, NEG_INF)
            m_old = m_ref[s]
            m_new = jnp.maximum(m_old, jnp.max(st, axis=0, keepdims=True))
            alpha = jnp.exp(m_old - m_new)
            p = jnp.exp(st - m_new)
            m_ref[s] = m_new
            l_ref[s] = alpha * l_ref[s] + jnp.sum(p, axis=0, keepdims=True)
            acc_ref[s] = alpha * acc_ref[s] + jnp.dot(vt_j, p.astype(jnp.bfloat16), preferred_element_type=jnp.float32)
        return carry

    lax.fori_loop(0, qi, past_block, 0)

    row = lax.broadcasted_iota(jnp.int32, (LANES, MOBA_BLOCK), 0)
    out_t = jnp.where((row // D_HEAD) == 0, acc_ref[0] / l_ref[0], acc_ref[1] / l_ref[1])
    o_ref[0] = out_t.T


def moba_prompt(q, k, v):
    b, t, w = q.shape
    nb = t // MOBA_BLOCK
    return pl.pallas_call(
        _moba_prompt_kernel,
        grid=(b, w // LANES, nb),
        in_specs=[
            pl.BlockSpec((1, MOBA_BLOCK, LANES), lambda bi, hp, qi: (bi, qi, hp)),
            pl.BlockSpec((1, t, LANES), lambda bi, hp, qi: (bi, 0, hp)),
            pl.BlockSpec((1, t, LANES), lambda bi, hp, qi: (bi, 0, hp)),
        ],
        out_specs=pl.BlockSpec((1, MOBA_BLOCK, LANES), lambda bi, hp, qi: (bi, qi, hp)),
        out_shape=jax.ShapeDtypeStruct((b, t, w), jnp.float32),
        scratch_shapes=[
            pltpu.VMEM((LANES, t), jnp.bfloat16),
            pltpu.VMEM((t, LANES), jnp.bfloat16),
            pltpu.VMEM((2, nb, MOBA_BLOCK), jnp.float32),
            pltpu.VMEM((2, 1, MOBA_BLOCK), jnp.float32),
            pltpu.VMEM((2, 1, MOBA_BLOCK), jnp.float32),
            pltpu.VMEM((2, LANES, MOBA_BLOCK), jnp.float32),
        ],
        compiler_params=pltpu.CompilerParams(dimension_semantics=("parallel", "parallel", "arbitrary"),
                                             vmem_limit_bytes=VMEM_LIMIT),
        name="moba_prompt",
    )(q, k, v)


PAGES_PER_BLOCK = MOBA_BLOCK // PAGE_SIZE
KMEAN_BLOCKS = 8


def _block_kmean_kernel(pt_ref, *refs):
    pages, o_ref = refs[:-1], refs[-1]
    for n in range(KMEAN_BLOCKS):
        tot = jnp.sum(pages[n * PAGES_PER_BLOCK][0], axis=0)
        for i in range(1, PAGES_PER_BLOCK):
            tot = tot + jnp.sum(pages[n * PAGES_PER_BLOCK + i][0], axis=0)
        o_ref[0, n] = tot * (1.0 / MOBA_BLOCK)


def block_kmean(cache_k, page_table):
    bd, n_pages = page_table.shape
    _, _, h, dh = cache_k.shape
    per_step = KMEAN_BLOCKS * PAGES_PER_BLOCK
    page_spec = lambda i: pl.BlockSpec((1, PAGE_SIZE, h, dh), lambda b, g, pt: (pt[b, g * per_step + i], 0, 0, 0))
    return pl.pallas_call(
        _block_kmean_kernel,
        grid_spec=pltpu.PrefetchScalarGridSpec(
            num_scalar_prefetch=1,
            grid=(bd, n_pages // per_step),
            in_specs=[page_spec(i) for i in range(per_step)],
            out_specs=pl.BlockSpec((1, KMEAN_BLOCKS, h, dh), lambda b, g, pt: (b, g, 0, 0)),
        ),
        out_shape=jax.ShapeDtypeStruct((bd, n_pages // PAGES_PER_BLOCK, h, dh), jnp.float32),
        compiler_params=pltpu.CompilerParams(dimension_semantics=("parallel", "arbitrary"),
                                             vmem_limit_bytes=VMEM_LIMIT),
        name="block_kmean",
    )(page_table, *([cache_k] * per_step))


def _sample_select_kernel(q_ref, km_ref, idx_ref):
    gate = jnp.sum(km_ref[0] * q_ref[...], axis=-1, keepdims=True)
    nb = gate.shape[0]
    blocks = lax.broadcasted_iota(jnp.int32, gate.shape, 0).astype(jnp.float32)
    for i in range(MOBA_TOPK):
        m = jnp.max(gate, axis=0, keepdims=True)
        first = jnp.min(jnp.where(gate == m, blocks, float(nb)), axis=0, keepdims=True)
        gate = jnp.where(blocks == first, NEG_INF, gate)
        idx_ref[0, i] = first[0]


def sample_select(q, kmean):
    bd, nb, h, dh = kmean.shape
    idx = pl.pallas_call(
        _sample_select_kernel,
        grid=(bd,),
        in_specs=[pl.BlockSpec((1, h, dh), lambda b: (b, 0, 0)), pl.BlockSpec((1, nb, h, dh), lambda b: (b, 0, 0, 0))],
        out_specs=pl.BlockSpec((1, MOBA_TOPK, h, 1), lambda b: (b, 0, 0, 0)),
        out_shape=jax.ShapeDtypeStruct((bd, MOBA_TOPK, h, 1), jnp.float32),
        compiler_params=pltpu.CompilerParams(dimension_semantics=("parallel",), vmem_limit_bytes=VMEM_LIMIT),
        name="sample_select",
    )(q, kmean)
    return idx[..., 0].astype(jnp.int32).transpose(0, 2, 1)


def _sample_attend_kernel(q_ref, ko_ref, vo_ref, ks_ref, vs_ref, o_ref):
    scale = D_HEAD ** -0.5
    for h in range(H_ATT):
        qh = q_ref[0, h:h + 1, :]
        s_sel = jnp.sum(ks_ref[0, h] * qh, axis=1, keepdims=True) * scale
        s_own = jnp.sum(ko_ref[0, h:h + 1, :] * qh, axis=1, keepdims=True) * scale
        m = jnp.maximum(jnp.max(s_sel, axis=0, keepdims=True), s_own)
        p_sel = jnp.exp(s_sel - m)
        p_own = jnp.exp(s_own - m)
        denom = jnp.sum(p_sel, axis=0, keepdims=True) + p_own
        acc = jnp.sum(p_sel * vs_ref[0, h], axis=0, keepdims=True) + p_own * vo_ref[0, h:h + 1, :]
        o_ref[0, h:h + 1, :] = acc / denom


def sample_attend(q, k_own, v_own, k_sel, v_sel):
    bd, h, n_keys, dh = k_sel.shape
    tok = pl.BlockSpec((1, h, dh), lambda b: (b, 0, 0))
    sel = pl.BlockSpec((1, h, n_keys, dh), lambda b: (b, 0, 0, 0))
    return pl.pallas_call(
        _sample_attend_kernel,
        grid=(bd,),
        in_specs=[tok, tok, tok, sel, sel],
        out_specs=tok,
        out_shape=jax.ShapeDtypeStruct((bd, h, dh), jnp.float32),
        compiler_params=pltpu.CompilerParams(dimension_semantics=("parallel",), vmem_limit_bytes=VMEM_LIMIT),
        name="sample_attend",
    )(q, k_own, v_own, k_sel, v_sel)


def moba_sample(q, k, v, cache_k, cache_v, page_table):
    bd, s, w = q.shape
    assert s == 1 and PAST_LEN % MOBA_BLOCK == 0 and PAST_LEN // MOBA_BLOCK >= MOBA_TOPK
    heads = lambda z: z.reshape(bd, H_ATT, D_HEAD)
    sel = sample_select(heads(q), block_kmean(cache_k, page_table))
    logical = sel[..., None] * PAGES_PER_BLOCK + jnp.arange(PAGES_PER_BLOCK)
    phys = page_table[jnp.arange(bd)[:, None, None, None], logical]
    hi = jnp.arange(H_ATT)[None, :, None, None]
    gather = lambda c: c[phys, :, hi, :].reshape(bd, H_ATT, MOBA_TOPK * MOBA_BLOCK, D_HEAD)
    out = sample_attend(heads(q), heads(k), heads(v), gather(cache_k), gather(cache_v))
    return out.reshape(bd, 1, w)


PEER_CANDS = [(a, b) for a in range(PEER_TOPK) for b in range(PEER_TOPK) if (a + 1) * (b + 1) <= PEER_TOPK]
PEER_NCAND = -(-len(PEER_CANDS) // 8) * 8


def _extract_topk(x, n_rows, k):
    rows = lax.broadcasted_iota(jnp.int32, x.shape, 0).astype(jnp.float32)
    vals = []
    for _ in range(k):
        m = jnp.max(x, axis=0, keepdims=True)
        first = jnp.min(jnp.where(x == m, rows, float(n_rows)), axis=0, keepdims=True)
        x = jnp.where(rows == first, NEG_INF, x)
        vals.append(m)
    return x, vals


def _peer_route_kernel(xn_ref, wpq_ref, keys_ref, t1_ref, s2_ref, e1_ref, e2_ref, sv_ref, comb_ref):
    q = jnp.dot(xn_ref[...], wpq_ref[...], preferred_element_type=jnp.float32)
    half = PEER_DQ // 2
    for h in range(PEER_HEADS):
        masked = []
        for p in range(2):
            qs = q[:, (2 * h + p) * half:(2 * h + p + 1) * half].astype(jnp.bfloat16)
            st = lax.dot_general(keys_ref[p], qs, (((1,), (1,)), ((), ())),
                                 preferred_element_type=jnp.float32)
            rem, vals = _extract_topk(st, PEER_NKEYS, PEER_TOPK)
            for i, v in enumerate(vals):
                sv_ref[p, i:i + 1, :] = v
            masked.append(jnp.where(rem == NEG_INF, st, NEG_INF))
        comb_ref[...] = jnp.full(comb_ref.shape, NEG_INF, jnp.float32)
        for c, (a, b) in enumerate(PEER_CANDS):
            comb_ref[c:c + 1, :] = sv_ref[0, a:a + 1, :] + sv_ref[1, b:b + 1, :]
        _, cvals = _extract_topk(comb_ref[...], PEER_NCAND, PEER_TOPK + 1)
        cmax = cvals[0]
        z = jnp.zeros_like(cmax)
        for v in cvals[:PEER_TOPK]:
            z = z + jnp.exp(v - cmax)
        cut = 0.5 * (cvals[PEER_TOPK - 1] + cvals[PEER_TOPK])
        t1_ref[h] = cut - masked[0]
        s2_ref[h] = masked[1]
        e1_ref[h] = jnp.exp(masked[0] - sv_ref[0, 0:1, :])
        e2_ref[h] = jnp.exp(masked[1] - sv_ref[1, 0:1, :]) / z


def peer_route(xn_bf16, wpq_bf16, keys_bf16, tm):
    n = xn_bf16.shape[0]
    tab = jax.ShapeDtypeStruct((PEER_HEADS, PEER_NKEYS, n), jnp.float32)
    tab_spec = pl.BlockSpec((PEER_HEADS, PEER_NKEYS, tm), lambda i: (0, 0, i))
    return pl.pallas_call(
        _peer_route_kernel,
        grid=(n // tm,),
        in_specs=[
            pl.BlockSpec((tm, D_MODEL), lambda i: (i, 0)),
            pl.BlockSpec((D_MODEL, PEER_HEADS * PEER_DQ), lambda i: (0, 0)),
            pl.BlockSpec((2, PEER_NKEYS, PEER_DQ // 2), lambda i: (0, 0, 0)),
        ],
        out_specs=[tab_spec] * 4,
        out_shape=[tab] * 4,
        scratch_shapes=[pltpu.VMEM((2, PEER_TOPK, tm), jnp.float32),
                        pltpu.VMEM((PEER_NCAND, tm), jnp.float32)],
        compiler_params=pltpu.CompilerParams(dimension_semantics=("parallel",),
                                             vmem_limit_bytes=VMEM_LIMIT),
        name="peer_route",
    )(xn_bf16, wpq_bf16, keys_bf16)


def _gelu_exact(x):
    return 0.5 * x * (1.0 + lax.erf(x * (1.0 / math.sqrt(2.0))))


def _peer_dense_kernel(xn_ref, u_ref, vt_ref, t1_ref, s2_ref, e1_ref, e2_ref, h_ref, nfin_ref, o_ref, acc_ref, p_ref):
    j = pl.program_id(1)
    n_e1 = u_ref.shape[0] // PEER_NKEYS
    tm = xn_ref.shape[0]

    @pl.when(j == 0)
    def _():
        acc_ref[...] = jnp.zeros_like(acc_ref)

    for eb in range(n_e1 // PEER_ELB):
        erows = slice(eb * PEER_ELB * PEER_NKEYS, (eb + 1) * PEER_ELB * PEER_NKEYS)
        at = lax.dot_general(u_ref[erows, :], xn_ref[...], NT_DIMS,
                             preferred_element_type=jnp.float32)
        for c in range(tm // LANES):
            cols = slice(c * LANES, (c + 1) * LANES)
            for r0 in range(0, PEER_NKEYS, PEER_SGR):
                accs = [jnp.zeros((PEER_SGR, LANES), jnp.float32) for _ in range(PEER_ELB)]
                for h in range(PEER_HEADS):
                    s2t = s2_ref[h, r0:r0 + PEER_SGR, cols]
                    e2t = e2_ref[h, r0:r0 + PEER_SGR, cols]
                    for i in range(PEER_ELB):
                        el = eb * PEER_ELB + i
                        picked = s2t >= t1_ref[h, el:el + 1, cols]
                        accs[i] = accs[i] + jnp.where(picked, e2t, 0.0) * e1_ref[h, el:el + 1, cols]
                for i in range(PEER_ELB):
                    lo = i * PEER_NKEYS + r0
                    act = _gelu_exact(at[lo:lo + PEER_SGR, cols])
                    p_ref[eb, lo:lo + PEER_SGR, cols] = (accs[i] * act).astype(jnp.bfloat16)
        acc_ref[...] += jnp.dot(vt_ref[:, erows], p_ref[eb], preferred_element_type=jnp.float32)

    @pl.when(j == pl.num_programs(1) - 1)
    def _():
        o_ref[...] = _rms_norm(h_ref[...] + acc_ref[...].T, nfin_ref[...])


def peer_dense(xn_bf16, u_bf16, vt_bf16, t1, s2, e1, e2, h, norm_final, tm, te):
    n = xn_bf16.shape[0]
    n_e1 = te // PEER_NKEYS
    return pl.pallas_call(
        _peer_dense_kernel,
        grid=(n // tm, PEER_EXPERTS // te),
        in_specs=[
            pl.BlockSpec((tm, D_MODEL), lambda i, j: (i, 0)),
            pl.BlockSpec((te, D_MODEL), lambda i, j: (j, 0)),
            pl.BlockSpec((D_MODEL, te), lambda i, j: (0, j)),
            pl.BlockSpec((PEER_HEADS, n_e1, tm), lambda i, j: (0, j, i)),
            pl.BlockSpec((PEER_HEADS, PEER_NKEYS, tm), lambda i, j: (0, 0, i)),
            pl.BlockSpec((PEER_HEADS, n_e1, tm), lambda i, j: (0, j, i)),
            pl.BlockSpec((PEER_HEADS, PEER_NKEYS, tm), lambda i, j: (0, 0, i)),
            pl.BlockSpec((tm, D_MODEL), lambda i, j: (i, 0)),
            pl.BlockSpec((1, D_MODEL), lambda i, j: (0, 0)),
        ],
        out_specs=pl.BlockSpec((tm, D_MODEL), lambda i, j: (i, 0)),
        out_shape=jax.ShapeDtypeStruct((n, D_MODEL), jnp.float32),
        scratch_shapes=[pltpu.VMEM((D_MODEL, tm), jnp.float32),
                        pltpu.VMEM((n_e1 // PEER_ELB, PEER_ELB * PEER_NKEYS, tm), jnp.bfloat16)],
        compiler_params=pltpu.CompilerParams(dimension_semantics=("parallel", "arbitrary"),
                                             vmem_limit_bytes=VMEM_LIMIT),
        name="peer_dense",
    )(xn_bf16, u_bf16, vt_bf16, t1, s2, e1, e2, h, norm_final.reshape(1, D_MODEL))


def peer_block(h, xn_bf16, wpq_bf16, keys_bf16, u_bf16, vt_bf16, norm_final):
    n = h.shape[0]
    tm = PEER_TM if n >= PEER_TM else PEER_TM_SMALL
    pad = -n % tm
    hp, xb = jnp.pad(h, ((0, pad), (0, 0))), jnp.pad(xn_bf16, ((0, pad), (0, 0)))
    t1, s2, e1, e2 = peer_route(xb, wpq_bf16, keys_bf16, PEER_TM_SMALL)
    return peer_dense(xb, u_bf16, vt_bf16, t1, s2, e1, e2, hp, norm_final, tm, PEER_TE)[:n]


def layer(x, pos, prev0, s0, attend, norm_mix, w_in_bf16, rw, w_out_bf16, norm_ffn, peer, norm_final):
    mu, w0, w_dec, a0, w_aaa, w_gate, k_k, k_a, r_k, gn_g, gn_b = rw
    b, t, d = x.shape
    n = b * t
    if t == 1:
        xr, posr = x.reshape(1, b, d), jnp.broadcast_to(pos, (b,))
    else:
        xr, posr = x, pos
    seq = lambda z: z.reshape(b, t, z.shape[-1])
    pr, q, k, v = map(seq, project(xr, norm_mix, w_in_bf16, posr, min(xr.shape[1], PROJ_TT)))
    r, w, k2, kk, bb, vv, g, bonus = rwkv_prep(pr, prev0, mu, w0, w_dec, a0, w_aaa, w_gate, k_k, k_a, r_k, min(t, 256))
    y, s_new = rwkv_scan(r, w, k2, kk, bb, vv, s0, RWKV_NB, min(t, 128))
    a_out = attend(q, k, v)
    flat = lambda z: z.reshape(n, z.shape[-1])
    h, xn = merge(flat(y), flat(bonus), flat(g), gn_g, gn_b, flat(a_out), flat(x), w_out_bf16, norm_ffn, min(n, 512))
    out = peer_block(h, xn, *peer, norm_final)
    heads = lambda z: z.reshape(b, t, H_ATT, D_HEAD)
    return out.reshape(b, t, d), heads(k), heads(v), s_new, pr[:, -1]


def kernel(x_prompt, x_sample, cache_k, cache_v, page_table, state_wkv, state_shift, norm_mix, w_in, mu_shift, w0, w_decay_up, a0, w_aaa_up, w_gate_up, k_k, k_a, r_k, gn_gain, gn_bias, w_out, norm_ffn, w_pq, peer_sub_keys, expert_u, expert_v, norm_final):
    l = 0
    bf16 = jnp.bfloat16
    rw = (mu_shift[l], w0[l], w_decay_up[l], a0[l], w_aaa_up[l], w_gate_up[l], k_k[l], k_a[l], r_k[l], gn_gain[l], gn_bias[l])
    peer = (w_pq[l].astype(bf16), peer_sub_keys[l].astype(bf16), expert_u[l].astype(bf16), expert_v[l].T.astype(bf16))
    shared = (norm_mix[l], w_in[l].astype(bf16), rw, w_out[l].astype(bf16), norm_ffn[l], peer, norm_final)
    bp, tp, _ = x_prompt.shape
    bs, ts, _ = x_sample.shape
    y_p, k_p, v_p, s_p, sh_p = layer(
        x_prompt, jnp.arange(tp, dtype=jnp.int32), jnp.zeros((bp, W_RWKV_IN), jnp.float32),
        jnp.zeros((bp, H_RWKV, D_HEAD, D_HEAD), jnp.float32), moba_prompt, *shared)
    attend_cache = lambda q, k, v: moba_sample(q, k, v, cache_k[l], cache_v[l], page_table)
    y_s, k_s, v_s, s_s, sh_s = layer(
        x_sample, PAST_LEN + jnp.arange(ts, dtype=jnp.int32), state_shift[l], state_wkv[l], attend_cache, *shared)
    return (y_p, y_s, k_p[None], v_p[None], s_p[None], sh_p[None], k_s[None], v_s[None], s_s[None], sh_s[None])
```

```python
import functools
import math

import jax
import jax.numpy as jnp
from jax import lax
from jax.experimental import pallas as pl
from jax.experimental.pallas import tpu as pltpu

D_MODEL = 1024
DEPTH = 1
PAST_LEN = 16384
PAGE_SIZE = 128
D_HEAD = 64
H_RWKV = 8
H_ATT = 8
W_RWKV = H_RWKV * D_HEAD
W_ATT = H_ATT * D_HEAD
D_DECAY_LORA = 64
D_AAA_LORA = 64
D_GATE_LORA = 128
W_RWKV_IN = 3 * W_RWKV + D_DECAY_LORA + D_AAA_LORA + D_GATE_LORA
W_IN = W_RWKV_IN + 3 * W_ATT
GN_EPS = 64e-5
NORM_EPS = 1e-6
MOBA_BLOCK = 256
MOBA_TOPK = 3
Q_CHUNK = 64
ROT_DIM = D_HEAD // 4
ROPE_THETA = 500000.0
PEER_HEADS = 8
PEER_NKEYS = 128
PEER_TOPK = 16
PEER_DQ = 256
PEER_EXPERTS = PEER_NKEYS * PEER_NKEYS
PEER_TM = 512
PEER_TM_SMALL = 256
PEER_TE = 1024
PEER_ELB = 4
PEER_SGR = 32
NEG_INF = float("-inf")
VMEM_LIMIT = 56 * 1024 * 1024
LANES = 128
SUBLANES = 8
PROJ_TT = 512
RWKV_NB = 4
NT_DIMS = (((1,), (1,)), ((), ()))


def _rms_norm(x, g):
    return x * lax.rsqrt(jnp.mean(x * x, axis=-1, keepdims=True) + NORM_EPS) * g


def rope_tables(pos):
    half = ROT_DIM // 2
    inv = 1.0 / (ROPE_THETA ** (jnp.arange(half, dtype=jnp.float32) * 2.0 / ROT_DIM))
    ang = pos.astype(jnp.float32)[:, None] * inv[None, :]
    lane = jnp.arange(LANES) % D_HEAD
    idx = lane % half
    cos = jnp.where(lane[None, :] < ROT_DIM, jnp.cos(ang)[:, idx], 1.0)
    sin = jnp.sin(ang)[:, idx]
    sin_lo = jnp.where(lane[None, :] < half, -sin, 0.0)
    sin_hi = jnp.where((lane[None, :] >= half) & (lane[None, :] < ROT_DIM), sin, 0.0)
    return cos, sin_lo, sin_hi


def _project_kernel(x_ref, g_ref, w_ref, cos_ref, slo_ref, shi_ref, pr_o, q_o, k_o, v_o):
    xn = _rms_norm(x_ref[0], g_ref[...])
    p = jnp.dot(xn.astype(jnp.bfloat16), w_ref[...], preferred_element_type=jnp.float32)
    pr_o[0] = p[:, :W_RWKV_IN]
    reps = W_ATT // LANES
    cos = jnp.concatenate([cos_ref[...]] * reps, axis=1)
    slo = jnp.concatenate([slo_ref[...]] * reps, axis=1)
    shi = jnp.concatenate([shi_ref[...]] * reps, axis=1)
    half = ROT_DIM // 2

    def rope(z):
        ahead = pltpu.roll(z, W_ATT - half, 1)
        behind = pltpu.roll(z, half, 1)
        return z * cos + ahead * slo + behind * shi

    q_o[0] = rope(p[:, W_RWKV_IN:W_RWKV_IN + W_ATT])
    k_o[0] = rope(p[:, W_RWKV_IN + W_ATT:W_RWKV_IN + 2 * W_ATT])
    v_o[0] = p[:, W_RWKV_IN + 2 * W_ATT:]


def project(x, norm_g, w_in_bf16, pos, tt):
    b, t, d = x.shape
    cos, slo, shi = rope_tables(pos)
    f32 = jnp.float32
    tab_spec = pl.BlockSpec((tt, LANES), lambda bi, ti: (ti, 0))
    out_spec = lambda w: pl.BlockSpec((1, tt, w), lambda bi, ti: (bi, ti, 0))
    return pl.pallas_call(
        _project_kernel,
        grid=(b, t // tt),
        in_specs=[
            pl.BlockSpec((1, tt, d), lambda bi, ti: (bi, ti, 0)),
            pl.BlockSpec((1, d), lambda bi, ti: (0, 0)),
            pl.BlockSpec((d, W_IN), lambda bi, ti: (0, 0)),
            tab_spec, tab_spec, tab_spec,
        ],
        out_specs=[out_spec(W_RWKV_IN), out_spec(W_ATT), out_spec(W_ATT), out_spec(W_ATT)],
        out_shape=[jax.ShapeDtypeStruct((b, t, W_RWKV_IN), f32)] + [jax.ShapeDtypeStruct((b, t, W_ATT), f32)] * 3,
        compiler_params=pltpu.CompilerParams(dimension_semantics=("parallel", "parallel"),
                                             vmem_limit_bytes=VMEM_LIMIT),
        name="project",
    )(x, norm_g.reshape(1, d), w_in_bf16, cos, slo, shi)


N_PAIR = W_RWKV // LANES
HI = lax.Precision.HIGHEST


def _sigmoid(x):
    return 1.0 / (1.0 + jnp.exp(-x))


def _softplus(x):
    return jnp.maximum(x, 0.0) + jnp.log(1.0 + jnp.exp(-jnp.abs(x)))


def _rwkv_prep_kernel(p_ref, prev0_ref, mu_ref, w0_ref, wdec_ref, a0_ref, waaa_ref, wgate_ref, kk_w_ref, ka_ref,
                      rk_ref, seg_ref, r_o, w_o, k_o, kk_o, b_o, v_o, g_o, bonus_o, carry_ref):
    ti = pl.program_id(1)

    @pl.when(ti == 0)
    def _():
        carry_ref[...] = prev0_ref[0]

    p = p_ref[0]
    row = lax.broadcasted_iota(jnp.int32, p.shape, 0)
    if p.shape[0] == 1:
        prev = carry_ref[...]
    else:
        prev = jnp.where(row == 0, carry_ref[...], pltpu.roll(p, 1, 0))
    carry_ref[...] = p[p.shape[0] - 1:, :]
    ps = p + (prev - p) * mu_ref[...]
    r = ps[:, 0:W_RWKV]
    k = ps[:, W_RWKV:2 * W_RWKV]
    v = ps[:, 2 * W_RWKV:3 * W_RWKV]
    lora = ps[:, 3 * W_RWKV:3 * W_RWKV + LANES]
    gd = ps[:, 3 * W_RWKV + LANES:]
    dec = jnp.dot(jnp.tanh(lora).astype(jnp.bfloat16), wdec_ref[...], preferred_element_type=jnp.float32)
    logw = -jnp.exp(-_softplus(-(w0_ref[...] + dec)) - 0.5)
    a = _sigmoid(a0_ref[...] + jnp.dot(lora.astype(jnp.bfloat16), waaa_ref[...], preferred_element_type=jnp.float32))
    g = jnp.dot(_sigmoid(gd).astype(jnp.bfloat16), wgate_ref[...], preferred_element_type=jnp.float32)
    kk = k * kk_w_ref[...]
    sumsq = jnp.dot(kk * kk, seg_ref[...], precision=HI, preferred_element_type=jnp.float32)
    kk = kk / jnp.maximum(jnp.sqrt(sumsq), 1e-12)
    k2 = k * (1.0 + (a - 1.0) * ka_ref[...])
    rkk = jnp.dot(r * k2 * rk_ref[...], seg_ref[...], precision=HI, preferred_element_type=jnp.float32)
    r_o[0] = r
    w_o[0] = jnp.exp(logw)
    k_o[0] = k2
    kk_o[0] = kk
    b_o[0] = kk * a
    v_o[0] = v
    g_o[0] = g
    bonus_o[0] = rkk * v


def rwkv_prep(p, prev0, mu, w0, w_dec, a0, w_aaa, w_gate, k_k, k_a, r_k, tt):
    b, t, _ = p.shape
    f32 = jnp.float32
    wdec_pad = jnp.concatenate([w_dec, jnp.zeros_like(w_aaa)], axis=0).astype(jnp.bfloat16)
    waaa_pad = jnp.concatenate([jnp.zeros_like(w_dec), w_aaa], axis=0).astype(jnp.bfloat16)
    head = jnp.arange(W_RWKV) // D_HEAD
    seg = (head[:, None] == head[None, :]).astype(f32)
    row = lambda z: z.reshape(1, -1).astype(f32)
    vec_spec = lambda n: pl.BlockSpec((1, n), lambda bi, ti: (0, 0))
    mat_spec = lambda m, n: pl.BlockSpec((m, n), lambda bi, ti: (0, 0))
    out_spec = pl.BlockSpec((1, tt, W_RWKV), lambda bi, ti: (bi, ti, 0))
    out = jax.ShapeDtypeStruct((b, t, W_RWKV), f32)
    return pl.pallas_call(
        _rwkv_prep_kernel,
        grid=(b, t // tt),
        in_specs=[
            pl.BlockSpec((1, tt, W_RWKV_IN), lambda bi, ti: (bi, ti, 0)),
            pl.BlockSpec((1, 1, W_RWKV_IN), lambda bi, ti: (bi, 0, 0)),
            vec_spec(W_RWKV_IN), vec_spec(W_RWKV), mat_spec(LANES, W_RWKV), vec_spec(W_RWKV),
            mat_spec(LANES, W_RWKV), mat_spec(D_GATE_LORA, W_RWKV), vec_spec(W_RWKV), vec_spec(W_RWKV),
            vec_spec(W_RWKV), mat_spec(W_RWKV, W_RWKV),
        ],
        out_specs=[out_spec] * 8,
        out_shape=[out] * 8,
        scratch_shapes=[pltpu.VMEM((1, W_RWKV_IN), f32)],
        compiler_params=pltpu.CompilerParams(dimension_semantics=("parallel", "arbitrary"),
                                             vmem_limit_bytes=VMEM_LIMIT),
        name="rwkv_prep",
    )(p, prev0.reshape(b, 1, W_RWKV_IN), row(mu), row(w0), wdec_pad, row(a0), waaa_pad, w_gate.astype(jnp.bfloat16),
      row(k_k), row(k_a), row(r_k), seg)


def _rwkv_scan_kernel(r_ref, w_ref, k_ref, kk_ref, b_ref, v_ref, s0_ref, y_ref, sT_ref, s_ref):
    ci = pl.program_id(1)
    nb, tc = r_ref.shape[0], r_ref.shape[1]
    tiles = [(bi, p) for bi in range(nb) for p in range(N_PAIR)]

    @pl.when(ci == 0)
    def _():
        for i, (bi, p) in enumerate(tiles):
            s_ref[i] = s0_ref[bi, p]

    lane = lax.broadcasted_iota(jnp.int32, (D_HEAD, LANES), 1)
    sub = lax.broadcasted_iota(jnp.int32, (D_HEAD, LANES), 0)
    first = lane < D_HEAD
    eye2 = (sub == (lane % D_HEAD))
    sub128 = lax.broadcasted_iota(jnp.int32, (LANES, LANES), 0)
    lane128 = lax.broadcasted_iota(jnp.int32, (LANES, LANES), 1)
    eye128 = sub128 == lane128
    seg = jnp.where((sub128 // D_HEAD) == (lane128 // D_HEAD), 1.0, 0.0).astype(jnp.bfloat16)

    def head_sums(xs, two_terms):
        if not two_terms:
            sums = jnp.dot(jnp.concatenate([x.astype(jnp.bfloat16) for x in xs], axis=0), seg,
                           preferred_element_type=jnp.float32)
            return [sums[i * D_HEAD:(i + 1) * D_HEAD] for i in range(len(xs))]
        parts = []
        for x in xs:
            hi = x.astype(jnp.bfloat16)
            parts += [hi, (x - hi.astype(jnp.float32)).astype(jnp.bfloat16)]
        both = jnp.dot(jnp.concatenate(parts, axis=0), seg, preferred_element_type=jnp.float32)
        return [both[i * LANES:i * LANES + D_HEAD] + both[i * LANES + D_HEAD:(i + 1) * LANES] for i in range(len(xs))]

    grp = min(tc, SUBLANES)

    def steps(gi, carry):
        rows = pl.ds(pl.multiple_of(gi * grp, grp), grp)
        lanes = lambda p: slice(p * LANES, (p + 1) * LANES)
        load = lambda ref: [ref[bi, rows, lanes(p)] for bi, p in tiles]
        r_g, w_g, k_g, kk_g, b_g, v_g = load(r_ref), load(w_ref), load(k_ref), load(kk_ref), load(b_ref), load(v_ref)
        states = [s_ref[i] for i in range(len(tiles))]
        yrows = [[] for _ in tiles]
        for j in range(grp):
            row = slice(j, j + 1)
            sks = head_sums([s * kk[row] for s, kk in zip(states, kk_g)], two_terms=True)
            for i in range(len(tiles)):
                vcol = jnp.sum(jnp.where(eye128, v_g[i][row], 0.0), axis=1, keepdims=True)
                vmat = jnp.where(first, vcol[0:D_HEAD], vcol[D_HEAD:LANES])
                states[i] = states[i] * w_g[i][row] - sks[i] * b_g[i][row] + vmat * k_g[i][row]
            ys = head_sums([s * r[row] for s, r in zip(states, r_g)], two_terms=False)
            for i in range(len(tiles)):
                yrows[i].append(jnp.sum(jnp.where(eye2, ys[i], 0.0), axis=0, keepdims=True))
        for i, (bi, p) in enumerate(tiles):
            y_ref[bi, rows, lanes(p)] = yrows[i][0] if grp == 1 else jnp.concatenate(yrows[i], axis=0)
            s_ref[i] = states[i]
        return carry

    lax.fori_loop(0, tc // grp, steps, 0)

    @pl.when(ci == pl.num_programs(1) - 1)
    def _():
        for i, (bi, p) in enumerate(tiles):
            sT_ref[bi, p] = s_ref[i]


def rwkv_scan(r, w, k, kk, bb, v, s0, nb, tc):
    b, t, _ = r.shape
    f32 = jnp.float32
    pair = lambda s: s.reshape(b, N_PAIR, 2, D_HEAD, D_HEAD).transpose(0, 1, 3, 2, 4).reshape(b, N_PAIR, D_HEAD, LANES)
    unpair = lambda s: s.reshape(b, N_PAIR, D_HEAD, 2, D_HEAD).transpose(0, 1, 3, 2, 4).reshape(b, H_RWKV, D_HEAD, D_HEAD)
    seq_spec = pl.BlockSpec((nb, tc, W_RWKV), lambda bi, ci: (bi, ci, 0))
    st_spec = pl.BlockSpec((nb, N_PAIR, D_HEAD, LANES), lambda bi, ci: (bi, 0, 0, 0))
    y, s_fin = pl.pallas_call(
        _rwkv_scan_kernel,
        grid=(b // nb, t // tc),
        in_specs=[seq_spec] * 6 + [st_spec],
        out_specs=[seq_spec, st_spec],
        out_shape=[jax.ShapeDtypeStruct((b, t, W_RWKV), f32), jax.ShapeDtypeStruct((b, N_PAIR, D_HEAD, LANES), f32)],
        scratch_shapes=[pltpu.VMEM((nb * N_PAIR, D_HEAD, LANES), f32)],
        compiler_params=pltpu.CompilerParams(dimension_semantics=("parallel", "arbitrary"),
                                             vmem_limit_bytes=VMEM_LIMIT),
        name="rwkv_scan",
    )(r, w, k, kk, bb, v, pair(s0.astype(f32)))
    return y, unpair(s_fin)


def _merge_kernel(y_ref, bonus_ref, g_ref, gng_ref, gnb_ref, seg_ref, a_ref, x_ref, wr_ref, wa_ref, nf_ref, h_o, xn_o):
    y = y_ref[...]
    inv = 1.0 / D_HEAD
    mean = jnp.dot(y, seg_ref[...], precision=HI, preferred_element_type=jnp.float32) * inv
    dlt = y - mean
    var = jnp.dot(dlt * dlt, seg_ref[...], precision=HI, preferred_element_type=jnp.float32) * inv
    yn = dlt * lax.rsqrt(var + GN_EPS) * gng_ref[...] + gnb_ref[...]
    r_out = (yn + bonus_ref[...]) * g_ref[...]
    h = (x_ref[...] + jnp.dot(r_out.astype(jnp.bfloat16), wr_ref[...], preferred_element_type=jnp.float32)
         + jnp.dot(a_ref[...].astype(jnp.bfloat16), wa_ref[...], preferred_element_type=jnp.float32))
    h_o[...] = h
    xn_o[...] = _rms_norm(h, nf_ref[...]).astype(jnp.bfloat16)


def merge(y, bonus, g, gn_g, gn_b, a_out, x, w_out_bf16, norm_ffn, tt):
    n, d = x.shape
    head = jnp.arange(W_RWKV) // D_HEAD
    seg = (head[:, None] == head[None, :]).astype(jnp.float32)
    row = lambda w: pl.BlockSpec((tt, w), lambda i: (i, 0))
    vec = lambda w: pl.BlockSpec((1, w), lambda i: (0, 0))
    mat = lambda r, c: pl.BlockSpec((r, c), lambda i: (0, 0))
    return pl.pallas_call(
        _merge_kernel,
        grid=(n // tt,),
        in_specs=[row(W_RWKV), row(W_RWKV), row(W_RWKV), vec(W_RWKV), vec(W_RWKV), mat(W_RWKV, W_RWKV),
                  row(W_ATT), row(d), mat(W_RWKV, d), mat(W_ATT, d), vec(d)],
        out_specs=[row(d), row(d)],
        out_shape=[jax.ShapeDtypeStruct((n, d), jnp.float32), jax.ShapeDtypeStruct((n, d), jnp.bfloat16)],
        compiler_params=pltpu.CompilerParams(dimension_semantics=("parallel",), vmem_limit_bytes=VMEM_LIMIT),
        name="merge",
    )(y, bonus, g, gn_g.reshape(1, -1), gn_b.reshape(1, -1), seg, a_out, x, w_out_bf16[:W_RWKV], w_out_bf16[W_RWKV:],
      norm_ffn.reshape(1, d))


def _moba_prompt_kernel(q_ref, k_ref, v_ref, o_ref, vt_ref, kb_ref, sel_ref, m_ref, l_ref, acc_ref):
    qi = pl.program_id(2)
    t_len = k_ref.shape[1]
    nb = t_len // MOBA_BLOCK
    scale = D_HEAD ** -0.5

    @pl.when(qi == 0)
    def _():
        vt_ref[...] = v_ref[0].T.astype(jnp.bfloat16)
        kb_ref[...] = k_ref[0].astype(jnp.bfloat16)

    q2 = q_ref[0]
    lane = lax.broadcasted_iota(jnp.int32, q2.shape, 1)
    kmean = jnp.mean(k_ref[0].reshape(nb, MOBA_BLOCK, LANES), axis=1)
    blk_row = lax.broadcasted_iota(jnp.int32, (nb, MOBA_BLOCK), 0)
    key_row = lax.broadcasted_iota(jnp.int32, (MOBA_BLOCK, MOBA_BLOCK), 0)
    qry_col = lax.broadcasted_iota(jnp.int32, (MOBA_BLOCK, MOBA_BLOCK), 1)
    qms = []
    for s in range(2):
        qm = jnp.where((lane // D_HEAD) == s, q2, 0.0)
        qms.append(qm.astype(jnp.bfloat16))
        gate = lax.dot_general(kmean, qm, NT_DIMS, precision=lax.Precision.HIGHEST,
                               preferred_element_type=jnp.float32)
        rank = jnp.zeros(gate.shape, jnp.float32)
        for jp in range(nb):
            g_jp = gate[jp:jp + 1, :]
            ahead = (g_jp > gate) | ((g_jp == gate) & (jp < blk_row))
            rank = rank + jnp.where(ahead & (jp < qi), 1.0, 0.0)
        sel_ref[s] = jnp.where((rank < float(MOBA_TOPK)) & (blk_row < qi), 1.0, 0.0)

        own = pl.ds(pl.multiple_of(qi * MOBA_BLOCK, MOBA_BLOCK), MOBA_BLOCK)
        st = lax.dot_general(kb_ref[own, :], qms[s], NT_DIMS, preferred_element_type=jnp.float32) * scale
        st = jnp.where(key_row <= qry_col, st, NEG_INF)
        m0 = jnp.max(st, axis=0, keepdims=True)
        p = jnp.exp(st - m0)
        m_ref[s] = m0
        l_ref[s] = jnp.sum(p, axis=0, keepdims=True)
        acc_ref[s] = jnp.dot(vt_ref[:, own], p.astype(jnp.bfloat16), preferred_element_type=jnp.float32)

    def past_block(j, carry):
        blk = pl.ds(pl.multiple_of(j * MOBA_BLOCK, MOBA_BLOCK), MOBA_BLOCK)
        k_j = kb_ref[blk, :]
        vt_j = vt_ref[:, blk]
        for s in range(2):
            st = lax.dot_general(k_j, qms[s], NT_DIMS, preferred_element_type=jnp.float32) * scale
            st = jnp.where(sel_ref[s, pl.ds(j, 1), :] > 0.0, st, NEG_INF)
            m_old = m_ref[s]
            m_new = jnp.maximum(m_old, jnp.max(st, axis=0, keepdims=True))
            alpha = jnp.exp(m_old - m_new)
            p = jnp.exp(st - m_new)
            m_ref[s] = m_new
            l_ref[s] = alpha * l_ref[s] + jnp.sum(p, axis=0, keepdims=True)
            acc_ref[s] = alpha * acc_ref[s] + jnp.dot(vt_j, p.astype(jnp.bfloat16), preferred_element_type=jnp.float32)
        return carry

    lax.fori_loop(0, qi, past_block, 0)

    row = lax.broadcasted_iota(jnp.int32, (LANES, MOBA_BLOCK), 0)
    out_t = jnp.where((row // D_HEAD) == 0, acc_ref[0] / l_ref[0], acc_ref[1] / l_ref[1])
    o_ref[0] = out_t.T


def moba_prompt(q, k, v):
    b, t, w = q.shape
    nb = t // MOBA_BLOCK
    return pl.pallas_call(
        _moba_prompt_kernel,
        grid=(b, w // LANES, nb),
        in_specs=[
            pl.BlockSpec((1, MOBA_BLOCK, LANES), lambda bi, hp, qi: (bi, qi, hp)),
            pl.BlockSpec((1, t, LANES), lambda bi, hp, qi: (bi, 0, hp)),
            pl.BlockSpec((1, t, LANES), lambda bi, hp, qi: (bi, 0, hp)),
        ],
        out_specs=pl.BlockSpec((1, MOBA_BLOCK, LANES), lambda bi, hp, qi: (bi, qi, hp)),
        out_shape=jax.ShapeDtypeStruct((b, t, w), jnp.float32),
        scratch_shapes=[
            pltpu.VMEM((LANES, t), jnp.bfloat16),
            pltpu.VMEM((t, LANES), jnp.bfloat16),
            pltpu.VMEM((2, nb, MOBA_BLOCK), jnp.float32),
            pltpu.VMEM((2, 1, MOBA_BLOCK), jnp.float32),
            pltpu.VMEM((2, 1, MOBA_BLOCK), jnp.float32),
            pltpu.VMEM((2, LANES, MOBA_BLOCK), jnp.float32),
        ],
        compiler_params=pltpu.CompilerParams(dimension_semantics=("parallel", "parallel", "arbitrary"),
                                             vmem_limit_bytes=VMEM_LIMIT),
        name="moba_prompt",
    )(q, k, v)


PAGES_PER_BLOCK = MOBA_BLOCK // PAGE_SIZE


def _sample_select_kernel(q_ref, km_ref, idx_ref):
    gate = jnp.sum(km_ref[0] * q_ref[...], axis=-1, keepdims=True)
    nb = gate.shape[0]
    blocks = lax.broadcasted_iota(jnp.int32, gate.shape, 0).astype(jnp.float32)
    for i in range(MOBA_TOPK):
        m = jnp.max(gate, axis=0, keepdims=True)
        first = jnp.min(jnp.where(gate == m, blocks, float(nb)), axis=0, keepdims=True)
        gate = jnp.where(blocks == first, NEG_INF, gate)
        idx_ref[0, i] = first[0]


def sample_select(q, kmean):
    bd, nb, h, dh = kmean.shape
    idx = pl.pallas_call(
        _sample_select_kernel,
        grid=(bd,),
        in_specs=[pl.BlockSpec((1, h, dh), lambda b: (b, 0, 0)), pl.BlockSpec((1, nb, h, dh), lambda b: (b, 0, 0, 0))],
        out_specs=pl.BlockSpec((1, MOBA_TOPK, h, 1), lambda b: (b, 0, 0, 0)),
        out_shape=jax.ShapeDtypeStruct((bd, MOBA_TOPK, h, 1), jnp.float32),
        compiler_params=pltpu.CompilerParams(dimension_semantics=("parallel",), vmem_limit_bytes=VMEM_LIMIT),
        name="sample_select",
    )(q, kmean)
    return idx[..., 0].astype(jnp.int32).transpose(0, 2, 1)


def _sample_attend_kernel(q_ref, ko_ref, vo_ref, ks_ref, vs_ref, o_ref):
    scale = D_HEAD ** -0.5
    for h in range(H_ATT):
        qh = q_ref[0, h:h + 1, :]
        s_sel = jnp.sum(ks_ref[0, h] * qh, axis=1, keepdims=True) * scale
        s_own = jnp.sum(ko_ref[0, h:h + 1, :] * qh, axis=1, keepdims=True) * scale
        m = jnp.maximum(jnp.max(s_sel, axis=0, keepdims=True), s_own)
        p_sel = jnp.exp(s_sel - m)
        p_own = jnp.exp(s_own - m)
        denom = jnp.sum(p_sel, axis=0, keepdims=True) + p_own
        acc = jnp.sum(p_sel * vs_ref[0, h], axis=0, keepdims=True) + p_own * vo_ref[0, h:h + 1, :]
        o_ref[0, h:h + 1, :] = acc / denom


def sample_attend(q, k_own, v_own, k_sel, v_sel):
    bd, h, n_keys, dh = k_sel.shape
    tok = pl.BlockSpec((1, h, dh), lambda b: (b, 0, 0))
    sel = pl.BlockSpec((1, h, n_keys, dh), lambda b: (b, 0, 0, 0))
    return pl.pallas_call(
        _sample_attend_kernel,
        grid=(bd,),
        in_specs=[tok, tok, tok, sel, sel],
        out_specs=tok,
        out_shape=jax.ShapeDtypeStruct((bd, h, dh), jnp.float32),
        compiler_params=pltpu.CompilerParams(dimension_semantics=("parallel",), vmem_limit_bytes=VMEM_LIMIT),
        name="sample_attend",
    )(q, k_own, v_own, k_sel, v_sel)


def moba_sample(q, k, v, cache_k, cache_v, page_table):
    bd, s, w = q.shape
    assert s == 1 and PAST_LEN % MOBA_BLOCK == 0 and PAST_LEN // MOBA_BLOCK >= MOBA_TOPK
    heads = lambda z: z.reshape(bd, H_ATT, D_HEAD)
    n_blocks = page_table.shape[1] // PAGES_PER_BLOCK
    kmean = jnp.mean(cache_k[page_table].reshape(bd, n_blocks, MOBA_BLOCK, H_ATT, D_HEAD), axis=2)
    sel = sample_select(heads(q), kmean)
    logical = sel[..., None] * PAGES_PER_BLOCK + jnp.arange(PAGES_PER_BLOCK)
    phys = page_table[jnp.arange(bd)[:, None, None, None], logical]
    hi = jnp.arange(H_ATT)[None, :, None, None]
    gather = lambda c: c[phys, :, hi, :].reshape(bd, H_ATT, MOBA_TOPK * MOBA_BLOCK, D_HEAD)
    out = sample_attend(heads(q), heads(k), heads(v), gather(cache_k), gather(cache_v))
    return out.reshape(bd, 1, w)


PEER_CANDS = [(a, b) for a in range(PEER_TOPK) for b in range(PEER_TOPK) if (a + 1) * (b + 1) <= PEER_TOPK]
PEER_NCAND = -(-len(PEER_CANDS) // 8) * 8


def _extract_topk(x, n_rows, k):
    rows = lax.broadcasted_iota(jnp.int32, x.shape, 0).astype(jnp.float32)
    vals = []
    for _ in range(k):
        m = jnp.max(x, axis=0, keepdims=True)
        first = jnp.min(jnp.where(x == m, rows, float(n_rows)), axis=0, keepdims=True)
        x = jnp.where(rows == first, NEG_INF, x)
        vals.append(m)
    return x, vals


def _peer_route_kernel(xn_ref, wpq_ref, keys_ref, t1_ref, s2_ref, e1_ref, e2_ref, sv_ref, comb_ref):
    q = jnp.dot(xn_ref[...], wpq_ref[...], preferred_element_type=jnp.float32)
    half = PEER_DQ // 2
    for h in range(PEER_HEADS):
        masked = []
        for p in range(2):
            qs = q[:, (2 * h + p) * half:(2 * h + p + 1) * half].astype(jnp.bfloat16)
            st = lax.dot_general(keys_ref[p], qs, (((1,), (1,)), ((), ())),
                                 preferred_element_type=jnp.float32)
            rem, vals = _extract_topk(st, PEER_NKEYS, PEER_TOPK)
            for i, v in enumerate(vals):
                sv_ref[p, i:i + 1, :] = v
            masked.append(jnp.where(rem == NEG_INF, st, NEG_INF))
        comb_ref[...] = jnp.full(comb_ref.shape, NEG_INF, jnp.float32)
        for c, (a, b) in enumerate(PEER_CANDS):
            comb_ref[c:c + 1, :] = sv_ref[0, a:a + 1, :] + sv_ref[1, b:b + 1, :]
        _, cvals = _extract_topk(comb_ref[...], PEER_NCAND, PEER_TOPK + 1)
        cmax = cvals[0]
        z = jnp.zeros_like(cmax)
        for v in cvals[:PEER_TOPK]:
            z = z + jnp.exp(v - cmax)
        cut = 0.5 * (cvals[PEER_TOPK - 1] + cvals[PEER_TOPK])
        t1_ref[h] = cut - masked[0]
        s2_ref[h] = masked[1]
        e1_ref[h] = jnp.exp(masked[0] - sv_ref[0, 0:1, :])
        e2_ref[h] = jnp.exp(masked[1] - sv_ref[1, 0:1, :]) / z


def peer_route(xn_bf16, wpq_bf16, keys_bf16, tm):
    n = xn_bf16.shape[0]
    tab = jax.ShapeDtypeStruct((PEER_HEADS, PEER_NKEYS, n), jnp.float32)
    tab_spec = pl.BlockSpec((PEER_HEADS, PEER_NKEYS, tm), lambda i: (0, 0, i))
    return pl.pallas_call(
        _peer_route_kernel,
        grid=(n // tm,),
        in_specs=[
            pl.BlockSpec((tm, D_MODEL), lambda i: (i, 0)),
            pl.BlockSpec((D_MODEL, PEER_HEADS * PEER_DQ), lambda i: (0, 0)),
            pl.BlockSpec((2, PEER_NKEYS, PEER_DQ // 2), lambda i: (0, 0, 0)),
        ],
        out_specs=[tab_spec] * 4,
        out_shape=[tab] * 4,
        scratch_shapes=[pltpu.VMEM((2, PEER_TOPK, tm), jnp.float32),
                        pltpu.VMEM((PEER_NCAND, tm), jnp.float32)],
        compiler_params=pltpu.CompilerParams(dimension_semantics=("parallel",),
                                             vmem_limit_bytes=VMEM_LIMIT),
        name="peer_route",
    )(xn_bf16, wpq_bf16, keys_bf16)


def _gelu_exact(x):
    return 0.5 * x * (1.0 + lax.erf(x * (1.0 / math.sqrt(2.0))))


def _peer_dense_kernel(xn_ref, u_ref, vt_ref, t1_ref, s2_ref, e1_ref, e2_ref, h_ref, nfin_ref, o_ref, acc_ref, p_ref):
    j = pl.program_id(1)
    n_e1 = u_ref.shape[0] // PEER_NKEYS
    tm = xn_ref.shape[0]

    @pl.when(j == 0)
    def _():
        acc_ref[...] = jnp.zeros_like(acc_ref)

    for eb in range(n_e1 // PEER_ELB):
        erows = slice(eb * PEER_ELB * PEER_NKEYS, (eb + 1) * PEER_ELB * PEER_NKEYS)
        at = lax.dot_general(u_ref[erows, :], xn_ref[...], NT_DIMS,
                             preferred_element_type=jnp.float32)
        for c in range(tm // LANES):
            cols = slice(c * LANES, (c + 1) * LANES)
            for r0 in range(0, PEER_NKEYS, PEER_SGR):
                accs = [jnp.zeros((PEER_SGR, LANES), jnp.float32) for _ in range(PEER_ELB)]
                for h in range(PEER_HEADS):
                    s2t = s2_ref[h, r0:r0 + PEER_SGR, cols]
                    e2t = e2_ref[h, r0:r0 + PEER_SGR, cols]
                    for i in range(PEER_ELB):
                        el = eb * PEER_ELB + i
                        picked = s2t >= t1_ref[h, el:el + 1, cols]
                        accs[i] = accs[i] + jnp.where(picked, e2t, 0.0) * e1_ref[h, el:el + 1, cols]
                for i in range(PEER_ELB):
                    lo = i * PEER_NKEYS + r0
                    act = _gelu_exact(at[lo:lo + PEER_SGR, cols])
                    p_ref[eb, lo:lo + PEER_SGR, cols] = (accs[i] * act).astype(jnp.bfloat16)
        acc_ref[...] += jnp.dot(vt_ref[:, erows], p_ref[eb], preferred_element_type=jnp.float32)

    @pl.when(j == pl.num_programs(1) - 1)
    def _():
        o_ref[...] = _rms_norm(h_ref[...] + acc_ref[...].T, nfin_ref[...])


def peer_dense(xn_bf16, u_bf16, vt_bf16, t1, s2, e1, e2, h, norm_final, tm, te):
    n = xn_bf16.shape[0]
    n_e1 = te // PEER_NKEYS
    return pl.pallas_call(
        _peer_dense_kernel,
        grid=(n // tm, PEER_EXPERTS // te),
        in_specs=[
            pl.BlockSpec((tm, D_MODEL), lambda i, j: (i, 0)),
            pl.BlockSpec((te, D_MODEL), lambda i, j: (j, 0)),
            pl.BlockSpec((D_MODEL, te), lambda i, j: (0, j)),
            pl.BlockSpec((PEER_HEADS, n_e1, tm), lambda i, j: (0, j, i)),
            pl.BlockSpec((PEER_HEADS, PEER_NKEYS, tm), lambda i, j: (0, 0, i)),
            pl.BlockSpec((PEER_HEADS, n_e1, tm), lambda i, j: (0, j, i)),
            pl.BlockSpec((PEER_HEADS, PEER_NKEYS, tm), lambda i, j: (0, 0, i)),
            pl.BlockSpec((tm, D_MODEL), lambda i, j: (i, 0)),
            pl.BlockSpec((1, D_MODEL), lambda i, j: (0, 0)),
        ],
        out_specs=pl.BlockSpec((tm, D_MODEL), lambda i, j: (i, 0)),
        out_shape=jax.ShapeDtypeStruct((n, D_MODEL), jnp.float32),
        scratch_shapes=[pltpu.VMEM((D_MODEL, tm), jnp.float32),
                        pltpu.VMEM((n_e1 // PEER_ELB, PEER_ELB * PEER_NKEYS, tm), jnp.bfloat16)],
        compiler_params=pltpu.CompilerParams(dimension_semantics=("parallel", "arbitrary"),
                                             vmem_limit_bytes=VMEM_LIMIT),
        name="peer_dense",
    )(xn_bf16, u_bf16, vt_bf16, t1, s2, e1, e2, h, norm_final.reshape(1, D_MODEL))


def peer_block(h, xn_bf16, wpq_bf16, keys_bf16, u_bf16, vt_bf16, norm_final):
    n = h.shape[0]
    tm = PEER_TM if n >= PEER_TM else PEER_TM_SMALL
    pad = -n % tm
    hp, xb = jnp.pad(h, ((0, pad), (0, 0))), jnp.pad(xn_bf16, ((0, pad), (0, 0)))
    t1, s2, e1, e2 = peer_route(xb, wpq_bf16, keys_bf16, PEER_TM_SMALL)
    return peer_dense(xb, u_bf16, vt_bf16, t1, s2, e1, e2, hp, norm_final, tm, PEER_TE)[:n]


def layer(x, pos, prev0, s0, attend, norm_mix, w_in_bf16, rw, w_out_bf16, norm_ffn, peer, norm_final):
    mu, w0, w_dec, a0, w_aaa, w_gate, k_k, k_a, r_k, gn_g, gn_b = rw
    b, t, d = x.shape
    n = b * t
    if t == 1:
        xr, posr = x.reshape(1, b, d), jnp.broadcast_to(pos, (b,))
    else:
        xr, posr = x, pos
    seq = lambda z: z.reshape(b, t, z.shape[-1])
    pr, q, k, v = map(seq, project(xr, norm_mix, w_in_bf16, posr, min(xr.shape[1], PROJ_TT)))
    r, w, k2, kk, bb, vv, g, bonus = rwkv_prep(pr, prev0, mu, w0, w_dec, a0, w_aaa, w_gate, k_k, k_a, r_k, min(t, 256))
    y, s_new = rwkv_scan(r, w, k2, kk, bb, vv, s0, RWKV_NB, min(t, 128))
    a_out = attend(q, k, v)
    flat = lambda z: z.reshape(n, z.shape[-1])
    h, xn = merge(flat(y), flat(bonus), flat(g), gn_g, gn_b, flat(a_out), flat(x), w_out_bf16, norm_ffn, min(n, 512))
    out = peer_block(h, xn, *peer, norm_final)
    heads = lambda z: z.reshape(b, t, H_ATT, D_HEAD)
    return out.reshape(b, t, d), heads(k), heads(v), s_new, pr[:, -1]


def kernel(x_prompt, x_sample, cache_k, cache_v, page_table, state_wkv, state_shift, norm_mix, w_in, mu_shift, w0, w_decay_up, a0, w_aaa_up, w_gate_up, k_k, k_a, r_k, gn_gain, gn_bias, w_out, norm_ffn, w_pq, peer_sub_keys, expert_u, expert_v, norm_final):
    l = 0
    bf16 = jnp.bfloat16
    rw = (mu_shift[l], w0[l], w_decay_up[l], a0[l], w_aaa_up[l], w_gate_up[l], k_k[l], k_a[l], r_k[l], gn_gain[l], gn_bias[l])
    peer = (w_pq[l].astype(bf16), peer_sub_keys[l].astype(bf16), expert_u[l].astype(bf16), expert_v[l].T.astype(bf16))
    shared = (norm_mix[l], w_in[l].astype(bf16), rw, w_out[l].astype(bf16), norm_ffn[l], peer, norm_final)
    bp, tp, _ = x_prompt.shape
    bs, ts, _ = x_sample.shape
    y_p, k_p, v_p, s_p, sh_p = layer(
        x_prompt, jnp.arange(tp, dtype=jnp.int32), jnp.zeros((bp, W_RWKV_IN), jnp.float32),
        jnp.zeros((bp, H_RWKV, D_HEAD, D_HEAD), jnp.float32), moba_prompt, *shared)
    attend_cache = lambda q, k, v: moba_sample(q, k, v, cache_k[l], cache_v[l], page_table)
    y_s, k_s, v_s, s_s, sh_s = layer(
        x_sample, PAST_LEN + jnp.arange(ts, dtype=jnp.int32), state_shift[l], state_wkv[l], attend_cache, *shared)
    return (y_p, y_s, k_p[None], v_p[None], s_p[None], sh_p[None], k_s[None], v_s[None], s_s[None], sh_s[None])
```

```python
import functools
import math

import jax
import jax.numpy as jnp
from jax import lax
from jax.experimental import pallas as pl
from jax.experimental.pallas import tpu as pltpu

D_MODEL = 1024
DEPTH = 1
PAST_LEN = 16384
PAGE_SIZE = 128
D_HEAD = 64
H_RWKV = 8
H_ATT = 8
W_RWKV = H_RWKV * D_HEAD
W_ATT = H_ATT * D_HEAD
D_DECAY_LORA = 64
D_AAA_LORA = 64
D_GATE_LORA = 128
W_RWKV_IN = 3 * W_RWKV + D_DECAY_LORA + D_AAA_LORA + D_GATE_LORA
W_IN = W_RWKV_IN + 3 * W_ATT
GN_EPS = 64e-5
NORM_EPS = 1e-6
MOBA_BLOCK = 256
MOBA_TOPK = 3
Q_CHUNK = 64
ROT_DIM = D_HEAD // 4
ROPE_THETA = 500000.0
PEER_HEADS = 8
PEER_NKEYS = 128
PEER_TOPK = 16
PEER_DQ = 256
PEER_EXPERTS = PEER_NKEYS * PEER_NKEYS
PEER_TM = 512
PEER_TM_SMALL = 256
PEER_TE = 1024
PEER_ELB = 4
PEER_SGR = 32
NEG_INF = float("-inf")
VMEM_LIMIT = 56 * 1024 * 1024
LANES = 128
SUBLANES = 8
PROJ_TT = 512
RWKV_NB = 4
NT_DIMS = (((1,), (1,)), ((), ()))


def _rms_norm(x, g):
    return x * lax.rsqrt(jnp.mean(x * x, axis=-1, keepdims=True) + NORM_EPS) * g


def rope_tables(pos):
    half = ROT_DIM // 2
    inv = 1.0 / (ROPE_THETA ** (jnp.arange(half, dtype=jnp.float32) * 2.0 / ROT_DIM))
    ang = pos.astype(jnp.float32)[:, None] * inv[None, :]
    lane = jnp.arange(LANES) % D_HEAD
    idx = lane % half
    cos = jnp.where(lane[None, :] < ROT_DIM, jnp.cos(ang)[:, idx], 1.0)
    sin = jnp.sin(ang)[:, idx]
    sin_lo = jnp.where(lane[None, :] < half, -sin, 0.0)
    sin_hi = jnp.where((lane[None, :] >= half) & (lane[None, :] < ROT_DIM), sin, 0.0)
    return cos, sin_lo, sin_hi


def _project_kernel(x_ref, g_ref, w_ref, cos_ref, slo_ref, shi_ref, pr_o, q_o, k_o, v_o):
    xn = _rms_norm(x_ref[0], g_ref[...])
    p = jnp.dot(xn.astype(jnp.bfloat16), w_ref[...], preferred_element_type=jnp.float32)
    pr_o[0] = p[:, :W_RWKV_IN]
    reps = W_ATT // LANES
    cos = jnp.concatenate([cos_ref[...]] * reps, axis=1)
    slo = jnp.concatenate([slo_ref[...]] * reps, axis=1)
    shi = jnp.concatenate([shi_ref[...]] * reps, axis=1)
    half = ROT_DIM // 2

    def rope(z):
        ahead = pltpu.roll(z, W_ATT - half, 1)
        behind = pltpu.roll(z, half, 1)
        return z * cos + ahead * slo + behind * shi

    q_o[0] = rope(p[:, W_RWKV_IN:W_RWKV_IN + W_ATT])
    k_o[0] = rope(p[:, W_RWKV_IN + W_ATT:W_RWKV_IN + 2 * W_ATT])
    v_o[0] = p[:, W_RWKV_IN + 2 * W_ATT:]


def project(x, norm_g, w_in_bf16, pos, tt):
    b, t, d = x.shape
    cos, slo, shi = rope_tables(pos)
    f32 = jnp.float32
    tab_spec = pl.BlockSpec((tt, LANES), lambda bi, ti: (ti, 0))
    out_spec = lambda w: pl.BlockSpec((1, tt, w), lambda bi, ti: (bi, ti, 0))
    return pl.pallas_call(
        _project_kernel,
        grid=(b, t // tt),
        in_specs=[
            pl.BlockSpec((1, tt, d), lambda bi, ti: (bi, ti, 0)),
            pl.BlockSpec((1, d), lambda bi, ti: (0, 0)),
            pl.BlockSpec((d, W_IN), lambda bi, ti: (0, 0)),
            tab_spec, tab_spec, tab_spec,
        ],
        out_specs=[out_spec(W_RWKV_IN), out_spec(W_ATT), out_spec(W_ATT), out_spec(W_ATT)],
        out_shape=[jax.ShapeDtypeStruct((b, t, W_RWKV_IN), f32)] + [jax.ShapeDtypeStruct((b, t, W_ATT), f32)] * 3,
        compiler_params=pltpu.CompilerParams(dimension_semantics=("parallel", "parallel"),
                                             vmem_limit_bytes=VMEM_LIMIT),
        name="project",
    )(x, norm_g.reshape(1, d), w_in_bf16, cos, slo, shi)


N_PAIR = W_RWKV // LANES
HI = lax.Precision.HIGHEST


def _sigmoid(x):
    return 1.0 / (1.0 + jnp.exp(-x))


def _softplus(x):
    return jnp.maximum(x, 0.0) + jnp.log(1.0 + jnp.exp(-jnp.abs(x)))


def _rwkv_prep_kernel(p_ref, prev0_ref, mu_ref, w0_ref, wdec_ref, a0_ref, waaa_ref, wgate_ref, kk_w_ref, ka_ref,
                      rk_ref, seg_ref, r_o, w_o, k_o, kk_o, b_o, v_o, g_o, bonus_o, carry_ref):
    ti = pl.program_id(1)

    @pl.when(ti == 0)
    def _():
        carry_ref[...] = prev0_ref[0]

    p = p_ref[0]
    row = lax.broadcasted_iota(jnp.int32, p.shape, 0)
    if p.shape[0] == 1:
        prev = carry_ref[...]
    else:
        prev = jnp.where(row == 0, carry_ref[...], pltpu.roll(p, 1, 0))
    carry_ref[...] = p[p.shape[0] - 1:, :]
    ps = p + (prev - p) * mu_ref[...]
    r = ps[:, 0:W_RWKV]
    k = ps[:, W_RWKV:2 * W_RWKV]
    v = ps[:, 2 * W_RWKV:3 * W_RWKV]
    lora = ps[:, 3 * W_RWKV:3 * W_RWKV + LANES]
    gd = ps[:, 3 * W_RWKV + LANES:]
    dec = jnp.dot(jnp.tanh(lora).astype(jnp.bfloat16), wdec_ref[...], preferred_element_type=jnp.float32)
    logw = -jnp.exp(-_softplus(-(w0_ref[...] + dec)) - 0.5)
    a = _sigmoid(a0_ref[...] + jnp.dot(lora.astype(jnp.bfloat16), waaa_ref[...], preferred_element_type=jnp.float32))
    g = jnp.dot(_sigmoid(gd).astype(jnp.bfloat16), wgate_ref[...], preferred_element_type=jnp.float32)
    kk = k * kk_w_ref[...]
    sumsq = jnp.dot(kk * kk, seg_ref[...], precision=HI, preferred_element_type=jnp.float32)
    kk = kk / jnp.maximum(jnp.sqrt(sumsq), 1e-12)
    k2 = k * (1.0 + (a - 1.0) * ka_ref[...])
    rkk = jnp.dot(r * k2 * rk_ref[...], seg_ref[...], precision=HI, preferred_element_type=jnp.float32)
    r_o[0] = r
    w_o[0] = jnp.exp(logw)
    k_o[0] = k2
    kk_o[0] = kk
    b_o[0] = kk * a
    v_o[0] = v
    g_o[0] = g
    bonus_o[0] = rkk * v


def rwkv_prep(p, prev0, mu, w0, w_dec, a0, w_aaa, w_gate, k_k, k_a, r_k, tt):
    b, t, _ = p.shape
    f32 = jnp.float32
    wdec_pad = jnp.concatenate([w_dec, jnp.zeros_like(w_aaa)], axis=0).astype(jnp.bfloat16)
    waaa_pad = jnp.concatenate([jnp.zeros_like(w_dec), w_aaa], axis=0).astype(jnp.bfloat16)
    head = jnp.arange(W_RWKV) // D_HEAD
    seg = (head[:, None] == head[None, :]).astype(f32)
    row = lambda z: z.reshape(1, -1).astype(f32)
    vec_spec = lambda n: pl.BlockSpec((1, n), lambda bi, ti: (0, 0))
    mat_spec = lambda m, n: pl.BlockSpec((m, n), lambda bi, ti: (0, 0))
    out_spec = pl.BlockSpec((1, tt, W_RWKV), lambda bi, ti: (bi, ti, 0))
    out = jax.ShapeDtypeStruct((b, t, W_RWKV), f32)
    return pl.pallas_call(
        _rwkv_prep_kernel,
        grid=(b, t // tt),
        in_specs=[
            pl.BlockSpec((1, tt, W_RWKV_IN), lambda bi, ti: (bi, ti, 0)),
            pl.BlockSpec((1, 1, W_RWKV_IN), lambda bi, ti: (bi, 0, 0)),
            vec_spec(W_RWKV_IN), vec_spec(W_RWKV), mat_spec(LANES, W_RWKV), vec_spec(W_RWKV),
            mat_spec(LANES, W_RWKV), mat_spec(D_GATE_LORA, W_RWKV), vec_spec(W_RWKV), vec_spec(W_RWKV),
            vec_spec(W_RWKV), mat_spec(W_RWKV, W_RWKV),
        ],
        out_specs=[out_spec] * 8,
        out_shape=[out] * 8,
        scratch_shapes=[pltpu.VMEM((1, W_RWKV_IN), f32)],
        compiler_params=pltpu.CompilerParams(dimension_semantics=("parallel", "arbitrary"),
                                             vmem_limit_bytes=VMEM_LIMIT),
        name="rwkv_prep",
    )(p, prev0.reshape(b, 1, W_RWKV_IN), row(mu), row(w0), wdec_pad, row(a0), waaa_pad, w_gate.astype(jnp.bfloat16),
      row(k_k), row(k_a), row(r_k), seg)


def _rwkv_scan_kernel(r_ref, w_ref, k_ref, kk_ref, b_ref, v_ref, s0_ref, y_ref, sT_ref, s_ref):
    ci = pl.program_id(1)
    nb, tc = r_ref.shape[0], r_ref.shape[1]
    tiles = [(bi, p) for bi in range(nb) for p in range(N_PAIR)]

    @pl.when(ci == 0)
    def _():
        for i, (bi, p) in enumerate(tiles):
            s_ref[i] = s0_ref[bi, p]

    lane = lax.broadcasted_iota(jnp.int32, (D_HEAD, LANES), 1)
    sub = lax.broadcasted_iota(jnp.int32, (D_HEAD, LANES), 0)
    first = lane < D_HEAD
    eye2 = (sub == (lane % D_HEAD))
    sub128 = lax.broadcasted_iota(jnp.int32, (LANES, LANES), 0)
    lane128 = lax.broadcasted_iota(jnp.int32, (LANES, LANES), 1)
    eye128 = sub128 == lane128
    seg = jnp.where((sub128 // D_HEAD) == (lane128 // D_HEAD), 1.0, 0.0).astype(jnp.bfloat16)

    def head_sums(xs):
        parts = []
        for x in xs:
            hi = x.astype(jnp.bfloat16)
            parts += [hi, (x - hi.astype(jnp.float32)).astype(jnp.bfloat16)]
        both = jnp.dot(jnp.concatenate(parts, axis=0), seg, preferred_element_type=jnp.float32)
        return [both[i * LANES:i * LANES + D_HEAD] + both[i * LANES + D_HEAD:(i + 1) * LANES] for i in range(len(xs))]

    grp = min(tc, SUBLANES)

    def steps(gi, carry):
        rows = pl.ds(pl.multiple_of(gi * grp, grp), grp)
        lanes = lambda p: slice(p * LANES, (p + 1) * LANES)
        load = lambda ref: [ref[bi, rows, lanes(p)] for bi, p in tiles]
        r_g, w_g, k_g, kk_g, b_g, v_g = load(r_ref), load(w_ref), load(k_ref), load(kk_ref), load(b_ref), load(v_ref)
        states = [s_ref[i] for i in range(len(tiles))]
        yrows = [[] for _ in tiles]
        for j in range(grp):
            row = slice(j, j + 1)
            sks = head_sums([s * kk[row] for s, kk in zip(states, kk_g)])
            for i in range(len(tiles)):
                vcol = jnp.sum(jnp.where(eye128, v_g[i][row], 0.0), axis=1, keepdims=True)
                vmat = jnp.where(first, vcol[0:D_HEAD], vcol[D_HEAD:LANES])
                states[i] = states[i] * w_g[i][row] - sks[i] * b_g[i][row] + vmat * k_g[i][row]
            ys = head_sums([s * r[row] for s, r in zip(states, r_g)])
            for i in range(len(tiles)):
                yrows[i].append(jnp.sum(jnp.where(eye2, ys[i], 0.0), axis=0, keepdims=True))
        for i, (bi, p) in enumerate(tiles):
            y_ref[bi, rows, lanes(p)] = yrows[i][0] if grp == 1 else jnp.concatenate(yrows[i], axis=0)
            s_ref[i] = states[i]
        return carry

    lax.fori_loop(0, tc // grp, steps, 0)

    @pl.when(ci == pl.num_programs(1) - 1)
    def _():
        for i, (bi, p) in enumerate(tiles):
            sT_ref[bi, p] = s_ref[i]


def rwkv_scan(r, w, k, kk, bb, v, s0, nb, tc):
    b, t, _ = r.shape
    f32 = jnp.float32
    pair = lambda s: s.reshape(b, N_PAIR, 2, D_HEAD, D_HEAD).transpose(0, 1, 3, 2, 4).reshape(b, N_PAIR, D_HEAD, LANES)
    unpair = lambda s: s.reshape(b, N_PAIR, D_HEAD, 2, D_HEAD).transpose(0, 1, 3, 2, 4).reshape(b, H_RWKV, D_HEAD, D_HEAD)
    seq_spec = pl.BlockSpec((nb, tc, W_RWKV), lambda bi, ci: (bi, ci, 0))
    st_spec = pl.BlockSpec((nb, N_PAIR, D_HEAD, LANES), lambda bi, ci: (bi, 0, 0, 0))
    y, s_fin = pl.pallas_call(
        _rwkv_scan_kernel,
        grid=(b // nb, t // tc),
        in_specs=[seq_spec] * 6 + [st_spec],
        out_specs=[seq_spec, st_spec],
        out_shape=[jax.ShapeDtypeStruct((b, t, W_RWKV), f32), jax.ShapeDtypeStruct((b, N_PAIR, D_HEAD, LANES), f32)],
        scratch_shapes=[pltpu.VMEM((nb * N_PAIR, D_HEAD, LANES), f32)],
        compiler_params=pltpu.CompilerParams(dimension_semantics=("parallel", "arbitrary"),
                                             vmem_limit_bytes=VMEM_LIMIT),
        name="rwkv_scan",
    )(r, w, k, kk, bb, v, pair(s0.astype(f32)))
    return y, unpair(s_fin)


def _merge_kernel(y_ref, bonus_ref, g_ref, gng_ref, gnb_ref, seg_ref, a_ref, x_ref, wr_ref, wa_ref, nf_ref, h_o, xn_o):
    y = y_ref[...]
    inv = 1.0 / D_HEAD
    mean = jnp.dot(y, seg_ref[...], precision=HI, preferred_element_type=jnp.float32) * inv
    dlt = y - mean
    var = jnp.dot(dlt * dlt, seg_ref[...], precision=HI, preferred_element_type=jnp.float32) * inv
    yn = dlt * lax.rsqrt(var + GN_EPS) * gng_ref[...] + gnb_ref[...]
    r_out = (yn + bonus_ref[...]) * g_ref[...]
    h = (x_ref[...] + jnp.dot(r_out.astype(jnp.bfloat16), wr_ref[...], preferred_element_type=jnp.float32)
         + jnp.dot(a_ref[...].astype(jnp.bfloat16), wa_ref[...], preferred_element_type=jnp.float32))
    h_o[...] = h
    xn_o[...] = _rms_norm(h, nf_ref[...]).astype(jnp.bfloat16)


def merge(y, bonus, g, gn_g, gn_b, a_out, x, w_out_bf16, norm_ffn, tt):
    n, d = x.shape
    head = jnp.arange(W_RWKV) // D_HEAD
    seg = (head[:, None] == head[None, :]).astype(jnp.float32)
    row = lambda w: pl.BlockSpec((tt, w), lambda i: (i, 0))
    vec = lambda w: pl.BlockSpec((1, w), lambda i: (0, 0))
    mat = lambda r, c: pl.BlockSpec((r, c), lambda i: (0, 0))
    return pl.pallas_call(
        _merge_kernel,
        grid=(n // tt,),
        in_specs=[row(W_RWKV), row(W_RWKV), row(W_RWKV), vec(W_RWKV), vec(W_RWKV), mat(W_RWKV, W_RWKV),
                  row(W_ATT), row(d), mat(W_RWKV, d), mat(W_ATT, d), vec(d)],
        out_specs=[row(d), row(d)],
        out_shape=[jax.ShapeDtypeStruct((n, d), jnp.float32), jax.ShapeDtypeStruct((n, d), jnp.bfloat16)],
        compiler_params=pltpu.CompilerParams(dimension_semantics=("parallel",), vmem_limit_bytes=VMEM_LIMIT),
        name="merge",
    )(y, bonus, g, gn_g.reshape(1, -1), gn_b.reshape(1, -1), seg, a_out, x, w_out_bf16[:W_RWKV], w_out_bf16[W_RWKV:],
      norm_ffn.reshape(1, d))


def _moba_block_kernel(q_ref, k_ref, v_ref, o_ref, *, qi):
    scale = D_HEAD ** -0.5
    nb = qi + 1
    k_all = k_ref[0]
    kb = k_all.astype(jnp.bfloat16)
    vt = v_ref[0].T.astype(jnp.bfloat16)
    q2 = q_ref[0]
    lane = lax.broadcasted_iota(jnp.int32, q2.shape, 1)
    key_row = lax.broadcasted_iota(jnp.int32, (MOBA_BLOCK, MOBA_BLOCK), 0)
    qry_col = lax.broadcasted_iota(jnp.int32, (MOBA_BLOCK, MOBA_BLOCK), 1)
    row = lax.broadcasted_iota(jnp.int32, (LANES, MOBA_BLOCK), 0)
    blk = lambda z, j: z[j * MOBA_BLOCK:(j + 1) * MOBA_BLOCK]
    outs = []
    for s in range(2):
        qm = jnp.where((lane // D_HEAD) == s, q2, 0.0)
        qmb = qm.astype(jnp.bfloat16)
        picked = [None] * qi
        if qi > MOBA_TOPK:
            kmean = jnp.concatenate([jnp.mean(blk(k_all, j), axis=0, keepdims=True) for j in range(qi)], axis=0)
            gate = lax.dot_general(kmean, qm, NT_DIMS, precision=HI, preferred_element_type=jnp.float32)
            for j in range(qi):
                ahead = jnp.zeros((1, MOBA_BLOCK), jnp.float32)
                for jp in range(qi):
                    if jp != j:
                        beats = (gate[jp:jp + 1] >= gate[j:j + 1]) if jp < j else (gate[jp:jp + 1] > gate[j:j + 1])
                        ahead = ahead + jnp.where(beats, 1.0, 0.0)
                picked[j] = ahead < float(MOBA_TOPK)
        scores = []
        for j in range(nb):
            st = lax.dot_general(blk(kb, j), qmb, NT_DIMS, preferred_element_type=jnp.float32) * scale
            if j == qi:
                st = jnp.where(key_row <= qry_col, st, NEG_INF)
            elif picked[j] is not None:
                st = jnp.where(picked[j], st, NEG_INF)
            scores.append(st)
        m = functools.reduce(jnp.maximum, [jnp.max(st, axis=0, keepdims=True) for st in scores])
        ps = [jnp.exp(st - m) for st in scores]
        denom = functools.reduce(jnp.add, [jnp.sum(p, axis=0, keepdims=True) for p in ps])
        acc = functools.reduce(jnp.add, [
            jnp.dot(vt[:, j * MOBA_BLOCK:(j + 1) * MOBA_BLOCK], ps[j].astype(jnp.bfloat16),
                    preferred_element_type=jnp.float32) for j in range(nb)])
        outs.append(acc / denom)
    o_ref[0] = jnp.where((row // D_HEAD) == 0, outs[0], outs[1]).T


def moba_prompt(q, k, v):
    b, t, w = q.shape
    outs = []
    for qi in range(t // MOBA_BLOCK):
        n_keys = (qi + 1) * MOBA_BLOCK
        outs.append(pl.pallas_call(
            functools.partial(_moba_block_kernel, qi=qi),
            grid=(b, w // LANES),
            in_specs=[
                pl.BlockSpec((1, MOBA_BLOCK, LANES), lambda bi, hp, qi=qi: (bi, qi, hp)),
                pl.BlockSpec((1, n_keys, LANES), lambda bi, hp: (bi, 0, hp)),
                pl.BlockSpec((1, n_keys, LANES), lambda bi, hp: (bi, 0, hp)),
            ],
            out_specs=pl.BlockSpec((1, MOBA_BLOCK, LANES), lambda bi, hp: (bi, 0, hp)),
            out_shape=jax.ShapeDtypeStruct((b, MOBA_BLOCK, w), jnp.float32),
            compiler_params=pltpu.CompilerParams(dimension_semantics=("parallel", "parallel"),
                                                 vmem_limit_bytes=VMEM_LIMIT),
            name=f"moba_prompt_q{qi}",
        )(q, k, v))
    return jnp.concatenate(outs, axis=1)


PAGES_PER_BLOCK = MOBA_BLOCK // PAGE_SIZE


def _sample_select_kernel(q_ref, km_ref, idx_ref):
    gate = jnp.sum(km_ref[0] * q_ref[...], axis=-1, keepdims=True)
    nb = gate.shape[0]
    blocks = lax.broadcasted_iota(jnp.int32, gate.shape, 0).astype(jnp.float32)
    for i in range(MOBA_TOPK):
        m = jnp.max(gate, axis=0, keepdims=True)
        first = jnp.min(jnp.where(gate == m, blocks, float(nb)), axis=0, keepdims=True)
        gate = jnp.where(blocks == first, NEG_INF, gate)
        idx_ref[0, i] = first[0]


def sample_select(q, kmean):
    bd, nb, h, dh = kmean.shape
    idx = pl.pallas_call(
        _sample_select_kernel,
        grid=(bd,),
        in_specs=[pl.BlockSpec((1, h, dh), lambda b: (b, 0, 0)), pl.BlockSpec((1, nb, h, dh), lambda b: (b, 0, 0, 0))],
        out_specs=pl.BlockSpec((1, MOBA_TOPK, h, 1), lambda b: (b, 0, 0, 0)),
        out_shape=jax.ShapeDtypeStruct((bd, MOBA_TOPK, h, 1), jnp.float32),
        compiler_params=pltpu.CompilerParams(dimension_semantics=("parallel",), vmem_limit_bytes=VMEM_LIMIT),
        name="sample_select",
    )(q, kmean)
    return idx[..., 0].astype(jnp.int32).transpose(0, 2, 1)


def _sample_attend_kernel(q_ref, ko_ref, vo_ref, ks_ref, vs_ref, o_ref):
    scale = D_HEAD ** -0.5
    for h in range(H_ATT):
        qh = q_ref[0, h:h + 1, :]
        s_sel = jnp.sum(ks_ref[0, h] * qh, axis=1, keepdims=True) * scale
        s_own = jnp.sum(ko_ref[0, h:h + 1, :] * qh, axis=1, keepdims=True) * scale
        m = jnp.maximum(jnp.max(s_sel, axis=0, keepdims=True), s_own)
        p_sel = jnp.exp(s_sel - m)
        p_own = jnp.exp(s_own - m)
        denom = jnp.sum(p_sel, axis=0, keepdims=True) + p_own
        acc = jnp.sum(p_sel * vs_ref[0, h], axis=0, keepdims=True) + p_own * vo_ref[0, h:h + 1, :]
        o_ref[0, h:h + 1, :] = acc / denom


def sample_attend(q, k_own, v_own, k_sel, v_sel):
    bd, h, n_keys, dh = k_sel.shape
    tok = pl.BlockSpec((1, h, dh), lambda b: (b, 0, 0))
    sel = pl.BlockSpec((1, h, n_keys, dh), lambda b: (b, 0, 0, 0))
    return pl.pallas_call(
        _sample_attend_kernel,
        grid=(bd,),
        in_specs=[tok, tok, tok, sel, sel],
        out_specs=tok,
        out_shape=jax.ShapeDtypeStruct((bd, h, dh), jnp.float32),
        compiler_params=pltpu.CompilerParams(dimension_semantics=("parallel",), vmem_limit_bytes=VMEM_LIMIT),
        name="sample_attend",
    )(q, k_own, v_own, k_sel, v_sel)


def moba_sample(q, k, v, cache_k, cache_v, page_table):
    bd, s, w = q.shape
    assert s == 1 and PAST_LEN % MOBA_BLOCK == 0 and PAST_LEN // MOBA_BLOCK >= MOBA_TOPK
    heads = lambda z: z.reshape(bd, H_ATT, D_HEAD)
    n_blocks = page_table.shape[1] // PAGES_PER_BLOCK
    kmean = jnp.mean(cache_k[page_table].reshape(bd, n_blocks, MOBA_BLOCK, H_ATT, D_HEAD), axis=2)
    sel = sample_select(heads(q), kmean)
    logical = sel[..., None] * PAGES_PER_BLOCK + jnp.arange(PAGES_PER_BLOCK)
    phys = page_table[jnp.arange(bd)[:, None, None, None], logical]
    hi = jnp.arange(H_ATT)[None, :, None, None]
    gather = lambda c: c[phys, :, hi, :].reshape(bd, H_ATT, MOBA_TOPK * MOBA_BLOCK, D_HEAD)
    out = sample_attend(heads(q), heads(k), heads(v), gather(cache_k), gather(cache_v))
    return out.reshape(bd, 1, w)


PEER_CANDS = [(a, b) for a in range(PEER_TOPK) for b in range(PEER_TOPK) if (a + 1) * (b + 1) <= PEER_TOPK]
PEER_NCAND = -(-len(PEER_CANDS) // 8) * 8


def _extract_topk(x, n_rows, k):
    rows = lax.broadcasted_iota(jnp.int32, x.shape, 0).astype(jnp.float32)
    vals = []
    for _ in range(k):
        m = jnp.max(x, axis=0, keepdims=True)
        first = jnp.min(jnp.where(x == m, rows, float(n_rows)), axis=0, keepdims=True)
        x = jnp.where(rows == first, NEG_INF, x)
        vals.append(m)
    return x, vals


def _peer_route_kernel(xn_ref, wpq_ref, keys_ref, t1_ref, s2_ref, e1_ref, e2_ref, sv_ref, comb_ref):
    q = jnp.dot(xn_ref[...], wpq_ref[...], preferred_element_type=jnp.float32)
    half = PEER_DQ // 2
    for h in range(PEER_HEADS):
        masked = []
        for p in range(2):
            qs = q[:, (2 * h + p) * half:(2 * h + p + 1) * half].astype(jnp.bfloat16)
            st = lax.dot_general(keys_ref[p], qs, (((1,), (1,)), ((), ())),
                                 preferred_element_type=jnp.float32)
            rem, vals = _extract_topk(st, PEER_NKEYS, PEER_TOPK)
            for i, v in enumerate(vals):
                sv_ref[p, i:i + 1, :] = v
            masked.append(jnp.where(rem == NEG_INF, st, NEG_INF))
        comb_ref[...] = jnp.full(comb_ref.shape, NEG_INF, jnp.float32)
        for c, (a, b) in enumerate(PEER_CANDS):
            comb_ref[c:c + 1, :] = sv_ref[0, a:a + 1, :] + sv_ref[1, b:b + 1, :]
        _, cvals = _extract_topk(comb_ref[...], PEER_NCAND, PEER_TOPK + 1)
        cmax = cvals[0]
        z = jnp.zeros_like(cmax)
        for v in cvals[:PEER_TOPK]:
            z = z + jnp.exp(v - cmax)
        cut = 0.5 * (cvals[PEER_TOPK - 1] + cvals[PEER_TOPK])
        t1_ref[h] = cut - masked[0]
        s2_ref[h] = masked[1]
        e1_ref[h] = jnp.exp(masked[0] - sv_ref[0, 0:1, :])
        e2_ref[h] = jnp.exp(masked[1] - sv_ref[1, 0:1, :]) / z


def peer_route(xn_bf16, wpq_bf16, keys_bf16, tm):
    n = xn_bf16.shape[0]
    tab = jax.ShapeDtypeStruct((PEER_HEADS, PEER_NKEYS, n), jnp.float32)
    tab_spec = pl.BlockSpec((PEER_HEADS, PEER_NKEYS, tm), lambda i: (0, 0, i))
    return pl.pallas_call(
        _peer_route_kernel,
        grid=(n // tm,),
        in_specs=[
            pl.BlockSpec((tm, D_MODEL), lambda i: (i, 0)),
            pl.BlockSpec((D_MODEL, PEER_HEADS * PEER_DQ), lambda i: (0, 0)),
            pl.BlockSpec((2, PEER_NKEYS, PEER_DQ // 2), lambda i: (0, 0, 0)),
        ],
        out_specs=[tab_spec] * 4,
        out_shape=[tab] * 4,
        scratch_shapes=[pltpu.VMEM((2, PEER_TOPK, tm), jnp.float32),
                        pltpu.VMEM((PEER_NCAND, tm), jnp.float32)],
        compiler_params=pltpu.CompilerParams(dimension_semantics=("parallel",),
                                             vmem_limit_bytes=VMEM_LIMIT),
        name="peer_route",
    )(xn_bf16, wpq_bf16, keys_bf16)


def _gelu_exact(x):
    return 0.5 * x * (1.0 + lax.erf(x * (1.0 / math.sqrt(2.0))))


def _peer_dense_kernel(xn_ref, u_ref, vt_ref, t1_ref, s2_ref, e1_ref, e2_ref, h_ref, nfin_ref, o_ref, acc_ref, p_ref):
    j = pl.program_id(1)
    n_e1 = u_ref.shape[0] // PEER_NKEYS
    tm = xn_ref.shape[0]

    @pl.when(j == 0)
    def _():
        acc_ref[...] = jnp.zeros_like(acc_ref)

    for eb in range(n_e1 // PEER_ELB):
        erows = slice(eb * PEER_ELB * PEER_NKEYS, (eb + 1) * PEER_ELB * PEER_NKEYS)
        at = lax.dot_general(u_ref[erows, :], xn_ref[...], NT_DIMS,
                             preferred_element_type=jnp.float32)
        for c in range(tm // LANES):
            cols = slice(c * LANES, (c + 1) * LANES)
            for r0 in range(0, PEER_NKEYS, PEER_SGR):
                accs = [jnp.zeros((PEER_SGR, LANES), jnp.float32) for _ in range(PEER_ELB)]
                for h in range(PEER_HEADS):
                    s2t = s2_ref[h, r0:r0 + PEER_SGR, cols]
                    e2t = e2_ref[h, r0:r0 + PEER_SGR, cols]
                    for i in range(PEER_ELB):
                        el = eb * PEER_ELB + i
                        picked = s2t >= t1_ref[h, el:el + 1, cols]
                        accs[i] = accs[i] + jnp.where(picked, e2t, 0.0) * e1_ref[h, el:el + 1, cols]
                for i in range(PEER_ELB):
                    lo = i * PEER_NKEYS + r0
                    act = _gelu_exact(at[lo:lo + PEER_SGR, cols])
                    p_ref[eb, lo:lo + PEER_SGR, cols] = (accs[i] * act).astype(jnp.bfloat16)
        acc_ref[...] += jnp.dot(vt_ref[:, erows], p_ref[eb], preferred_element_type=jnp.float32)

    @pl.when(j == pl.num_programs(1) - 1)
    def _():
        o_ref[...] = _rms_norm(h_ref[...] + acc_ref[...].T, nfin_ref[...])


def peer_dense(xn_bf16, u_bf16, vt_bf16, t1, s2, e1, e2, h, norm_final, tm, te):
    n = xn_bf16.shape[0]
    n_e1 = te // PEER_NKEYS
    return pl.pallas_call(
        _peer_dense_kernel,
        grid=(n // tm, PEER_EXPERTS // te),
        in_specs=[
            pl.BlockSpec((tm, D_MODEL), lambda i, j: (i, 0)),
            pl.BlockSpec((te, D_MODEL), lambda i, j: (j, 0)),
            pl.BlockSpec((D_MODEL, te), lambda i, j: (0, j)),
            pl.BlockSpec((PEER_HEADS, n_e1, tm), lambda i, j: (0, j, i)),
            pl.BlockSpec((PEER_HEADS, PEER_NKEYS, tm), lambda i, j: (0, 0, i)),
            pl.BlockSpec((PEER_HEADS, n_e1, tm), lambda i, j: (0, j, i)),
            pl.BlockSpec((PEER_HEADS, PEER_NKEYS, tm), lambda i, j: (0, 0, i)),
            pl.BlockSpec((tm, D_MODEL), lambda i, j: (i, 0)),
            pl.BlockSpec((1, D_MODEL), lambda i, j: (0, 0)),
        ],
        out_specs=pl.BlockSpec((tm, D_MODEL), lambda i, j: (i, 0)),
        out_shape=jax.ShapeDtypeStruct((n, D_MODEL), jnp.float32),
        scratch_shapes=[pltpu.VMEM((D_MODEL, tm), jnp.float32),
                        pltpu.VMEM((n_e1 // PEER_ELB, PEER_ELB * PEER_NKEYS, tm), jnp.bfloat16)],
        compiler_params=pltpu.CompilerParams(dimension_semantics=("parallel", "arbitrary"),
                                             vmem_limit_bytes=VMEM_LIMIT),
        name="peer_dense",
    )(xn_bf16, u_bf16, vt_bf16, t1, s2, e1, e2, h, norm_final.reshape(1, D_MODEL))


def peer_block(h, xn_bf16, wpq_bf16, keys_bf16, u_bf16, vt_bf16, norm_final):
    n = h.shape[0]
    tm = PEER_TM if n >= PEER_TM else PEER_TM_SMALL
    pad = -n % tm
    hp, xb = jnp.pad(h, ((0, pad), (0, 0))), jnp.pad(xn_bf16, ((0, pad), (0, 0)))
    t1, s2, e1, e2 = peer_route(xb, wpq_bf16, keys_bf16, PEER_TM_SMALL)
    return peer_dense(xb, u_bf16, vt_bf16, t1, s2, e1, e2, hp, norm_final, tm, PEER_TE)[:n]


def layer(x, pos, prev0, s0, attend, norm_mix, w_in_bf16, rw, w_out_bf16, norm_ffn, peer, norm_final):
    mu, w0, w_dec, a0, w_aaa, w_gate, k_k, k_a, r_k, gn_g, gn_b = rw
    b, t, d = x.shape
    n = b * t
    if t == 1:
        xr, posr = x.reshape(1, b, d), jnp.broadcast_to(pos, (b,))
    else:
        xr, posr = x, pos
    seq = lambda z: z.reshape(b, t, z.shape[-1])
    pr, q, k, v = map(seq, project(xr, norm_mix, w_in_bf16, posr, min(xr.shape[1], PROJ_TT)))
    r, w, k2, kk, bb, vv, g, bonus = rwkv_prep(pr, prev0, mu, w0, w_dec, a0, w_aaa, w_gate, k_k, k_a, r_k, min(t, 256))
    y, s_new = rwkv_scan(r, w, k2, kk, bb, vv, s0, RWKV_NB, min(t, 128))
    a_out = attend(q, k, v)
    flat = lambda z: z.reshape(n, z.shape[-1])
    h, xn = merge(flat(y), flat(bonus), flat(g), gn_g, gn_b, flat(a_out), flat(x), w_out_bf16, norm_ffn, min(n, 512))
    out = peer_block(h, xn, *peer, norm_final)
    heads = lambda z: z.reshape(b, t, H_ATT, D_HEAD)
    return out.reshape(b, t, d), heads(k), heads(v), s_new, pr[:, -1]


def kernel(x_prompt, x_sample, cache_k, cache_v, page_table, state_wkv, state_shift, norm_mix, w_in, mu_shift, w0, w_decay_up, a0, w_aaa_up, w_gate_up, k_k, k_a, r_k, gn_gain, gn_bias, w_out, norm_ffn, w_pq, peer_sub_keys, expert_u, expert_v, norm_final):
    l = 0
    bf16 = jnp.bfloat16
    rw = (mu_shift[l], w0[l], w_decay_up[l], a0[l], w_aaa_up[l], w_gate_up[l], k_k[l], k_a[l], r_k[l], gn_gain[l], gn_bias[l])
    peer = (w_pq[l].astype(bf16), peer_sub_keys[l].astype(bf16), expert_u[l].astype(bf16), expert_v[l].T.astype(bf16))
    shared = (norm_mix[l], w_in[l].astype(bf16), rw, w_out[l].astype(bf16), norm_ffn[l], peer, norm_final)
    bp, tp, _ = x_prompt.shape
    bs, ts, _ = x_sample.shape
    y_p, k_p, v_p, s_p, sh_p = layer(
        x_prompt, jnp.arange(tp, dtype=jnp.int32), jnp.zeros((bp, W_RWKV_IN), jnp.float32),
        jnp.zeros((bp, H_RWKV, D_HEAD, D_HEAD), jnp.float32), moba_prompt, *shared)
    attend_cache = lambda q, k, v: moba_sample(q, k, v, cache_k[l], cache_v[l], page_table)
    y_s, k_s, v_s, s_s, sh_s = layer(
        x_sample, PAST_LEN + jnp.arange(ts, dtype=jnp.int32), state_shift[l], state_wkv[l], attend_cache, *shared)
    return (y_p, y_s, k_p[None], v_p[None], s_p[None], sh_p[None], k_s[None], v_s[None], s_s[None], sh_s[None])
```

```python
import functools
import math

import jax
import jax.numpy as jnp
from jax import lax
from jax.experimental import pallas as pl
from jax.experimental.pallas import tpu as pltpu

D_MODEL = 1024
DEPTH = 1
PAST_LEN = 16384
PAGE_SIZE = 128
D_HEAD = 64
H_RWKV = 8
H_ATT = 8
W_RWKV = H_RWKV * D_HEAD
W_ATT = H_ATT * D_HEAD
D_DECAY_LORA = 64
D_AAA_LORA = 64
D_GATE_LORA = 128
W_RWKV_IN = 3 * W_RWKV + D_DECAY_LORA + D_AAA_LORA + D_GATE_LORA
W_IN = W_RWKV_IN + 3 * W_ATT
GN_EPS = 64e-5
NORM_EPS = 1e-6
MOBA_BLOCK = 256
MOBA_TOPK = 3
Q_CHUNK = 64
ROT_DIM = D_HEAD // 4
ROPE_THETA = 500000.0
PEER_HEADS = 8
PEER_NKEYS = 128
PEER_TOPK = 16
PEER_DQ = 256
PEER_EXPERTS = PEER_NKEYS * PEER_NKEYS
PEER_TM = 512
PEER_TM_SMALL = 256
PEER_TE = 1024
PEER_ELB = 4
PEER_SGR = 32
NEG_INF = float("-inf")
VMEM_LIMIT = 56 * 1024 * 1024
LANES = 128
SUBLANES = 8
PROJ_TT = 512
RWKV_NB = 4
NT_DIMS = (((1,), (1,)), ((), ()))


def _rms_norm(x, g):
    return x * lax.rsqrt(jnp.mean(x * x, axis=-1, keepdims=True) + NORM_EPS) * g


def rope_tables(pos):
    half = ROT_DIM // 2
    inv = 1.0 / (ROPE_THETA ** (jnp.arange(half, dtype=jnp.float32) * 2.0 / ROT_DIM))
    ang = pos.astype(jnp.float32)[:, None] * inv[None, :]
    lane = jnp.arange(LANES) % D_HEAD
    idx = lane % half
    cos = jnp.where(lane[None, :] < ROT_DIM, jnp.cos(ang)[:, idx], 1.0)
    sin = jnp.sin(ang)[:, idx]
    sin_lo = jnp.where(lane[None, :] < half, -sin, 0.0)
    sin_hi = jnp.where((lane[None, :] >= half) & (lane[None, :] < ROT_DIM), sin, 0.0)
    return cos, sin_lo, sin_hi


def _project_kernel(x_ref, g_ref, w_ref, cos_ref, slo_ref, shi_ref, pr_o, q_o, k_o, v_o):
    xn = _rms_norm(x_ref[0], g_ref[...])
    p = jnp.dot(xn.astype(jnp.bfloat16), w_ref[...], preferred_element_type=jnp.float32)
    pr_o[0] = p[:, :W_RWKV_IN]
    reps = W_ATT // LANES
    cos = jnp.concatenate([cos_ref[...]] * reps, axis=1)
    slo = jnp.concatenate([slo_ref[...]] * reps, axis=1)
    shi = jnp.concatenate([shi_ref[...]] * reps, axis=1)
    half = ROT_DIM // 2

    def rope(z):
        ahead = pltpu.roll(z, W_ATT - half, 1)
        behind = pltpu.roll(z, half, 1)
        return z * cos + ahead * slo + behind * shi

    q_o[0] = rope(p[:, W_RWKV_IN:W_RWKV_IN + W_ATT])
    k_o[0] = rope(p[:, W_RWKV_IN + W_ATT:W_RWKV_IN + 2 * W_ATT])
    v_o[0] = p[:, W_RWKV_IN + 2 * W_ATT:]


def project(x, norm_g, w_in_bf16, pos, tt):
    b, t, d = x.shape
    cos, slo, shi = rope_tables(pos)
    f32 = jnp.float32
    tab_spec = pl.BlockSpec((tt, LANES), lambda bi, ti: (ti, 0))
    out_spec = lambda w: pl.BlockSpec((1, tt, w), lambda bi, ti: (bi, ti, 0))
    return pl.pallas_call(
        _project_kernel,
        grid=(b, t // tt),
        in_specs=[
            pl.BlockSpec((1, tt, d), lambda bi, ti: (bi, ti, 0)),
            pl.BlockSpec((1, d), lambda bi, ti: (0, 0)),
            pl.BlockSpec((d, W_IN), lambda bi, ti: (0, 0)),
            tab_spec, tab_spec, tab_spec,
        ],
        out_specs=[out_spec(W_RWKV_IN), out_spec(W_ATT), out_spec(W_ATT), out_spec(W_ATT)],
        out_shape=[jax.ShapeDtypeStruct((b, t, W_RWKV_IN), f32)] + [jax.ShapeDtypeStruct((b, t, W_ATT), f32)] * 3,
        compiler_params=pltpu.CompilerParams(dimension_semantics=("parallel", "parallel"),
                                             vmem_limit_bytes=VMEM_LIMIT),
        name="project",
    )(x, norm_g.reshape(1, d), w_in_bf16, cos, slo, shi)


N_PAIR = W_RWKV // LANES
HI = lax.Precision.HIGHEST


def _sigmoid(x):
    return 1.0 / (1.0 + jnp.exp(-x))


def _softplus(x):
    return jnp.maximum(x, 0.0) + jnp.log(1.0 + jnp.exp(-jnp.abs(x)))


def _rwkv_prep_kernel(p_ref, prev0_ref, mu_ref, w0_ref, wdec_ref, a0_ref, waaa_ref, wgate_ref, kk_w_ref, ka_ref,
                      rk_ref, seg_ref, r_o, w_o, k_o, kk_o, b_o, v_o, g_o, bonus_o, carry_ref):
    ti = pl.program_id(1)

    @pl.when(ti == 0)
    def _():
        carry_ref[...] = prev0_ref[0]

    p = p_ref[0]
    row = lax.broadcasted_iota(jnp.int32, p.shape, 0)
    if p.shape[0] == 1:
        prev = carry_ref[...]
    else:
        prev = jnp.where(row == 0, carry_ref[...], pltpu.roll(p, 1, 0))
    carry_ref[...] = p[p.shape[0] - 1:, :]
    ps = p + (prev - p) * mu_ref[...]
    r = ps[:, 0:W_RWKV]
    k = ps[:, W_RWKV:2 * W_RWKV]
    v = ps[:, 2 * W_RWKV:3 * W_RWKV]
    lora = ps[:, 3 * W_RWKV:3 * W_RWKV + LANES]
    gd = ps[:, 3 * W_RWKV + LANES:]
    dec = jnp.dot(jnp.tanh(lora).astype(jnp.bfloat16), wdec_ref[...], preferred_element_type=jnp.float32)
    logw = -jnp.exp(-_softplus(-(w0_ref[...] + dec)) - 0.5)
    a = _sigmoid(a0_ref[...] + jnp.dot(lora.astype(jnp.bfloat16), waaa_ref[...], preferred_element_type=jnp.float32))
    g = jnp.dot(_sigmoid(gd).astype(jnp.bfloat16), wgate_ref[...], preferred_element_type=jnp.float32)
    kk = k * kk_w_ref[...]
    sumsq = jnp.dot(kk * kk, seg_ref[...], precision=HI, preferred_element_type=jnp.float32)
    kk = kk / jnp.maximum(jnp.sqrt(sumsq), 1e-12)
    k2 = k * (1.0 + (a - 1.0) * ka_ref[...])
    rkk = jnp.dot(r * k2 * rk_ref[...], seg_ref[...], precision=HI, preferred_element_type=jnp.float32)
    r_o[0] = r
    w_o[0] = jnp.exp(logw)
    k_o[0] = k2
    kk_o[0] = kk
    b_o[0] = kk * a
    v_o[0] = v
    g_o[0] = g
    bonus_o[0] = rkk * v


def rwkv_prep(p, prev0, mu, w0, w_dec, a0, w_aaa, w_gate, k_k, k_a, r_k, tt):
    b, t, _ = p.shape
    f32 = jnp.float32
    wdec_pad = jnp.concatenate([w_dec, jnp.zeros_like(w_aaa)], axis=0).astype(jnp.bfloat16)
    waaa_pad = jnp.concatenate([jnp.zeros_like(w_dec), w_aaa], axis=0).astype(jnp.bfloat16)
    head = jnp.arange(W_RWKV) // D_HEAD
    seg = (head[:, None] == head[None, :]).astype(f32)
    row = lambda z: z.reshape(1, -1).astype(f32)
    vec_spec = lambda n: pl.BlockSpec((1, n), lambda bi, ti: (0, 0))
    mat_spec = lambda m, n: pl.BlockSpec((m, n), lambda bi, ti: (0, 0))
    out_spec = pl.BlockSpec((1, tt, W_RWKV), lambda bi, ti: (bi, ti, 0))
    out = jax.ShapeDtypeStruct((b, t, W_RWKV), f32)
    return pl.pallas_call(
        _rwkv_prep_kernel,
        grid=(b, t // tt),
        in_specs=[
            pl.BlockSpec((1, tt, W_RWKV_IN), lambda bi, ti: (bi, ti, 0)),
            pl.BlockSpec((1, 1, W_RWKV_IN), lambda bi, ti: (bi, 0, 0)),
            vec_spec(W_RWKV_IN), vec_spec(W_RWKV), mat_spec(LANES, W_RWKV), vec_spec(W_RWKV),
            mat_spec(LANES, W_RWKV), mat_spec(D_GATE_LORA, W_RWKV), vec_spec(W_RWKV), vec_spec(W_RWKV),
            vec_spec(W_RWKV), mat_spec(W_RWKV, W_RWKV),
        ],
        out_specs=[out_spec] * 8,
        out_shape=[out] * 8,
        scratch_shapes=[pltpu.VMEM((1, W_RWKV_IN), f32)],
        compiler_params=pltpu.CompilerParams(dimension_semantics=("parallel", "arbitrary"),
                                             vmem_limit_bytes=VMEM_LIMIT),
        name="rwkv_prep",
    )(p, prev0.reshape(b, 1, W_RWKV_IN), row(mu), row(w0), wdec_pad, row(a0), waaa_pad, w_gate.astype(jnp.bfloat16),
      row(k_k), row(k_a), row(r_k), seg)


def _rwkv_scan_kernel(r_ref, w_ref, k_ref, kk_ref, b_ref, v_ref, s0_ref, y_ref, sT_ref, s_ref):
    ci = pl.program_id(1)
    nb, tc = r_ref.shape[0], r_ref.shape[1]
    tiles = [(bi, p) for bi in range(nb) for p in range(N_PAIR)]

    @pl.when(ci == 0)
    def _():
        for i, (bi, p) in enumerate(tiles):
            s_ref[i] = s0_ref[bi, p]

    lane = lax.broadcasted_iota(jnp.int32, (D_HEAD, LANES), 1)
    sub = lax.broadcasted_iota(jnp.int32, (D_HEAD, LANES), 0)
    first = lane < D_HEAD
    eye2 = (sub == (lane % D_HEAD))
    sub128 = lax.broadcasted_iota(jnp.int32, (LANES, LANES), 0)
    lane128 = lax.broadcasted_iota(jnp.int32, (LANES, LANES), 1)
    eye128 = sub128 == lane128
    seg = jnp.where((sub128 // D_HEAD) == (lane128 // D_HEAD), 1.0, 0.0).astype(jnp.bfloat16)

    def head_sums(xs):
        parts = []
        for x in xs:
            hi = x.astype(jnp.bfloat16)
            parts += [hi, (x - hi.astype(jnp.float32)).astype(jnp.bfloat16)]
        both = jnp.dot(jnp.concatenate(parts, axis=0), seg, preferred_element_type=jnp.float32)
        return [both[i * LANES:i * LANES + D_HEAD] + both[i * LANES + D_HEAD:(i + 1) * LANES] for i in range(len(xs))]

    grp = min(tc, SUBLANES)

    def steps(gi, carry):
        rows = pl.ds(pl.multiple_of(gi * grp, grp), grp)
        lanes = lambda p: slice(p * LANES, (p + 1) * LANES)
        load = lambda ref: [ref[bi, rows, lanes(p)] for bi, p in tiles]
        r_g, w_g, k_g, kk_g, b_g, v_g = load(r_ref), load(w_ref), load(k_ref), load(kk_ref), load(b_ref), load(v_ref)
        states = [s_ref[i] for i in range(len(tiles))]
        yrows = [[] for _ in tiles]
        for j in range(grp):
            row = slice(j, j + 1)
            sks = head_sums([s * kk[row] for s, kk in zip(states, kk_g)])
            for i in range(len(tiles)):
                vcol = jnp.sum(jnp.where(eye128, v_g[i][row], 0.0), axis=1, keepdims=True)
                vmat = jnp.where(first, vcol[0:D_HEAD], vcol[D_HEAD:LANES])
                states[i] = states[i] * w_g[i][row] - sks[i] * b_g[i][row] + vmat * k_g[i][row]
            ys = head_sums([s * r[row] for s, r in zip(states, r_g)])
            for i in range(len(tiles)):
                yrows[i].append(jnp.sum(jnp.where(eye2, ys[i], 0.0), axis=0, keepdims=True))
        for i, (bi, p) in enumerate(tiles):
            y_ref[bi, rows, lanes(p)] = yrows[i][0] if grp == 1 else jnp.concatenate(yrows[i], axis=0)
            s_ref[i] = states[i]
        return carry

    lax.fori_loop(0, tc // grp, steps, 0)

    @pl.when(ci == pl.num_programs(1) - 1)
    def _():
        for i, (bi, p) in enumerate(tiles):
            sT_ref[bi, p] = s_ref[i]


def rwkv_scan(r, w, k, kk, bb, v, s0, nb, tc):
    b, t, _ = r.shape
    f32 = jnp.float32
    pair = lambda s: s.reshape(b, N_PAIR, 2, D_HEAD, D_HEAD).transpose(0, 1, 3, 2, 4).reshape(b, N_PAIR, D_HEAD, LANES)
    unpair = lambda s: s.reshape(b, N_PAIR, D_HEAD, 2, D_HEAD).transpose(0, 1, 3, 2, 4).reshape(b, H_RWKV, D_HEAD, D_HEAD)
    seq_spec = pl.BlockSpec((nb, tc, W_RWKV), lambda bi, ci: (bi, ci, 0))
    st_spec = pl.BlockSpec((nb, N_PAIR, D_HEAD, LANES), lambda bi, ci: (bi, 0, 0, 0))
    y, s_fin = pl.pallas_call(
        _rwkv_scan_kernel,
        grid=(b // nb, t // tc),
        in_specs=[seq_spec] * 6 + [st_spec],
        out_specs=[seq_spec, st_spec],
        out_shape=[jax.ShapeDtypeStruct((b, t, W_RWKV), f32), jax.ShapeDtypeStruct((b, N_PAIR, D_HEAD, LANES), f32)],
        scratch_shapes=[pltpu.VMEM((nb * N_PAIR, D_HEAD, LANES), f32)],
        compiler_params=pltpu.CompilerParams(dimension_semantics=("parallel", "arbitrary"),
                                             vmem_limit_bytes=VMEM_LIMIT),
        name="rwkv_scan",
    )(r, w, k, kk, bb, v, pair(s0.astype(f32)))
    return y, unpair(s_fin)


def _merge_kernel(y_ref, bonus_ref, g_ref, gng_ref, gnb_ref, seg_ref, a_ref, x_ref, wr_ref, wa_ref, nf_ref, h_o, xn_o):
    y = y_ref[...]
    inv = 1.0 / D_HEAD
    mean = jnp.dot(y, seg_ref[...], precision=HI, preferred_element_type=jnp.float32) * inv
    dlt = y - mean
    var = jnp.dot(dlt * dlt, seg_ref[...], precision=HI, preferred_element_type=jnp.float32) * inv
    yn = dlt * lax.rsqrt(var + GN_EPS) * gng_ref[...] + gnb_ref[...]
    r_out = (yn + bonus_ref[...]) * g_ref[...]
    h = (x_ref[...] + jnp.dot(r_out.astype(jnp.bfloat16), wr_ref[...], preferred_element_type=jnp.float32)
         + jnp.dot(a_ref[...].astype(jnp.bfloat16), wa_ref[...], preferred_element_type=jnp.float32))
    h_o[...] = h
    xn_o[...] = _rms_norm(h, nf_ref[...]).astype(jnp.bfloat16)


def merge(y, bonus, g, gn_g, gn_b, a_out, x, w_out_bf16, norm_ffn, tt):
    n, d = x.shape
    head = jnp.arange(W_RWKV) // D_HEAD
    seg = (head[:, None] == head[None, :]).astype(jnp.float32)
    row = lambda w: pl.BlockSpec((tt, w), lambda i: (i, 0))
    vec = lambda w: pl.BlockSpec((1, w), lambda i: (0, 0))
    mat = lambda r, c: pl.BlockSpec((r, c), lambda i: (0, 0))
    return pl.pallas_call(
        _merge_kernel,
        grid=(n // tt,),
        in_specs=[row(W_RWKV), row(W_RWKV), row(W_RWKV), vec(W_RWKV), vec(W_RWKV), mat(W_RWKV, W_RWKV),
                  row(W_ATT), row(d), mat(W_RWKV, d), mat(W_ATT, d), vec(d)],
        out_specs=[row(d), row(d)],
        out_shape=[jax.ShapeDtypeStruct((n, d), jnp.float32), jax.ShapeDtypeStruct((n, d), jnp.bfloat16)],
        compiler_params=pltpu.CompilerParams(dimension_semantics=("parallel",), vmem_limit_bytes=VMEM_LIMIT),
        name="merge",
    )(y, bonus, g, gn_g.reshape(1, -1), gn_b.reshape(1, -1), seg, a_out, x, w_out_bf16[:W_RWKV], w_out_bf16[W_RWKV:],
      norm_ffn.reshape(1, d))


def _moba_block_kernel(q_ref, k_ref, v_ref, o_ref, *, qi):
    scale = D_HEAD ** -0.5
    nb = qi + 1
    k_all = k_ref[0]
    kb = k_all.astype(jnp.bfloat16)
    vt = v_ref[0].T.astype(jnp.bfloat16)
    q2 = q_ref[0]
    lane = lax.broadcasted_iota(jnp.int32, q2.shape, 1)
    key_row = lax.broadcasted_iota(jnp.int32, (MOBA_BLOCK, MOBA_BLOCK), 0)
    qry_col = lax.broadcasted_iota(jnp.int32, (MOBA_BLOCK, MOBA_BLOCK), 1)
    row = lax.broadcasted_iota(jnp.int32, (LANES, MOBA_BLOCK), 0)
    blk = lambda z, j: z[j * MOBA_BLOCK:(j + 1) * MOBA_BLOCK]
    outs = []
    for s in range(2):
        qm = jnp.where((lane // D_HEAD) == s, q2, 0.0)
        qmb = qm.astype(jnp.bfloat16)
        picked = [None] * qi
        if qi > MOBA_TOPK:
            kmean = jnp.concatenate([jnp.mean(blk(k_all, j), axis=0, keepdims=True) for j in range(qi)], axis=0)
            gate = lax.dot_general(kmean, qm, NT_DIMS, precision=HI, preferred_element_type=jnp.float32)
            for j in range(qi):
                ahead = jnp.zeros((1, MOBA_BLOCK), jnp.float32)
                for jp in range(qi):
                    if jp != j:
                        beats = (gate[jp:jp + 1] >= gate[j:j + 1]) if jp < j else (gate[jp:jp + 1] > gate[j:j + 1])
                        ahead = ahead + jnp.where(beats, 1.0, 0.0)
                picked[j] = ahead < float(MOBA_TOPK)
        scores = []
        for j in range(nb):
            st = lax.dot_general(blk(kb, j), qmb, NT_DIMS, preferred_element_type=jnp.float32) * scale
            if j == qi:
                st = jnp.where(key_row <= qry_col, st, NEG_INF)
            elif picked[j] is not None:
                st = jnp.where(picked[j], st, NEG_INF)
            scores.append(st)
        m = functools.reduce(jnp.maximum, [jnp.max(st, axis=0, keepdims=True) for st in scores])
        ps = [jnp.exp(st - m) for st in scores]
        denom = functools.reduce(jnp.add, [jnp.sum(p, axis=0, keepdims=True) for p in ps])
        acc = functools.reduce(jnp.add, [
            jnp.dot(vt[:, j * MOBA_BLOCK:(j + 1) * MOBA_BLOCK], ps[j].astype(jnp.bfloat16),
                    preferred_element_type=jnp.float32) for j in range(nb)])
        outs.append(acc / denom)
    o_ref[0] = jnp.where((row // D_HEAD) == 0, outs[0], outs[1]).T


def moba_prompt(q, k, v):
    b, t, w = q.shape
    outs = []
    for qi in range(t // MOBA_BLOCK):
        n_keys = (qi + 1) * MOBA_BLOCK
        outs.append(pl.pallas_call(
            functools.partial(_moba_block_kernel, qi=qi),
            grid=(b, w // LANES),
            in_specs=[
                pl.BlockSpec((1, MOBA_BLOCK, LANES), lambda bi, hp, qi=qi: (bi, qi, hp)),
                pl.BlockSpec((1, n_keys, LANES), lambda bi, hp: (bi, 0, hp)),
                pl.BlockSpec((1, n_keys, LANES), lambda bi, hp: (bi, 0, hp)),
            ],
            out_specs=pl.BlockSpec((1, MOBA_BLOCK, LANES), lambda bi, hp: (bi, 0, hp)),
            out_shape=jax.ShapeDtypeStruct((b, MOBA_BLOCK, w), jnp.float32),
            compiler_params=pltpu.CompilerParams(dimension_semantics=("parallel", "parallel"),
                                                 vmem_limit_bytes=VMEM_LIMIT),
            name=f"moba_prompt_q{qi}",
        )(q, k, v))
    return jnp.concatenate(outs, axis=1)


PAGES_PER_BLOCK = MOBA_BLOCK // PAGE_SIZE


def _sample_select_kernel(q_ref, km_ref, idx_ref):
    gate = jnp.sum(km_ref[0] * q_ref[...], axis=-1, keepdims=True)
    nb = gate.shape[0]
    blocks = lax.broadcasted_iota(jnp.int32, gate.shape, 0).astype(jnp.float32)
    for i in range(MOBA_TOPK):
        m = jnp.max(gate, axis=0, keepdims=True)
        first = jnp.min(jnp.where(gate == m, blocks, float(nb)), axis=0, keepdims=True)
        gate = jnp.where(blocks == first, NEG_INF, gate)
        idx_ref[0, i] = first[0]


def sample_select(q, kmean):
    bd, nb, h, dh = kmean.shape
    idx = pl.pallas_call(
        _sample_select_kernel,
        grid=(bd,),
        in_specs=[pl.BlockSpec((1, h, dh), lambda b: (b, 0, 0)), pl.BlockSpec((1, nb, h, dh), lambda b: (b, 0, 0, 0))],
        out_specs=pl.BlockSpec((1, MOBA_TOPK, h, 1), lambda b: (b, 0, 0, 0)),
        out_shape=jax.ShapeDtypeStruct((bd, MOBA_TOPK, h, 1), jnp.float32),
        compiler_params=pltpu.CompilerParams(dimension_semantics=("parallel",), vmem_limit_bytes=VMEM_LIMIT),
        name="sample_select",
    )(q, kmean)
    return idx[..., 0].astype(jnp.int32).transpose(0, 2, 1)


def _sample_attend_kernel(q_ref, ko_ref, vo_ref, ks_ref, vs_ref, o_ref):
    scale = D_HEAD ** -0.5
    for h in range(H_ATT):
        qh = q_ref[0, h:h + 1, :]
        s_sel = jnp.sum(ks_ref[0, h] * qh, axis=1, keepdims=True) * scale
        s_own = jnp.sum(ko_ref[0, h:h + 1, :] * qh, axis=1, keepdims=True) * scale
        m = jnp.maximum(jnp.max(s_sel, axis=0, keepdims=True), s_own)
        p_sel = jnp.exp(s_sel - m)
        p_own = jnp.exp(s_own - m)
        denom = jnp.sum(p_sel, axis=0, keepdims=True) + p_own
        acc = jnp.sum(p_sel * vs_ref[0, h], axis=0, keepdims=True) + p_own * vo_ref[0, h:h + 1, :]
        o_ref[0, h:h + 1, :] = acc / denom


def sample_attend(q, k_own, v_own, k_sel, v_sel):
    bd, h, n_keys, dh = k_sel.shape
    tok = pl.BlockSpec((1, h, dh), lambda b: (b, 0, 0))
    sel = pl.BlockSpec((1, h, n_keys, dh), lambda b: (b, 0, 0, 0))
    return pl.pallas_call(
        _sample_attend_kernel,
        grid=(bd,),
        in_specs=[tok, tok, tok, sel, sel],
        out_specs=tok,
        out_shape=jax.ShapeDtypeStruct((bd, h, dh), jnp.float32),
        compiler_params=pltpu.CompilerParams(dimension_semantics=("parallel",), vmem_limit_bytes=VMEM_LIMIT),
        name="sample_attend",
    )(q, k_own, v_own, k_sel, v_sel)


def moba_sample(q, k, v, cache_k, cache_v, page_table):
    bd, s, w = q.shape
    assert s == 1 and PAST_LEN % MOBA_BLOCK == 0 and PAST_LEN // MOBA_BLOCK >= MOBA_TOPK
    heads = lambda z: z.reshape(bd, H_ATT, D_HEAD)
    n_blocks = page_table.shape[1] // PAGES_PER_BLOCK
    kmean = jnp.mean(cache_k[page_table].reshape(bd, n_blocks, MOBA_BLOCK, H_ATT, D_HEAD), axis=2)
    sel = sample_select(heads(q), kmean)
    logical = sel[..., None] * PAGES_PER_BLOCK + jnp.arange(PAGES_PER_BLOCK)
    phys = page_table[jnp.arange(bd)[:, None, None, None], logical]
    hi = jnp.arange(H_ATT)[None, :, None, None]
    gather = lambda c: c[phys, :, hi, :].reshape(bd, H_ATT, MOBA_TOPK * MOBA_BLOCK, D_HEAD)
    out = sample_attend(heads(q), heads(k), heads(v), gather(cache_k), gather(cache_v))
    return out.reshape(bd, 1, w)


PEER_CANDS = [(a, b) for a in range(PEER_TOPK) for b in range(PEER_TOPK) if (a + 1) * (b + 1) <= PEER_TOPK]
PEER_NCAND = -(-len(PEER_CANDS) // 8) * 8


def _extract_topk(x, n_rows, k, exact):
    rows = lax.broadcasted_iota(jnp.int32, x.shape, 0).astype(jnp.float32)
    vals = []
    for _ in range(k):
        m = jnp.max(x, axis=0, keepdims=True)
        if exact:
            first = jnp.min(jnp.where(x == m, rows, float(n_rows)), axis=0, keepdims=True)
            x = jnp.where(rows == first, NEG_INF, x)
        else:
            x = jnp.where(x == m, NEG_INF, x)
        vals.append(m)
    return x, vals


def _peer_route_tables(q_ref, keys_ref, t1_ref, s2_ref, e1_ref, e2_ref, sv_ref, comb_ref, exact):
    half = PEER_DQ // 2
    tm = q_ref.shape[0]
    n_pad = PEER_NCAND - len(PEER_CANDS)
    ties = jnp.zeros((1, tm), jnp.float32)
    removed = lambda rem: jnp.sum(jnp.where(rem == NEG_INF, 1.0, 0.0), axis=0, keepdims=True)
    for h in range(PEER_HEADS):
        masked = []
        for p in range(2):
            qs = q_ref[:, (2 * h + p) * half:(2 * h + p + 1) * half].astype(jnp.bfloat16)
            st = lax.dot_general(keys_ref[p], qs, NT_DIMS, preferred_element_type=jnp.float32)
            rem, vals = _extract_topk(st, PEER_NKEYS, PEER_TOPK, exact)
            if not exact:
                ties = ties + (removed(rem) - float(PEER_TOPK))
            for i, v in enumerate(vals):
                sv_ref[p, i:i + 1, :] = v
            masked.append(jnp.where(rem == NEG_INF, st, NEG_INF))
        comb_ref[...] = jnp.full(comb_ref.shape, NEG_INF, jnp.float32)
        for c, (a, b) in enumerate(PEER_CANDS):
            comb_ref[c:c + 1, :] = sv_ref[0, a:a + 1, :] + sv_ref[1, b:b + 1, :]
        rem, cvals = _extract_topk(comb_ref[...], PEER_NCAND, PEER_TOPK + 1, exact)
        if not exact:
            ties = ties + (removed(rem) - float(PEER_TOPK + 1 + n_pad))
        cmax = cvals[0]
        z = jnp.zeros_like(cmax)
        for v in cvals[:PEER_TOPK]:
            z = z + jnp.exp(v - cmax)
        cut = 0.5 * (cvals[PEER_TOPK - 1] + cvals[PEER_TOPK])
        t1_ref[h] = cut - masked[0]
        s2_ref[h] = masked[1]
        e1_ref[h] = jnp.exp(masked[0] - sv_ref[0, 0:1, :])
        e2_ref[h] = jnp.exp(masked[1] - sv_ref[1, 0:1, :]) / z
    return ties


def _peer_route_kernel(xn_ref, wpq_ref, keys_ref, t1_ref, s2_ref, e1_ref, e2_ref, q_ref, sv_ref, comb_ref):
    q_ref[...] = jnp.dot(xn_ref[...], wpq_ref[...], preferred_element_type=jnp.float32)
    tables = functools.partial(_peer_route_tables, q_ref, keys_ref, t1_ref, s2_ref, e1_ref, e2_ref, sv_ref, comb_ref)
    ties = tables(exact=False)

    @pl.when(jnp.max(ties) > 0.0)
    def _():
        tables(exact=True)


def peer_route(xn_bf16, wpq_bf16, keys_bf16, tm):
    n = xn_bf16.shape[0]
    tab = jax.ShapeDtypeStruct((PEER_HEADS, PEER_NKEYS, n), jnp.float32)
    tab_spec = pl.BlockSpec((PEER_HEADS, PEER_NKEYS, tm), lambda i: (0, 0, i))
    return pl.pallas_call(
        _peer_route_kernel,
        grid=(n // tm,),
        in_specs=[
            pl.BlockSpec((tm, D_MODEL), lambda i: (i, 0)),
            pl.BlockSpec((D_MODEL, PEER_HEADS * PEER_DQ), lambda i: (0, 0)),
            pl.BlockSpec((2, PEER_NKEYS, PEER_DQ // 2), lambda i: (0, 0, 0)),
        ],
        out_specs=[tab_spec] * 4,
        out_shape=[tab] * 4,
        scratch_shapes=[pltpu.VMEM((tm, PEER_HEADS * PEER_DQ), jnp.float32),
                        pltpu.VMEM((2, PEER_TOPK, tm), jnp.float32),
                        pltpu.VMEM((PEER_NCAND, tm), jnp.float32)],
        compiler_params=pltpu.CompilerParams(dimension_semantics=("parallel",),
                                             vmem_limit_bytes=VMEM_LIMIT),
        name="peer_route",
    )(xn_bf16, wpq_bf16, keys_bf16)


def _gelu_exact(x):
    return 0.5 * x * (1.0 + lax.erf(x * (1.0 / math.sqrt(2.0))))


def _peer_dense_kernel(xn_ref, u_ref, vt_ref, t1_ref, s2_ref, e1_ref, e2_ref, h_ref, nfin_ref, o_ref, acc_ref, p_ref):
    j = pl.program_id(1)
    n_e1 = u_ref.shape[0] // PEER_NKEYS
    tm = xn_ref.shape[0]

    @pl.when(j == 0)
    def _():
        acc_ref[...] = jnp.zeros_like(acc_ref)

    for eb in range(n_e1 // PEER_ELB):
        erows = slice(eb * PEER_ELB * PEER_NKEYS, (eb + 1) * PEER_ELB * PEER_NKEYS)
        at = lax.dot_general(u_ref[erows, :], xn_ref[...], NT_DIMS,
                             preferred_element_type=jnp.float32)
        for c in range(tm // LANES):
            cols = slice(c * LANES, (c + 1) * LANES)
            for r0 in range(0, PEER_NKEYS, PEER_SGR):
                accs = [jnp.zeros((PEER_SGR, LANES), jnp.float32) for _ in range(PEER_ELB)]
                for h in range(PEER_HEADS):
                    s2t = s2_ref[h, r0:r0 + PEER_SGR, cols]
                    e2t = e2_ref[h, r0:r0 + PEER_SGR, cols]
                    for i in range(PEER_ELB):
                        el = eb * PEER_ELB + i
                        picked = s2t >= t1_ref[h, el:el + 1, cols]
                        accs[i] = accs[i] + jnp.where(picked, e2t, 0.0) * e1_ref[h, el:el + 1, cols]
                for i in range(PEER_ELB):
                    lo = i * PEER_NKEYS + r0
                    act = _gelu_exact(at[lo:lo + PEER_SGR, cols])
                    p_ref[eb, lo:lo + PEER_SGR, cols] = (accs[i] * act).astype(jnp.bfloat16)
        acc_ref[...] += jnp.dot(vt_ref[:, erows], p_ref[eb], preferred_element_type=jnp.float32)

    @pl.when(j == pl.num_programs(1) - 1)
    def _():
        o_ref[...] = _rms_norm(h_ref[...] + acc_ref[...].T, nfin_ref[...])


def peer_dense(xn_bf16, u_bf16, vt_bf16, t1, s2, e1, e2, h, norm_final, tm, te):
    n = xn_bf16.shape[0]
    n_e1 = te // PEER_NKEYS
    return pl.pallas_call(
        _peer_dense_kernel,
        grid=(n // tm, PEER_EXPERTS // te),
        in_specs=[
            pl.BlockSpec((tm, D_MODEL), lambda i, j: (i, 0)),
            pl.BlockSpec((te, D_MODEL), lambda i, j: (j, 0)),
            pl.BlockSpec((D_MODEL, te), lambda i, j: (0, j)),
            pl.BlockSpec((PEER_HEADS, n_e1, tm), lambda i, j: (0, j, i)),
            pl.BlockSpec((PEER_HEADS, PEER_NKEYS, tm), lambda i, j: (0, 0, i)),
            pl.BlockSpec((PEER_HEADS, n_e1, tm), lambda i, j: (0, j, i)),
            pl.BlockSpec((PEER_HEADS, PEER_NKEYS, tm), lambda i, j: (0, 0, i)),
            pl.BlockSpec((tm, D_MODEL), lambda i, j: (i, 0)),
            pl.BlockSpec((1, D_MODEL), lambda i, j: (0, 0)),
        ],
        out_specs=pl.BlockSpec((tm, D_MODEL), lambda i, j: (i, 0)),
        out_shape=jax.ShapeDtypeStruct((n, D_MODEL), jnp.float32),
        scratch_shapes=[pltpu.VMEM((D_MODEL, tm), jnp.float32),
                        pltpu.VMEM((n_e1 // PEER_ELB, PEER_ELB * PEER_NKEYS, tm), jnp.bfloat16)],
        compiler_params=pltpu.CompilerParams(dimension_semantics=("parallel", "arbitrary"),
                                             vmem_limit_bytes=VMEM_LIMIT),
        name="peer_dense",
    )(xn_bf16, u_bf16, vt_bf16, t1, s2, e1, e2, h, norm_final.reshape(1, D_MODEL))


def peer_block(h, xn_bf16, wpq_bf16, keys_bf16, u_bf16, vt_bf16, norm_final):
    n = h.shape[0]
    tm = PEER_TM if n >= PEER_TM else PEER_TM_SMALL
    pad = -n % tm
    hp, xb = jnp.pad(h, ((0, pad), (0, 0))), jnp.pad(xn_bf16, ((0, pad), (0, 0)))
    t1, s2, e1, e2 = peer_route(xb, wpq_bf16, keys_bf16, PEER_TM_SMALL)
    return peer_dense(xb, u_bf16, vt_bf16, t1, s2, e1, e2, hp, norm_final, tm, PEER_TE)[:n]


def layer(x, pos, prev0, s0, attend, norm_mix, w_in_bf16, rw, w_out_bf16, norm_ffn, peer, norm_final):
    mu, w0, w_dec, a0, w_aaa, w_gate, k_k, k_a, r_k, gn_g, gn_b = rw
    b, t, d = x.shape
    n = b * t
    if t == 1:
        xr, posr = x.reshape(1, b, d), jnp.broadcast_to(pos, (b,))
    else:
        xr, posr = x, pos
    seq = lambda z: z.reshape(b, t, z.shape[-1])
    pr, q, k, v = map(seq, project(xr, norm_mix, w_in_bf16, posr, min(xr.shape[1], PROJ_TT)))
    r, w, k2, kk, bb, vv, g, bonus = rwkv_prep(pr, prev0, mu, w0, w_dec, a0, w_aaa, w_gate, k_k, k_a, r_k, min(t, 256))
    y, s_new = rwkv_scan(r, w, k2, kk, bb, vv, s0, RWKV_NB, min(t, 128))
    a_out = attend(q, k, v)
    flat = lambda z: z.reshape(n, z.shape[-1])
    h, xn = merge(flat(y), flat(bonus), flat(g), gn_g, gn_b, flat(a_out), flat(x), w_out_bf16, norm_ffn, min(n, 512))
    out = peer_block(h, xn, *peer, norm_final)
    heads = lambda z: z.reshape(b, t, H_ATT, D_HEAD)
    return out.reshape(b, t, d), heads(k), heads(v), s_new, pr[:, -1]


def kernel(x_prompt, x_sample, cache_k, cache_v, page_table, state_wkv, state_shift, norm_mix, w_in, mu_shift, w0, w_decay_up, a0, w_aaa_up, w_gate_up, k_k, k_a, r_k, gn_gain, gn_bias, w_out, norm_ffn, w_pq, peer_sub_keys, expert_u, expert_v, norm_final):
    l = 0
    bf16 = jnp.bfloat16
    rw = (mu_shift[l], w0[l], w_decay_up[l], a0[l], w_aaa_up[l], w_gate_up[l], k_k[l], k_a[l], r_k[l], gn_gain[l], gn_bias[l])
    peer = (w_pq[l].astype(bf16), peer_sub_keys[l].astype(bf16), expert_u[l].astype(bf16), expert_v[l].T.astype(bf16))
    shared = (norm_mix[l], w_in[l].astype(bf16), rw, w_out[l].astype(bf16), norm_ffn[l], peer, norm_final)
    bp, tp, _ = x_prompt.shape
    bs, ts, _ = x_sample.shape
    y_p, k_p, v_p, s_p, sh_p = layer(
        x_prompt, jnp.arange(tp, dtype=jnp.int32), jnp.zeros((bp, W_RWKV_IN), jnp.float32),
        jnp.zeros((bp, H_RWKV, D_HEAD, D_HEAD), jnp.float32), moba_prompt, *shared)
    attend_cache = lambda q, k, v: moba_sample(q, k, v, cache_k[l], cache_v[l], page_table)
    y_s, k_s, v_s, s_s, sh_s = layer(
        x_sample, PAST_LEN + jnp.arange(ts, dtype=jnp.int32), state_shift[l], state_wkv[l], attend_cache, *shared)
    return (y_p, y_s, k_p[None], v_p[None], s_p[None], sh_p[None], k_s[None], v_s[None], s_s[None], sh_s[None])
```

```python
import functools
import math

import jax
import jax.numpy as jnp
from jax import lax
from jax.experimental import pallas as pl
from jax.experimental.pallas import tpu as pltpu

D_MODEL = 1024
DEPTH = 1
PAST_LEN = 16384
PAGE_SIZE = 128
D_HEAD = 64
H_RWKV = 8
H_ATT = 8
W_RWKV = H_RWKV * D_HEAD
W_ATT = H_ATT * D_HEAD
D_DECAY_LORA = 64
D_AAA_LORA = 64
D_GATE_LORA = 128
W_RWKV_IN = 3 * W_RWKV + D_DECAY_LORA + D_AAA_LORA + D_GATE_LORA
W_IN = W_RWKV_IN + 3 * W_ATT
GN_EPS = 64e-5
NORM_EPS = 1e-6
MOBA_BLOCK = 256
MOBA_TOPK = 3
Q_CHUNK = 64
ROT_DIM = D_HEAD // 4
ROPE_THETA = 500000.0
PEER_HEADS = 8
PEER_NKEYS = 128
PEER_TOPK = 16
PEER_DQ = 256
PEER_EXPERTS = PEER_NKEYS * PEER_NKEYS
PEER_TM = 512
PEER_TM_SMALL = 256
PEER_TE = 1024
PEER_ELB = 4
PEER_SGR = 32
NEG_INF = float("-inf")
VMEM_LIMIT = 56 * 1024 * 1024
LANES = 128
SUBLANES = 8
PROJ_TT = 512
RWKV_NB = 8
NT_DIMS = (((1,), (1,)), ((), ()))


def _rms_norm(x, g):
    return x * lax.rsqrt(jnp.mean(x * x, axis=-1, keepdims=True) + NORM_EPS) * g


def rope_tables(pos):
    half = ROT_DIM // 2
    inv = 1.0 / (ROPE_THETA ** (jnp.arange(half, dtype=jnp.float32) * 2.0 / ROT_DIM))
    ang = pos.astype(jnp.float32)[:, None] * inv[None, :]
    lane = jnp.arange(LANES) % D_HEAD
    idx = lane % half
    cos = jnp.where(lane[None, :] < ROT_DIM, jnp.cos(ang)[:, idx], 1.0)
    sin = jnp.sin(ang)[:, idx]
    sin_lo = jnp.where(lane[None, :] < half, -sin, 0.0)
    sin_hi = jnp.where((lane[None, :] >= half) & (lane[None, :] < ROT_DIM), sin, 0.0)
    return cos, sin_lo, sin_hi


def _project_kernel(x_ref, g_ref, w_ref, cos_ref, slo_ref, shi_ref, pr_o, q_o, k_o, v_o):
    xn = _rms_norm(x_ref[0], g_ref[...])
    p = jnp.dot(xn.astype(jnp.bfloat16), w_ref[...], preferred_element_type=jnp.float32)
    pr_o[0] = p[:, :W_RWKV_IN]
    reps = W_ATT // LANES
    cos = jnp.concatenate([cos_ref[...]] * reps, axis=1)
    slo = jnp.concatenate([slo_ref[...]] * reps, axis=1)
    shi = jnp.concatenate([shi_ref[...]] * reps, axis=1)
    half = ROT_DIM // 2

    def rope(z):
        ahead = pltpu.roll(z, W_ATT - half, 1)
        behind = pltpu.roll(z, half, 1)
        return z * cos + ahead * slo + behind * shi

    q_o[0] = rope(p[:, W_RWKV_IN:W_RWKV_IN + W_ATT])
    k_o[0] = rope(p[:, W_RWKV_IN + W_ATT:W_RWKV_IN + 2 * W_ATT])
    v_o[0] = p[:, W_RWKV_IN + 2 * W_ATT:]


def project(x, norm_g, w_in_bf16, pos, tt):
    b, t, d = x.shape
    cos, slo, shi = rope_tables(pos)
    f32 = jnp.float32
    tab_spec = pl.BlockSpec((tt, LANES), lambda bi, ti: (ti, 0))
    out_spec = lambda w: pl.BlockSpec((1, tt, w), lambda bi, ti: (bi, ti, 0))
    return pl.pallas_call(
        _project_kernel,
        grid=(b, t // tt),
        in_specs=[
            pl.BlockSpec((1, tt, d), lambda bi, ti: (bi, ti, 0)),
            pl.BlockSpec((1, d), lambda bi, ti: (0, 0)),
            pl.BlockSpec((d, W_IN), lambda bi, ti: (0, 0)),
            tab_spec, tab_spec, tab_spec,
        ],
        out_specs=[out_spec(W_RWKV_IN), out_spec(W_ATT), out_spec(W_ATT), out_spec(W_ATT)],
        out_shape=[jax.ShapeDtypeStruct((b, t, W_RWKV_IN), f32)] + [jax.ShapeDtypeStruct((b, t, W_ATT), f32)] * 3,
        compiler_params=pltpu.CompilerParams(dimension_semantics=("parallel", "parallel"),
                                             vmem_limit_bytes=VMEM_LIMIT),
        name="project",
    )(x, norm_g.reshape(1, d), w_in_bf16, cos, slo, shi)


N_PAIR = W_RWKV // LANES
HI = lax.Precision.HIGHEST


def _sigmoid(x):
    return 1.0 / (1.0 + jnp.exp(-x))


def _softplus(x):
    return jnp.maximum(x, 0.0) + jnp.log(1.0 + jnp.exp(-jnp.abs(x)))


def _rwkv_prep_kernel(p_ref, prev0_ref, mu_ref, w0_ref, wdec_ref, a0_ref, waaa_ref, wgate_ref, kk_w_ref, ka_ref,
                      rk_ref, seg_ref, r_o, w_o, k_o, kk_o, b_o, v_o, g_o, bonus_o, carry_ref):
    ti = pl.program_id(1)

    @pl.when(ti == 0)
    def _():
        carry_ref[...] = prev0_ref[0]

    p = p_ref[0]
    row = lax.broadcasted_iota(jnp.int32, p.shape, 0)
    if p.shape[0] == 1:
        prev = carry_ref[...]
    else:
        prev = jnp.where(row == 0, carry_ref[...], pltpu.roll(p, 1, 0))
    carry_ref[...] = p[p.shape[0] - 1:, :]
    ps = p + (prev - p) * mu_ref[...]
    r = ps[:, 0:W_RWKV]
    k = ps[:, W_RWKV:2 * W_RWKV]
    v = ps[:, 2 * W_RWKV:3 * W_RWKV]
    lora = ps[:, 3 * W_RWKV:3 * W_RWKV + LANES]
    gd = ps[:, 3 * W_RWKV + LANES:]
    dec = jnp.dot(jnp.tanh(lora).astype(jnp.bfloat16), wdec_ref[...], preferred_element_type=jnp.float32)
    logw = -jnp.exp(-_softplus(-(w0_ref[...] + dec)) - 0.5)
    a = _sigmoid(a0_ref[...] + jnp.dot(lora.astype(jnp.bfloat16), waaa_ref[...], preferred_element_type=jnp.float32))
    g = jnp.dot(_sigmoid(gd).astype(jnp.bfloat16), wgate_ref[...], preferred_element_type=jnp.float32)
    kk = k * kk_w_ref[...]
    sumsq = jnp.dot(kk * kk, seg_ref[...], precision=HI, preferred_element_type=jnp.float32)
    kk = kk / jnp.maximum(jnp.sqrt(sumsq), 1e-12)
    k2 = k * (1.0 + (a - 1.0) * ka_ref[...])
    rkk = jnp.dot(r * k2 * rk_ref[...], seg_ref[...], precision=HI, preferred_element_type=jnp.float32)
    r_o[0] = r
    w_o[0] = jnp.exp(logw)
    k_o[0] = k2
    kk_o[0] = kk
    b_o[0] = kk * a
    v_o[0] = v
    g_o[0] = g
    bonus_o[0] = rkk * v


def rwkv_prep(p, prev0, mu, w0, w_dec, a0, w_aaa, w_gate, k_k, k_a, r_k, tt):
    b, t, _ = p.shape
    f32 = jnp.float32
    wdec_pad = jnp.concatenate([w_dec, jnp.zeros_like(w_aaa)], axis=0).astype(jnp.bfloat16)
    waaa_pad = jnp.concatenate([jnp.zeros_like(w_dec), w_aaa], axis=0).astype(jnp.bfloat16)
    head = jnp.arange(W_RWKV) // D_HEAD
    seg = (head[:, None] == head[None, :]).astype(f32)
    row = lambda z: z.reshape(1, -1).astype(f32)
    vec_spec = lambda n: pl.BlockSpec((1, n), lambda bi, ti: (0, 0))
    mat_spec = lambda m, n: pl.BlockSpec((m, n), lambda bi, ti: (0, 0))
    out_spec = pl.BlockSpec((1, tt, W_RWKV), lambda bi, ti: (bi, ti, 0))
    out = jax.ShapeDtypeStruct((b, t, W_RWKV), f32)
    return pl.pallas_call(
        _rwkv_prep_kernel,
        grid=(b, t // tt),
        in_specs=[
            pl.BlockSpec((1, tt, W_RWKV_IN), lambda bi, ti: (bi, ti, 0)),
            pl.BlockSpec((1, 1, W_RWKV_IN), lambda bi, ti: (bi, 0, 0)),
            vec_spec(W_RWKV_IN), vec_spec(W_RWKV), mat_spec(LANES, W_RWKV), vec_spec(W_RWKV),
            mat_spec(LANES, W_RWKV), mat_spec(D_GATE_LORA, W_RWKV), vec_spec(W_RWKV), vec_spec(W_RWKV),
            vec_spec(W_RWKV), mat_spec(W_RWKV, W_RWKV),
        ],
        out_specs=[out_spec] * 8,
        out_shape=[out] * 8,
        scratch_shapes=[pltpu.VMEM((1, W_RWKV_IN), f32)],
        compiler_params=pltpu.CompilerParams(dimension_semantics=("parallel", "arbitrary"),
                                             vmem_limit_bytes=VMEM_LIMIT),
        name="rwkv_prep",
    )(p, prev0.reshape(b, 1, W_RWKV_IN), row(mu), row(w0), wdec_pad, row(a0), waaa_pad, w_gate.astype(jnp.bfloat16),
      row(k_k), row(k_a), row(r_k), seg)


def _rwkv_scan_kernel(r_ref, w_ref, k_ref, kk_ref, b_ref, v_ref, s0_ref, y_ref, sT_ref, s_ref):
    ci = pl.program_id(1)
    nb, tc = r_ref.shape[0], r_ref.shape[1]
    tiles = [(bi, p) for bi in range(nb) for p in range(N_PAIR)]

    @pl.when(ci == 0)
    def _():
        for i, (bi, p) in enumerate(tiles):
            s_ref[i] = s0_ref[bi, p]

    lane = lax.broadcasted_iota(jnp.int32, (D_HEAD, LANES), 1)
    sub = lax.broadcasted_iota(jnp.int32, (D_HEAD, LANES), 0)
    first = lane < D_HEAD
    eye2 = (sub == (lane % D_HEAD))
    sub128 = lax.broadcasted_iota(jnp.int32, (LANES, LANES), 0)
    lane128 = lax.broadcasted_iota(jnp.int32, (LANES, LANES), 1)
    eye128 = sub128 == lane128
    seg = jnp.where((sub128 // D_HEAD) == (lane128 // D_HEAD), 1.0, 0.0).astype(jnp.bfloat16)

    def head_sums(xs):
        parts = []
        for x in xs:
            hi = x.astype(jnp.bfloat16)
            parts += [hi, (x - hi.astype(jnp.float32)).astype(jnp.bfloat16)]
        both = jnp.dot(jnp.concatenate(parts, axis=0), seg, preferred_element_type=jnp.float32)
        return [both[i * LANES:i * LANES + D_HEAD] + both[i * LANES + D_HEAD:(i + 1) * LANES] for i in range(len(xs))]

    grp = min(tc, SUBLANES)

    def steps(gi, carry):
        rows = pl.ds(pl.multiple_of(gi * grp, grp), grp)
        lanes = lambda p: slice(p * LANES, (p + 1) * LANES)
        load = lambda ref: [ref[bi, rows, lanes(p)] for bi, p in tiles]
        r_g, w_g, k_g, kk_g, b_g, v_g = load(r_ref), load(w_ref), load(k_ref), load(kk_ref), load(b_ref), load(v_ref)
        states = [s_ref[i] for i in range(len(tiles))]
        yrows = [[] for _ in tiles]
        for j in range(grp):
            row = slice(j, j + 1)
            sks = head_sums([s * kk[row] for s, kk in zip(states, kk_g)])
            for i in range(len(tiles)):
                vcol = jnp.sum(jnp.where(eye128, v_g[i][row], 0.0), axis=1, keepdims=True)
                vmat = jnp.where(first, vcol[0:D_HEAD], vcol[D_HEAD:LANES])
                states[i] = states[i] * w_g[i][row] - sks[i] * b_g[i][row] + vmat * k_g[i][row]
            ys = head_sums([s * r[row] for s, r in zip(states, r_g)])
            for i in range(len(tiles)):
                yrows[i].append(jnp.sum(jnp.where(eye2, ys[i], 0.0), axis=0, keepdims=True))
        for i, (bi, p) in enumerate(tiles):
            y_ref[bi, rows, lanes(p)] = yrows[i][0] if grp == 1 else jnp.concatenate(yrows[i], axis=0)
            s_ref[i] = states[i]
        return carry

    lax.fori_loop(0, tc // grp, steps, 0)

    @pl.when(ci == pl.num_programs(1) - 1)
    def _():
        for i, (bi, p) in enumerate(tiles):
            sT_ref[bi, p] = s_ref[i]


def rwkv_scan(r, w, k, kk, bb, v, s0, nb, tc):
    b, t, _ = r.shape
    f32 = jnp.float32
    pair = lambda s: s.reshape(b, N_PAIR, 2, D_HEAD, D_HEAD).transpose(0, 1, 3, 2, 4).reshape(b, N_PAIR, D_HEAD, LANES)
    unpair = lambda s: s.reshape(b, N_PAIR, D_HEAD, 2, D_HEAD).transpose(0, 1, 3, 2, 4).reshape(b, H_RWKV, D_HEAD, D_HEAD)
    seq_spec = pl.BlockSpec((nb, tc, W_RWKV), lambda bi, ci: (bi, ci, 0))
    st_spec = pl.BlockSpec((nb, N_PAIR, D_HEAD, LANES), lambda bi, ci: (bi, 0, 0, 0))
    y, s_fin = pl.pallas_call(
        _rwkv_scan_kernel,
        grid=(b // nb, t // tc),
        in_specs=[seq_spec] * 6 + [st_spec],
        out_specs=[seq_spec, st_spec],
        out_shape=[jax.ShapeDtypeStruct((b, t, W_RWKV), f32), jax.ShapeDtypeStruct((b, N_PAIR, D_HEAD, LANES), f32)],
        scratch_shapes=[pltpu.VMEM((nb * N_PAIR, D_HEAD, LANES), f32)],
        compiler_params=pltpu.CompilerParams(dimension_semantics=("parallel", "arbitrary"),
                                             vmem_limit_bytes=VMEM_LIMIT),
        name="rwkv_scan",
    )(r, w, k, kk, bb, v, pair(s0.astype(f32)))
    return y, unpair(s_fin)


def _merge_kernel(y_ref, bonus_ref, g_ref, gng_ref, gnb_ref, seg_ref, a_ref, x_ref, wr_ref, wa_ref, nf_ref, h_o, xn_o):
    y = y_ref[...]
    inv = 1.0 / D_HEAD
    mean = jnp.dot(y, seg_ref[...], precision=HI, preferred_element_type=jnp.float32) * inv
    dlt = y - mean
    var = jnp.dot(dlt * dlt, seg_ref[...], precision=HI, preferred_element_type=jnp.float32) * inv
    yn = dlt * lax.rsqrt(var + GN_EPS) * gng_ref[...] + gnb_ref[...]
    r_out = (yn + bonus_ref[...]) * g_ref[...]
    h = (x_ref[...] + jnp.dot(r_out.astype(jnp.bfloat16), wr_ref[...], preferred_element_type=jnp.float32)
         + jnp.dot(a_ref[...].astype(jnp.bfloat16), wa_ref[...], preferred_element_type=jnp.float32))
    h_o[...] = h
    xn_o[...] = _rms_norm(h, nf_ref[...]).astype(jnp.bfloat16)


def merge(y, bonus, g, gn_g, gn_b, a_out, x, w_out_bf16, norm_ffn, tt):
    n, d = x.shape
    head = jnp.arange(W_RWKV) // D_HEAD
    seg = (head[:, None] == head[None, :]).astype(jnp.float32)
    row = lambda w: pl.BlockSpec((tt, w), lambda i: (i, 0))
    vec = lambda w: pl.BlockSpec((1, w), lambda i: (0, 0))
    mat = lambda r, c: pl.BlockSpec((r, c), lambda i: (0, 0))
    return pl.pallas_call(
        _merge_kernel,
        grid=(n // tt,),
        in_specs=[row(W_RWKV), row(W_RWKV), row(W_RWKV), vec(W_RWKV), vec(W_RWKV), mat(W_RWKV, W_RWKV),
                  row(W_ATT), row(d), mat(W_RWKV, d), mat(W_ATT, d), vec(d)],
        out_specs=[row(d), row(d)],
        out_shape=[jax.ShapeDtypeStruct((n, d), jnp.float32), jax.ShapeDtypeStruct((n, d), jnp.bfloat16)],
        compiler_params=pltpu.CompilerParams(dimension_semantics=("parallel",), vmem_limit_bytes=VMEM_LIMIT),
        name="merge",
    )(y, bonus, g, gn_g.reshape(1, -1), gn_b.reshape(1, -1), seg, a_out, x, w_out_bf16[:W_RWKV], w_out_bf16[W_RWKV:],
      norm_ffn.reshape(1, d))


def _moba_block_kernel(q_ref, k_ref, v_ref, o_ref, *, qi):
    scale = D_HEAD ** -0.5
    nb = qi + 1
    k_all = k_ref[0]
    kb = k_all.astype(jnp.bfloat16)
    vt = v_ref[0].T.astype(jnp.bfloat16)
    q2 = q_ref[0]
    lane = lax.broadcasted_iota(jnp.int32, q2.shape, 1)
    key_row = lax.broadcasted_iota(jnp.int32, (MOBA_BLOCK, MOBA_BLOCK), 0)
    qry_col = lax.broadcasted_iota(jnp.int32, (MOBA_BLOCK, MOBA_BLOCK), 1)
    row = lax.broadcasted_iota(jnp.int32, (LANES, MOBA_BLOCK), 0)
    blk = lambda z, j: z[j * MOBA_BLOCK:(j + 1) * MOBA_BLOCK]
    outs = []
    for s in range(2):
        qm = jnp.where((lane // D_HEAD) == s, q2, 0.0)
        qmb = qm.astype(jnp.bfloat16)
        picked = [None] * qi
        if qi > MOBA_TOPK:
            kmean = jnp.concatenate([jnp.mean(blk(k_all, j), axis=0, keepdims=True) for j in range(qi)], axis=0)
            gate = lax.dot_general(kmean, qm, NT_DIMS, precision=HI, preferred_element_type=jnp.float32)
            for j in range(qi):
                ahead = jnp.zeros((1, MOBA_BLOCK), jnp.float32)
                for jp in range(qi):
                    if jp != j:
                        beats = (gate[jp:jp + 1] >= gate[j:j + 1]) if jp < j else (gate[jp:jp + 1] > gate[j:j + 1])
                        ahead = ahead + jnp.where(beats, 1.0, 0.0)
                picked[j] = ahead < float(MOBA_TOPK)
        scores = []
        for j in range(nb):
            st = lax.dot_general(blk(kb, j), qmb, NT_DIMS, preferred_element_type=jnp.float32) * scale
            if j == qi:
                st = jnp.where(key_row <= qry_col, st, NEG_INF)
            elif picked[j] is not None:
                st = jnp.where(picked[j], st, NEG_INF)
            scores.append(st)
        m = functools.reduce(jnp.maximum, [jnp.max(st, axis=0, keepdims=True) for st in scores])
        ps = [jnp.exp(st - m) for st in scores]
        denom = functools.reduce(jnp.add, [jnp.sum(p, axis=0, keepdims=True) for p in ps])
        acc = functools.reduce(jnp.add, [
            jnp.dot(vt[:, j * MOBA_BLOCK:(j + 1) * MOBA_BLOCK], ps[j].astype(jnp.bfloat16),
                    preferred_element_type=jnp.float32) for j in range(nb)])
        outs.append(acc / denom)
    o_ref[0] = jnp.where((row // D_HEAD) == 0, outs[0], outs[1]).T


def moba_prompt(q, k, v):
    b, t, w = q.shape
    outs = []
    for qi in range(t // MOBA_BLOCK):
        n_keys = (qi + 1) * MOBA_BLOCK
        outs.append(pl.pallas_call(
            functools.partial(_moba_block_kernel, qi=qi),
            grid=(b, w // LANES),
            in_specs=[
                pl.BlockSpec((1, MOBA_BLOCK, LANES), lambda bi, hp, qi=qi: (bi, qi, hp)),
                pl.BlockSpec((1, n_keys, LANES), lambda bi, hp: (bi, 0, hp)),
                pl.BlockSpec((1, n_keys, LANES), lambda bi, hp: (bi, 0, hp)),
            ],
            out_specs=pl.BlockSpec((1, MOBA_BLOCK, LANES), lambda bi, hp: (bi, 0, hp)),
            out_shape=jax.ShapeDtypeStruct((b, MOBA_BLOCK, w), jnp.float32),
            compiler_params=pltpu.CompilerParams(dimension_semantics=("parallel", "parallel"),
                                                 vmem_limit_bytes=VMEM_LIMIT),
            name=f"moba_prompt_q{qi}",
        )(q, k, v))
    return jnp.concatenate(outs, axis=1)


PAGES_PER_BLOCK = MOBA_BLOCK // PAGE_SIZE


def _sample_select_kernel(q_ref, km_ref, idx_ref):
    gate = jnp.sum(km_ref[0] * q_ref[...], axis=-1, keepdims=True)
    nb = gate.shape[0]
    blocks = lax.broadcasted_iota(jnp.int32, gate.shape, 0).astype(jnp.float32)
    for i in range(MOBA_TOPK):
        m = jnp.max(gate, axis=0, keepdims=True)
        first = jnp.min(jnp.where(gate == m, blocks, float(nb)), axis=0, keepdims=True)
        gate = jnp.where(blocks == first, NEG_INF, gate)
        idx_ref[0, i] = first[0]


def sample_select(q, kmean):
    bd, nb, h, dh = kmean.shape
    idx = pl.pallas_call(
        _sample_select_kernel,
        grid=(bd,),
        in_specs=[pl.BlockSpec((1, h, dh), lambda b: (b, 0, 0)), pl.BlockSpec((1, nb, h, dh), lambda b: (b, 0, 0, 0))],
        out_specs=pl.BlockSpec((1, MOBA_TOPK, h, 1), lambda b: (b, 0, 0, 0)),
        out_shape=jax.ShapeDtypeStruct((bd, MOBA_TOPK, h, 1), jnp.float32),
        compiler_params=pltpu.CompilerParams(dimension_semantics=("parallel",), vmem_limit_bytes=VMEM_LIMIT),
        name="sample_select",
    )(q, kmean)
    return idx[..., 0].astype(jnp.int32).transpose(0, 2, 1)


def _sample_attend_kernel(q_ref, ko_ref, vo_ref, ks_ref, vs_ref, o_ref):
    scale = D_HEAD ** -0.5
    for h in range(H_ATT):
        qh = q_ref[0, h:h + 1, :]
        s_sel = jnp.sum(ks_ref[0, h] * qh, axis=1, keepdims=True) * scale
        s_own = jnp.sum(ko_ref[0, h:h + 1, :] * qh, axis=1, keepdims=True) * scale
        m = jnp.maximum(jnp.max(s_sel, axis=0, keepdims=True), s_own)
        p_sel = jnp.exp(s_sel - m)
        p_own = jnp.exp(s_own - m)
        denom = jnp.sum(p_sel, axis=0, keepdims=True) + p_own
        acc = jnp.sum(p_sel * vs_ref[0, h], axis=0, keepdims=True) + p_own * vo_ref[0, h:h + 1, :]
        o_ref[0, h:h + 1, :] = acc / denom


def sample_attend(q, k_own, v_own, k_sel, v_sel):
    bd, h, n_keys, dh = k_sel.shape
    tok = pl.BlockSpec((1, h, dh), lambda b: (b, 0, 0))
    sel = pl.BlockSpec((1, h, n_keys, dh), lambda b: (b, 0, 0, 0))
    return pl.pallas_call(
        _sample_attend_kernel,
        grid=(bd,),
        in_specs=[tok, tok, tok, sel, sel],
        out_specs=tok,
        out_shape=jax.ShapeDtypeStruct((bd, h, dh), jnp.float32),
        compiler_params=pltpu.CompilerParams(dimension_semantics=("parallel",), vmem_limit_bytes=VMEM_LIMIT),
        name="sample_attend",
    )(q, k_own, v_own, k_sel, v_sel)


def cached_block_kmeans(cache_k, page_table):
    bd, n_pages = page_table.shape
    blocks = cache_k[page_table].reshape(bd, n_pages // PAGES_PER_BLOCK, MOBA_BLOCK, H_ATT, D_HEAD)
    return jnp.mean(blocks, axis=2)


def moba_sample(q, k, v, kmean, cache_k, cache_v, page_table):
    bd, s, w = q.shape
    assert s == 1 and PAST_LEN % MOBA_BLOCK == 0 and PAST_LEN // MOBA_BLOCK >= MOBA_TOPK
    heads = lambda z: z.reshape(bd, H_ATT, D_HEAD)
    sel = sample_select(heads(q), kmean)
    logical = sel[..., None] * PAGES_PER_BLOCK + jnp.arange(PAGES_PER_BLOCK)
    phys = page_table[jnp.arange(bd)[:, None, None, None], logical]
    hi = jnp.arange(H_ATT)[None, :, None, None]
    gather = lambda c: c[phys, :, hi, :].reshape(bd, H_ATT, MOBA_TOPK * MOBA_BLOCK, D_HEAD)
    out = sample_attend(heads(q), heads(k), heads(v), gather(cache_k), gather(cache_v))
    return out.reshape(bd, 1, w)


PEER_CANDS = [(a, b) for a in range(PEER_TOPK) for b in range(PEER_TOPK) if (a + 1) * (b + 1) <= PEER_TOPK]
PEER_NCAND = -(-len(PEER_CANDS) // 8) * 8


def _extract_topk(x, n_rows, k, exact):
    rows = lax.broadcasted_iota(jnp.int32, x.shape, 0).astype(jnp.float32)
    vals = []
    for _ in range(k):
        m = jnp.max(x, axis=0, keepdims=True)
        if exact:
            first = jnp.min(jnp.where(x == m, rows, float(n_rows)), axis=0, keepdims=True)
            x = jnp.where(rows == first, NEG_INF, x)
        else:
            x = jnp.where(x == m, NEG_INF, x)
        vals.append(m)
    return x, vals


def _peer_route_tables(q_ref, keys_ref, t1_ref, s2_ref, e1_ref, e2_ref, sv_ref, comb_ref, exact):
    half = PEER_DQ // 2
    tm = q_ref.shape[0]
    n_pad = PEER_NCAND - len(PEER_CANDS)
    ties = jnp.zeros((1, tm), jnp.float32)
    removed = lambda rem: jnp.sum(jnp.where(rem == NEG_INF, 1.0, 0.0), axis=0, keepdims=True)
    for h in range(PEER_HEADS):
        masked = []
        for p in range(2):
            qs = q_ref[:, (2 * h + p) * half:(2 * h + p + 1) * half].astype(jnp.bfloat16)
            st = lax.dot_general(keys_ref[p], qs, NT_DIMS, preferred_element_type=jnp.float32)
            rem, vals = _extract_topk(st, PEER_NKEYS, PEER_TOPK, exact)
            if not exact:
                ties = ties + (removed(rem) - float(PEER_TOPK))
            for i, v in enumerate(vals):
                sv_ref[p, i:i + 1, :] = v
            masked.append(jnp.where(rem == NEG_INF, st, NEG_INF))
        comb_ref[...] = jnp.full(comb_ref.shape, NEG_INF, jnp.float32)
        for c, (a, b) in enumerate(PEER_CANDS):
            comb_ref[c:c + 1, :] = sv_ref[0, a:a + 1, :] + sv_ref[1, b:b + 1, :]
        rem, cvals = _extract_topk(comb_ref[...], PEER_NCAND, PEER_TOPK + 1, exact)
        if not exact:
            ties = ties + (removed(rem) - float(PEER_TOPK + 1 + n_pad))
        cmax = cvals[0]
        z = jnp.zeros_like(cmax)
        for v in cvals[:PEER_TOPK]:
            z = z + jnp.exp(v - cmax)
        cut = 0.5 * (cvals[PEER_TOPK - 1] + cvals[PEER_TOPK])
        t1_ref[h] = cut - masked[0]
        s2_ref[h] = masked[1]
        e1_ref[h] = jnp.exp(masked[0] - sv_ref[0, 0:1, :])
        e2_ref[h] = jnp.exp(masked[1] - sv_ref[1, 0:1, :]) / z
    return ties


def _peer_route_kernel(xn_ref, wpq_ref, keys_ref, t1_ref, s2_ref, e1_ref, e2_ref, q_ref, sv_ref, comb_ref):
    q_ref[...] = jnp.dot(xn_ref[...], wpq_ref[...], preferred_element_type=jnp.float32)
    tables = functools.partial(_peer_route_tables, q_ref, keys_ref, t1_ref, s2_ref, e1_ref, e2_ref, sv_ref, comb_ref)
    ties = tables(exact=False)

    @pl.when(jnp.max(ties) > 0.0)
    def _():
        tables(exact=True)


def peer_route(xn_bf16, wpq_bf16, keys_bf16, tm):
    n = xn_bf16.shape[0]
    tab = jax.ShapeDtypeStruct((PEER_HEADS, PEER_NKEYS, n), jnp.float32)
    tab_spec = pl.BlockSpec((PEER_HEADS, PEER_NKEYS, tm), lambda i: (0, 0, i))
    return pl.pallas_call(
        _peer_route_kernel,
        grid=(n // tm,),
        in_specs=[
            pl.BlockSpec((tm, D_MODEL), lambda i: (i, 0)),
            pl.BlockSpec((D_MODEL, PEER_HEADS * PEER_DQ), lambda i: (0, 0)),
            pl.BlockSpec((2, PEER_NKEYS, PEER_DQ // 2), lambda i: (0, 0, 0)),
        ],
        out_specs=[tab_spec] * 4,
        out_shape=[tab] * 4,
        scratch_shapes=[pltpu.VMEM((tm, PEER_HEADS * PEER_DQ), jnp.float32),
                        pltpu.VMEM((2, PEER_TOPK, tm), jnp.float32),
                        pltpu.VMEM((PEER_NCAND, tm), jnp.float32)],
        compiler_params=pltpu.CompilerParams(dimension_semantics=("parallel",),
                                             vmem_limit_bytes=VMEM_LIMIT),
        name="peer_route",
    )(xn_bf16, wpq_bf16, keys_bf16)


def _gelu_exact(x):
    return 0.5 * x * (1.0 + lax.erf(x * (1.0 / math.sqrt(2.0))))


def _peer_dense_kernel(xn_ref, u_ref, vt_ref, t1_ref, s2_ref, e1_ref, e2_ref, h_ref, nfin_ref, o_ref, acc_ref, p_ref):
    j = pl.program_id(1)
    n_e1 = u_ref.shape[0] // PEER_NKEYS
    tm = xn_ref.shape[0]

    @pl.when(j == 0)
    def _():
        acc_ref[...] = jnp.zeros_like(acc_ref)

    for eb in range(n_e1 // PEER_ELB):
        erows = slice(eb * PEER_ELB * PEER_NKEYS, (eb + 1) * PEER_ELB * PEER_NKEYS)
        at = lax.dot_general(u_ref[erows, :], xn_ref[...], NT_DIMS,
                             preferred_element_type=jnp.float32)
        for c in range(tm // LANES):
            cols = slice(c * LANES, (c + 1) * LANES)
            for r0 in range(0, PEER_NKEYS, PEER_SGR):
                accs = [jnp.zeros((PEER_SGR, LANES), jnp.float32) for _ in range(PEER_ELB)]
                for h in range(PEER_HEADS):
                    s2t = s2_ref[h, r0:r0 + PEER_SGR, cols]
                    e2t = e2_ref[h, r0:r0 + PEER_SGR, cols]
                    for i in range(PEER_ELB):
                        el = eb * PEER_ELB + i
                        picked = s2t >= t1_ref[h, el:el + 1, cols]
                        accs[i] = accs[i] + jnp.where(picked, e2t, 0.0) * e1_ref[h, el:el + 1, cols]
                for i in range(PEER_ELB):
                    lo = i * PEER_NKEYS + r0
                    act = _gelu_exact(at[lo:lo + PEER_SGR, cols])
                    p_ref[eb, lo:lo + PEER_SGR, cols] = (accs[i] * act).astype(jnp.bfloat16)
        acc_ref[...] += jnp.dot(vt_ref[:, erows], p_ref[eb], preferred_element_type=jnp.float32)

    @pl.when(j == pl.num_programs(1) - 1)
    def _():
        o_ref[...] = _rms_norm(h_ref[...] + acc_ref[...].T, nfin_ref[...])


def peer_dense(xn_bf16, u_bf16, vt_bf16, t1, s2, e1, e2, h, norm_final, tm, te):
    n = xn_bf16.shape[0]
    n_e1 = te // PEER_NKEYS
    return pl.pallas_call(
        _peer_dense_kernel,
        grid=(n // tm, PEER_EXPERTS // te),
        in_specs=[
            pl.BlockSpec((tm, D_MODEL), lambda i, j: (i, 0)),
            pl.BlockSpec((te, D_MODEL), lambda i, j: (j, 0)),
            pl.BlockSpec((D_MODEL, te), lambda i, j: (0, j)),
            pl.BlockSpec((PEER_HEADS, n_e1, tm), lambda i, j: (0, j, i)),
            pl.BlockSpec((PEER_HEADS, PEER_NKEYS, tm), lambda i, j: (0, 0, i)),
            pl.BlockSpec((PEER_HEADS, n_e1, tm), lambda i, j: (0, j, i)),
            pl.BlockSpec((PEER_HEADS, PEER_NKEYS, tm), lambda i, j: (0, 0, i)),
            pl.BlockSpec((tm, D_MODEL), lambda i, j: (i, 0)),
            pl.BlockSpec((1, D_MODEL), lambda i, j: (0, 0)),
        ],
        out_specs=pl.BlockSpec((tm, D_MODEL), lambda i, j: (i, 0)),
        out_shape=jax.ShapeDtypeStruct((n, D_MODEL), jnp.float32),
        scratch_shapes=[pltpu.VMEM((D_MODEL, tm), jnp.float32),
                        pltpu.VMEM((n_e1 // PEER_ELB, PEER_ELB * PEER_NKEYS, tm), jnp.bfloat16)],
        compiler_params=pltpu.CompilerParams(dimension_semantics=("parallel", "arbitrary"),
                                             vmem_limit_bytes=VMEM_LIMIT),
        name="peer_dense",
    )(xn_bf16, u_bf16, vt_bf16, t1, s2, e1, e2, h, norm_final.reshape(1, D_MODEL))


def peer_block(h, xn_bf16, wpq_bf16, keys_bf16, u_bf16, vt_bf16, norm_final):
    n = h.shape[0]
    tm = PEER_TM if n >= PEER_TM else PEER_TM_SMALL
    pad = -n % tm
    hp, xb = jnp.pad(h, ((0, pad), (0, 0))), jnp.pad(xn_bf16, ((0, pad), (0, 0)))
    t1, s2, e1, e2 = peer_route(xb, wpq_bf16, keys_bf16, PEER_TM_SMALL)
    return peer_dense(xb, u_bf16, vt_bf16, t1, s2, e1, e2, hp, norm_final, tm, PEER_TE)[:n]


def layer(x, pos, prev0, s0, attend, norm_mix, w_in_bf16, rw, w_out_bf16, norm_ffn, peer, norm_final):
    mu, w0, w_dec, a0, w_aaa, w_gate, k_k, k_a, r_k, gn_g, gn_b = rw
    b, t, d = x.shape
    n = b * t
    if t == 1:
        xr, posr = x.reshape(1, b, d), jnp.broadcast_to(pos, (b,))
    else:
        xr, posr = x, pos
    seq = lambda z: z.reshape(b, t, z.shape[-1])
    pr, q, k, v = map(seq, project(xr, norm_mix, w_in_bf16, posr, min(xr.shape[1], PROJ_TT)))
    r, w, k2, kk, bb, vv, g, bonus = rwkv_prep(pr, prev0, mu, w0, w_dec, a0, w_aaa, w_gate, k_k, k_a, r_k, min(t, 256))
    y, s_new = rwkv_scan(r, w, k2, kk, bb, vv, s0, RWKV_NB, min(t, 128))
    a_out = attend(q, k, v)
    flat = lambda z: z.reshape(n, z.shape[-1])
    h, xn = merge(flat(y), flat(bonus), flat(g), gn_g, gn_b, flat(a_out), flat(x), w_out_bf16, norm_ffn, min(n, 512))
    out = peer_block(h, xn, *peer, norm_final)
    heads = lambda z: z.reshape(b, t, H_ATT, D_HEAD)
    return out.reshape(b, t, d), heads(k), heads(v), s_new, pr[:, -1]


def kernel(x_prompt, x_sample, cache_k, cache_v, page_table, state_wkv, state_shift, norm_mix, w_in, mu_shift, w0, w_decay_up, a0, w_aaa_up, w_gate_up, k_k, k_a, r_k, gn_gain, gn_bias, w_out, norm_ffn, w_pq, peer_sub_keys, expert_u, expert_v, norm_final):
    l = 0
    bf16 = jnp.bfloat16
    rw = (mu_shift[l], w0[l], w_decay_up[l], a0[l], w_aaa_up[l], w_gate_up[l], k_k[l], k_a[l], r_k[l], gn_gain[l], gn_bias[l])
    peer = (w_pq[l].astype(bf16), peer_sub_keys[l].astype(bf16), expert_u[l].astype(bf16), expert_v[l].T.astype(bf16))
    shared = (norm_mix[l], w_in[l].astype(bf16), rw, w_out[l].astype(bf16), norm_ffn[l], peer, norm_final)
    bp, tp, _ = x_prompt.shape
    bs, ts, _ = x_sample.shape
    kmean = cached_block_kmeans(cache_k[l], page_table)
    y_p, k_p, v_p, s_p, sh_p = layer(
        x_prompt, jnp.arange(tp, dtype=jnp.int32), jnp.zeros((bp, W_RWKV_IN), jnp.float32),
        jnp.zeros((bp, H_RWKV, D_HEAD, D_HEAD), jnp.float32), moba_prompt, *shared)
    attend_cache = lambda q, k, v: moba_sample(q, k, v, kmean, cache_k[l], cache_v[l], page_table)
    y_s, k_s, v_s, s_s, sh_s = layer(
        x_sample, PAST_LEN + jnp.arange(ts, dtype=jnp.int32), state_shift[l], state_wkv[l], attend_cache, *shared)
    return (y_p, y_s, k_p[None], v_p[None], s_p[None], sh_p[None], k_s[None], v_s[None], s_s[None], sh_s[None])
```

```python
import functools
import math

import jax
import jax.numpy as jnp
from jax import lax
from jax.experimental import pallas as pl
from jax.experimental.pallas import tpu as pltpu

D_MODEL = 1024
DEPTH = 1
PAST_LEN = 16384
PAGE_SIZE = 128
D_HEAD = 64
H_RWKV = 8
H_ATT = 8
W_RWKV = H_RWKV * D_HEAD
W_ATT = H_ATT * D_HEAD
D_DECAY_LORA = 64
D_AAA_LORA = 64
D_GATE_LORA = 128
W_RWKV_IN = 3 * W_RWKV + D_DECAY_LORA + D_AAA_LORA + D_GATE_LORA
W_IN = W_RWKV_IN + 3 * W_ATT
GN_EPS = 64e-5
NORM_EPS = 1e-6
MOBA_BLOCK = 256
MOBA_TOPK = 3
Q_CHUNK = 64
ROT_DIM = D_HEAD // 4
ROPE_THETA = 500000.0
PEER_HEADS = 8
PEER_NKEYS = 128
PEER_TOPK = 16
PEER_DQ = 256
PEER_EXPERTS = PEER_NKEYS * PEER_NKEYS
PEER_TM = 512
PEER_TM_SMALL = 256
PEER_TE = 2048
PEER_ELB = 4
PEER_SGR = 32
NEG_INF = float("-inf")
VMEM_LIMIT = 56 * 1024 * 1024
LANES = 128
SUBLANES = 8
PROJ_TT = 512
RWKV_NB = 8
NT_DIMS = (((1,), (1,)), ((), ()))


def _rms_norm(x, g):
    return x * lax.rsqrt(jnp.mean(x * x, axis=-1, keepdims=True) + NORM_EPS) * g


def rope_tables(pos):
    half = ROT_DIM // 2
    inv = 1.0 / (ROPE_THETA ** (jnp.arange(half, dtype=jnp.float32) * 2.0 / ROT_DIM))
    ang = pos.astype(jnp.float32)[:, None] * inv[None, :]
    lane = jnp.arange(LANES) % D_HEAD
    idx = lane % half
    cos = jnp.where(lane[None, :] < ROT_DIM, jnp.cos(ang)[:, idx], 1.0)
    sin = jnp.sin(ang)[:, idx]
    sin_lo = jnp.where(lane[None, :] < half, -sin, 0.0)
    sin_hi = jnp.where((lane[None, :] >= half) & (lane[None, :] < ROT_DIM), sin, 0.0)
    return cos, sin_lo, sin_hi


def _project_kernel(x_ref, g_ref, w_ref, cos_ref, slo_ref, shi_ref, pr_o, q_o, k_o, v_o):
    xn = _rms_norm(x_ref[0], g_ref[...])
    p = jnp.dot(xn.astype(jnp.bfloat16), w_ref[...], preferred_element_type=jnp.float32)
    pr_o[0] = p[:, :W_RWKV_IN]
    reps = W_ATT // LANES
    cos = jnp.concatenate([cos_ref[...]] * reps, axis=1)
    slo = jnp.concatenate([slo_ref[...]] * reps, axis=1)
    shi = jnp.concatenate([shi_ref[...]] * reps, axis=1)
    half = ROT_DIM // 2

    def rope(z):
        ahead = pltpu.roll(z, W_ATT - half, 1)
        behind = pltpu.roll(z, half, 1)
        return z * cos + ahead * slo + behind * shi

    q_o[0] = rope(p[:, W_RWKV_IN:W_RWKV_IN + W_ATT])
    k_o[0] = rope(p[:, W_RWKV_IN + W_ATT:W_RWKV_IN + 2 * W_ATT])
    v_o[0] = p[:, W_RWKV_IN + 2 * W_ATT:]


def project(x, norm_g, w_in_bf16, pos, tt):
    b, t, d = x.shape
    cos, slo, shi = rope_tables(pos)
    f32 = jnp.float32
    tab_spec = pl.BlockSpec((tt, LANES), lambda bi, ti: (ti, 0))
    out_spec = lambda w: pl.BlockSpec((1, tt, w), lambda bi, ti: (bi, ti, 0))
    return pl.pallas_call(
        _project_kernel,
        grid=(b, t // tt),
        in_specs=[
            pl.BlockSpec((1, tt, d), lambda bi, ti: (bi, ti, 0)),
            pl.BlockSpec((1, d), lambda bi, ti: (0, 0)),
            pl.BlockSpec((d, W_IN), lambda bi, ti: (0, 0)),
            tab_spec, tab_spec, tab_spec,
        ],
        out_specs=[out_spec(W_RWKV_IN), out_spec(W_ATT), out_spec(W_ATT), out_spec(W_ATT)],
        out_shape=[jax.ShapeDtypeStruct((b, t, W_RWKV_IN), f32)] + [jax.ShapeDtypeStruct((b, t, W_ATT), f32)] * 3,
        compiler_params=pltpu.CompilerParams(dimension_semantics=("parallel", "parallel"),
                                             vmem_limit_bytes=VMEM_LIMIT),
        name="project",
    )(x, norm_g.reshape(1, d), w_in_bf16, cos, slo, shi)


N_PAIR = W_RWKV // LANES
HI = lax.Precision.HIGHEST


def _sigmoid(x):
    return 1.0 / (1.0 + jnp.exp(-x))


def _softplus(x):
    return jnp.maximum(x, 0.0) + jnp.log(1.0 + jnp.exp(-jnp.abs(x)))


def _rwkv_prep_kernel(p_ref, prev0_ref, mu_ref, w0_ref, wdec_ref, a0_ref, waaa_ref, wgate_ref, kk_w_ref, ka_ref,
                      rk_ref, seg_ref, r_o, w_o, k_o, kk_o, b_o, v_o, g_o, bonus_o, carry_ref):
    ti = pl.program_id(1)

    @pl.when(ti == 0)
    def _():
        carry_ref[...] = prev0_ref[0]

    p = p_ref[0]
    row = lax.broadcasted_iota(jnp.int32, p.shape, 0)
    if p.shape[0] == 1:
        prev = carry_ref[...]
    else:
        prev = jnp.where(row == 0, carry_ref[...], pltpu.roll(p, 1, 0))
    carry_ref[...] = p[p.shape[0] - 1:, :]
    ps = p + (prev - p) * mu_ref[...]
    r = ps[:, 0:W_RWKV]
    k = ps[:, W_RWKV:2 * W_RWKV]
    v = ps[:, 2 * W_RWKV:3 * W_RWKV]
    lora = ps[:, 3 * W_RWKV:3 * W_RWKV + LANES]
    gd = ps[:, 3 * W_RWKV + LANES:]
    dec = jnp.dot(jnp.tanh(lora).astype(jnp.bfloat16), wdec_ref[...], preferred_element_type=jnp.float32)
    logw = -jnp.exp(-_softplus(-(w0_ref[...] + dec)) - 0.5)
    a = _sigmoid(a0_ref[...] + jnp.dot(lora.astype(jnp.bfloat16), waaa_ref[...], preferred_element_type=jnp.float32))
    g = jnp.dot(_sigmoid(gd).astype(jnp.bfloat16), wgate_ref[...], preferred_element_type=jnp.float32)
    kk = k * kk_w_ref[...]
    sumsq = jnp.dot(kk * kk, seg_ref[...], precision=HI, preferred_element_type=jnp.float32)
    kk = kk / jnp.maximum(jnp.sqrt(sumsq), 1e-12)
    k2 = k * (1.0 + (a - 1.0) * ka_ref[...])
    rkk = jnp.dot(r * k2 * rk_ref[...], seg_ref[...], precision=HI, preferred_element_type=jnp.float32)
    r_o[0] = r
    w_o[0] = jnp.exp(logw)
    k_o[0] = k2
    kk_o[0] = kk
    b_o[0] = kk * a
    v_o[0] = v
    g_o[0] = g
    bonus_o[0] = rkk * v


def rwkv_prep(p, prev0, mu, w0, w_dec, a0, w_aaa, w_gate, k_k, k_a, r_k, tt):
    b, t, _ = p.shape
    f32 = jnp.float32
    wdec_pad = jnp.concatenate([w_dec, jnp.zeros_like(w_aaa)], axis=0).astype(jnp.bfloat16)
    waaa_pad = jnp.concatenate([jnp.zeros_like(w_dec), w_aaa], axis=0).astype(jnp.bfloat16)
    head = jnp.arange(W_RWKV) // D_HEAD
    seg = (head[:, None] == head[None, :]).astype(f32)
    row = lambda z: z.reshape(1, -1).astype(f32)
    vec_spec = lambda n: pl.BlockSpec((1, n), lambda bi, ti: (0, 0))
    mat_spec = lambda m, n: pl.BlockSpec((m, n), lambda bi, ti: (0, 0))
    out_spec = pl.BlockSpec((1, tt, W_RWKV), lambda bi, ti: (bi, ti, 0))
    out = jax.ShapeDtypeStruct((b, t, W_RWKV), f32)
    return pl.pallas_call(
        _rwkv_prep_kernel,
        grid=(b, t // tt),
        in_specs=[
            pl.BlockSpec((1, tt, W_RWKV_IN), lambda bi, ti: (bi, ti, 0)),
            pl.BlockSpec((1, 1, W_RWKV_IN), lambda bi, ti: (bi, 0, 0)),
            vec_spec(W_RWKV_IN), vec_spec(W_RWKV), mat_spec(LANES, W_RWKV), vec_spec(W_RWKV),
            mat_spec(LANES, W_RWKV), mat_spec(D_GATE_LORA, W_RWKV), vec_spec(W_RWKV), vec_spec(W_RWKV),
            vec_spec(W_RWKV), mat_spec(W_RWKV, W_RWKV),
        ],
        out_specs=[out_spec] * 8,
        out_shape=[out] * 8,
        scratch_shapes=[pltpu.VMEM((1, W_RWKV_IN), f32)],
        compiler_params=pltpu.CompilerParams(dimension_semantics=("parallel", "arbitrary"),
                                             vmem_limit_bytes=VMEM_LIMIT),
        name="rwkv_prep",
    )(p, prev0.reshape(b, 1, W_RWKV_IN), row(mu), row(w0), wdec_pad, row(a0), waaa_pad, w_gate.astype(jnp.bfloat16),
      row(k_k), row(k_a), row(r_k), seg)


def _rwkv_scan_kernel(r_ref, w_ref, k_ref, kk_ref, b_ref, v_ref, s0_ref, y_ref, sT_ref, s_ref):
    ci = pl.program_id(1)
    nb, tc = r_ref.shape[0], r_ref.shape[1]
    tiles = [(bi, p) for bi in range(nb) for p in range(N_PAIR)]

    @pl.when(ci == 0)
    def _():
        for i, (bi, p) in enumerate(tiles):
            s_ref[i] = s0_ref[bi, p]

    lane = lax.broadcasted_iota(jnp.int32, (D_HEAD, LANES), 1)
    sub = lax.broadcasted_iota(jnp.int32, (D_HEAD, LANES), 0)
    first = lane < D_HEAD
    eye2 = (sub == (lane % D_HEAD))
    sub128 = lax.broadcasted_iota(jnp.int32, (LANES, LANES), 0)
    lane128 = lax.broadcasted_iota(jnp.int32, (LANES, LANES), 1)
    eye128 = sub128 == lane128
    seg = jnp.where((sub128 // D_HEAD) == (lane128 // D_HEAD), 1.0, 0.0).astype(jnp.bfloat16)

    def head_sums(xs):
        parts = []
        for x in xs:
            hi = x.astype(jnp.bfloat16)
            parts += [hi, (x - hi.astype(jnp.float32)).astype(jnp.bfloat16)]
        both = jnp.dot(jnp.concatenate(parts, axis=0), seg, preferred_element_type=jnp.float32)
        return [both[i * LANES:i * LANES + D_HEAD] + both[i * LANES + D_HEAD:(i + 1) * LANES] for i in range(len(xs))]

    grp = min(tc, SUBLANES)

    def steps(gi, carry):
        rows = pl.ds(pl.multiple_of(gi * grp, grp), grp)
        lanes = lambda p: slice(p * LANES, (p + 1) * LANES)
        load = lambda ref: [ref[bi, rows, lanes(p)] for bi, p in tiles]
        r_g, w_g, k_g, kk_g, b_g, v_g = load(r_ref), load(w_ref), load(k_ref), load(kk_ref), load(b_ref), load(v_ref)
        states = [s_ref[i] for i in range(len(tiles))]
        yrows = [[] for _ in tiles]
        for j in range(grp):
            row = slice(j, j + 1)
            sks = head_sums([s * kk[row] for s, kk in zip(states, kk_g)])
            for i in range(len(tiles)):
                vcol = jnp.sum(jnp.where(eye128, v_g[i][row], 0.0), axis=1, keepdims=True)
                vmat = jnp.where(first, vcol[0:D_HEAD], vcol[D_HEAD:LANES])
                states[i] = states[i] * w_g[i][row] - sks[i] * b_g[i][row] + vmat * k_g[i][row]
            ys = head_sums([s * r[row] for s, r in zip(states, r_g)])
            for i in range(len(tiles)):
                yrows[i].append(jnp.sum(jnp.where(eye2, ys[i], 0.0), axis=0, keepdims=True))
        for i, (bi, p) in enumerate(tiles):
            y_ref[bi, rows, lanes(p)] = yrows[i][0] if grp == 1 else jnp.concatenate(yrows[i], axis=0)
            s_ref[i] = states[i]
        return carry

    lax.fori_loop(0, tc // grp, steps, 0)

    @pl.when(ci == pl.num_programs(1) - 1)
    def _():
        for i, (bi, p) in enumerate(tiles):
            sT_ref[bi, p] = s_ref[i]


def rwkv_scan(r, w, k, kk, bb, v, s0, nb, tc):
    b, t, _ = r.shape
    f32 = jnp.float32
    pair = lambda s: s.reshape(b, N_PAIR, 2, D_HEAD, D_HEAD).transpose(0, 1, 3, 2, 4).reshape(b, N_PAIR, D_HEAD, LANES)
    unpair = lambda s: s.reshape(b, N_PAIR, D_HEAD, 2, D_HEAD).transpose(0, 1, 3, 2, 4).reshape(b, H_RWKV, D_HEAD, D_HEAD)
    seq_spec = pl.BlockSpec((nb, tc, W_RWKV), lambda bi, ci: (bi, ci, 0))
    st_spec = pl.BlockSpec((nb, N_PAIR, D_HEAD, LANES), lambda bi, ci: (bi, 0, 0, 0))
    y, s_fin = pl.pallas_call(
        _rwkv_scan_kernel,
        grid=(b // nb, t // tc),
        in_specs=[seq_spec] * 6 + [st_spec],
        out_specs=[seq_spec, st_spec],
        out_shape=[jax.ShapeDtypeStruct((b, t, W_RWKV), f32), jax.ShapeDtypeStruct((b, N_PAIR, D_HEAD, LANES), f32)],
        scratch_shapes=[pltpu.VMEM((nb * N_PAIR, D_HEAD, LANES), f32)],
        compiler_params=pltpu.CompilerParams(dimension_semantics=("parallel", "arbitrary"),
                                             vmem_limit_bytes=VMEM_LIMIT),
        name="rwkv_scan",
    )(r, w, k, kk, bb, v, pair(s0.astype(f32)))
    return y, unpair(s_fin)


def _merge_kernel(y_ref, bonus_ref, g_ref, gng_ref, gnb_ref, seg_ref, a_ref, x_ref, wr_ref, wa_ref, nf_ref, h_o, xn_o):
    y = y_ref[...]
    inv = 1.0 / D_HEAD
    mean = jnp.dot(y, seg_ref[...], precision=HI, preferred_element_type=jnp.float32) * inv
    dlt = y - mean
    var = jnp.dot(dlt * dlt, seg_ref[...], precision=HI, preferred_element_type=jnp.float32) * inv
    yn = dlt * lax.rsqrt(var + GN_EPS) * gng_ref[...] + gnb_ref[...]
    r_out = (yn + bonus_ref[...]) * g_ref[...]
    h = (x_ref[...] + jnp.dot(r_out.astype(jnp.bfloat16), wr_ref[...], preferred_element_type=jnp.float32)
         + jnp.dot(a_ref[...].astype(jnp.bfloat16), wa_ref[...], preferred_element_type=jnp.float32))
    h_o[...] = h
    xn_o[...] = _rms_norm(h, nf_ref[...]).astype(jnp.bfloat16)


def merge(y, bonus, g, gn_g, gn_b, a_out, x, w_out_bf16, norm_ffn, tt):
    n, d = x.shape
    head = jnp.arange(W_RWKV) // D_HEAD
    seg = (head[:, None] == head[None, :]).astype(jnp.float32)
    row = lambda w: pl.BlockSpec((tt, w), lambda i: (i, 0))
    vec = lambda w: pl.BlockSpec((1, w), lambda i: (0, 0))
    mat = lambda r, c: pl.BlockSpec((r, c), lambda i: (0, 0))
    return pl.pallas_call(
        _merge_kernel,
        grid=(n // tt,),
        in_specs=[row(W_RWKV), row(W_RWKV), row(W_RWKV), vec(W_RWKV), vec(W_RWKV), mat(W_RWKV, W_RWKV),
                  row(W_ATT), row(d), mat(W_RWKV, d), mat(W_ATT, d), vec(d)],
        out_specs=[row(d), row(d)],
        out_shape=[jax.ShapeDtypeStruct((n, d), jnp.float32), jax.ShapeDtypeStruct((n, d), jnp.bfloat16)],
        compiler_params=pltpu.CompilerParams(dimension_semantics=("parallel",), vmem_limit_bytes=VMEM_LIMIT),
        name="merge",
    )(y, bonus, g, gn_g.reshape(1, -1), gn_b.reshape(1, -1), seg, a_out, x, w_out_bf16[:W_RWKV], w_out_bf16[W_RWKV:],
      norm_ffn.reshape(1, d))


def _moba_block_kernel(q_ref, k_ref, v_ref, o_ref, *, qi):
    scale = D_HEAD ** -0.5
    nb = qi + 1
    k_all = k_ref[0]
    kb = k_all.astype(jnp.bfloat16)
    vt = v_ref[0].T.astype(jnp.bfloat16)
    q2 = q_ref[0]
    lane = lax.broadcasted_iota(jnp.int32, q2.shape, 1)
    key_row = lax.broadcasted_iota(jnp.int32, (MOBA_BLOCK, MOBA_BLOCK), 0)
    qry_col = lax.broadcasted_iota(jnp.int32, (MOBA_BLOCK, MOBA_BLOCK), 1)
    row = lax.broadcasted_iota(jnp.int32, (LANES, MOBA_BLOCK), 0)
    blk = lambda z, j: z[j * MOBA_BLOCK:(j + 1) * MOBA_BLOCK]
    outs = []
    for s in range(2):
        qm = jnp.where((lane // D_HEAD) == s, q2, 0.0)
        qmb = qm.astype(jnp.bfloat16)
        picked = [None] * qi
        if qi > MOBA_TOPK:
            kmean = jnp.concatenate([jnp.mean(blk(k_all, j), axis=0, keepdims=True) for j in range(qi)], axis=0)
            gate = lax.dot_general(kmean, qm, NT_DIMS, precision=HI, preferred_element_type=jnp.float32)
            for j in range(qi):
                ahead = jnp.zeros((1, MOBA_BLOCK), jnp.float32)
                for jp in range(qi):
                    if jp != j:
                        beats = (gate[jp:jp + 1] >= gate[j:j + 1]) if jp < j else (gate[jp:jp + 1] > gate[j:j + 1])
                        ahead = ahead + jnp.where(beats, 1.0, 0.0)
                picked[j] = ahead < float(MOBA_TOPK)
        scores = []
        for j in range(nb):
            st = lax.dot_general(blk(kb, j), qmb, NT_DIMS, preferred_element_type=jnp.float32) * scale
            if j == qi:
                st = jnp.where(key_row <= qry_col, st, NEG_INF)
            elif picked[j] is not None:
                st = jnp.where(picked[j], st, NEG_INF)
            scores.append(st)
        m = functools.reduce(jnp.maximum, [jnp.max(st, axis=0, keepdims=True) for st in scores])
        ps = [jnp.exp(st - m) for st in scores]
        denom = functools.reduce(jnp.add, [jnp.sum(p, axis=0, keepdims=True) for p in ps])
        acc = functools.reduce(jnp.add, [
            jnp.dot(vt[:, j * MOBA_BLOCK:(j + 1) * MOBA_BLOCK], ps[j].astype(jnp.bfloat16),
                    preferred_element_type=jnp.float32) for j in range(nb)])
        outs.append(acc / denom)
    o_ref[0] = jnp.where((row // D_HEAD) == 0, outs[0], outs[1]).T


def moba_prompt(q, k, v):
    b, t, w = q.shape
    outs = []
    for qi in range(t // MOBA_BLOCK):
        n_keys = (qi + 1) * MOBA_BLOCK
        outs.append(pl.pallas_call(
            functools.partial(_moba_block_kernel, qi=qi),
            grid=(b, w // LANES),
            in_specs=[
                pl.BlockSpec((1, MOBA_BLOCK, LANES), lambda bi, hp, qi=qi: (bi, qi, hp)),
                pl.BlockSpec((1, n_keys, LANES), lambda bi, hp: (bi, 0, hp)),
                pl.BlockSpec((1, n_keys, LANES), lambda bi, hp: (bi, 0, hp)),
            ],
            out_specs=pl.BlockSpec((1, MOBA_BLOCK, LANES), lambda bi, hp: (bi, 0, hp)),
            out_shape=jax.ShapeDtypeStruct((b, MOBA_BLOCK, w), jnp.float32),
            compiler_params=pltpu.CompilerParams(dimension_semantics=("parallel", "parallel"),
                                                 vmem_limit_bytes=VMEM_LIMIT),
            name=f"moba_prompt_q{qi}",
        )(q, k, v))
    return jnp.concatenate(outs, axis=1)


PAGES_PER_BLOCK = MOBA_BLOCK // PAGE_SIZE


def _sample_select_kernel(q_ref, km_ref, idx_ref):
    gate = jnp.sum(km_ref[0] * q_ref[...], axis=-1, keepdims=True)
    nb = gate.shape[0]
    blocks = lax.broadcasted_iota(jnp.int32, gate.shape, 0).astype(jnp.float32)
    for i in range(MOBA_TOPK):
        m = jnp.max(gate, axis=0, keepdims=True)
        first = jnp.min(jnp.where(gate == m, blocks, float(nb)), axis=0, keepdims=True)
        gate = jnp.where(blocks == first, NEG_INF, gate)
        idx_ref[0, i] = first[0]


def sample_select(q, kmean):
    bd, nb, h, dh = kmean.shape
    idx = pl.pallas_call(
        _sample_select_kernel,
        grid=(bd,),
        in_specs=[pl.BlockSpec((1, h, dh), lambda b: (b, 0, 0)), pl.BlockSpec((1, nb, h, dh), lambda b: (b, 0, 0, 0))],
        out_specs=pl.BlockSpec((1, MOBA_TOPK, h, 1), lambda b: (b, 0, 0, 0)),
        out_shape=jax.ShapeDtypeStruct((bd, MOBA_TOPK, h, 1), jnp.float32),
        compiler_params=pltpu.CompilerParams(dimension_semantics=("parallel",), vmem_limit_bytes=VMEM_LIMIT),
        name="sample_select",
    )(q, kmean)
    return idx[..., 0].astype(jnp.int32).transpose(0, 2, 1)


def _sample_attend_kernel(q_ref, ko_ref, vo_ref, ks_ref, vs_ref, o_ref):
    scale = D_HEAD ** -0.5
    for h in range(H_ATT):
        qh = q_ref[0, h:h + 1, :]
        s_sel = jnp.sum(ks_ref[0, h] * qh, axis=1, keepdims=True) * scale
        s_own = jnp.sum(ko_ref[0, h:h + 1, :] * qh, axis=1, keepdims=True) * scale
        m = jnp.maximum(jnp.max(s_sel, axis=0, keepdims=True), s_own)
        p_sel = jnp.exp(s_sel - m)
        p_own = jnp.exp(s_own - m)
        denom = jnp.sum(p_sel, axis=0, keepdims=True) + p_own
        acc = jnp.sum(p_sel * vs_ref[0, h], axis=0, keepdims=True) + p_own * vo_ref[0, h:h + 1, :]
        o_ref[0, h:h + 1, :] = acc / denom


def sample_attend(q, k_own, v_own, k_sel, v_sel):
    bd, h, n_keys, dh = k_sel.shape
    tok = pl.BlockSpec((1, h, dh), lambda b: (b, 0, 0))
    sel = pl.BlockSpec((1, h, n_keys, dh), lambda b: (b, 0, 0, 0))
    return pl.pallas_call(
        _sample_attend_kernel,
        grid=(bd,),
        in_specs=[tok, tok, tok, sel, sel],
        out_specs=tok,
        out_shape=jax.ShapeDtypeStruct((bd, h, dh), jnp.float32),
        compiler_params=pltpu.CompilerParams(dimension_semantics=("parallel",), vmem_limit_bytes=VMEM_LIMIT),
        name="sample_attend",
    )(q, k_own, v_own, k_sel, v_sel)


def cached_block_kmeans(cache_k, page_table):
    bd, n_pages = page_table.shape
    blocks = cache_k[page_table].reshape(bd, n_pages // PAGES_PER_BLOCK, MOBA_BLOCK, H_ATT, D_HEAD)
    return jnp.mean(blocks, axis=2)


def moba_sample(q, k, v, kmean, cache_k, cache_v, page_table):
    bd, s, w = q.shape
    assert s == 1 and PAST_LEN % MOBA_BLOCK == 0 and PAST_LEN // MOBA_BLOCK >= MOBA_TOPK
    heads = lambda z: z.reshape(bd, H_ATT, D_HEAD)
    sel = sample_select(heads(q), kmean)
    logical = sel[..., None] * PAGES_PER_BLOCK + jnp.arange(PAGES_PER_BLOCK)
    phys = page_table[jnp.arange(bd)[:, None, None, None], logical]
    hi = jnp.arange(H_ATT)[None, :, None, None]
    gather = lambda c: c[phys, :, hi, :].reshape(bd, H_ATT, MOBA_TOPK * MOBA_BLOCK, D_HEAD)
    out = sample_attend(heads(q), heads(k), heads(v), gather(cache_k), gather(cache_v))
    return out.reshape(bd, 1, w)


PEER_CANDS = [(a, b) for a in range(PEER_TOPK) for b in range(PEER_TOPK) if (a + 1) * (b + 1) <= PEER_TOPK]
PEER_NCAND = -(-len(PEER_CANDS) // 8) * 8


def _extract_topk(x, n_rows, k, exact):
    rows = lax.broadcasted_iota(jnp.int32, x.shape, 0).astype(jnp.float32)
    vals = []
    for _ in range(k):
        m = jnp.max(x, axis=0, keepdims=True)
        if exact:
            first = jnp.min(jnp.where(x == m, rows, float(n_rows)), axis=0, keepdims=True)
            x = jnp.where(rows == first, NEG_INF, x)
        else:
            x = jnp.where(x == m, NEG_INF, x)
        vals.append(m)
    return x, vals


def _peer_route_tables(q_ref, keys_ref, t1_ref, s2_ref, e1_ref, e2_ref, sv_ref, comb_ref, exact):
    half = PEER_DQ // 2
    tm = q_ref.shape[0]
    n_pad = PEER_NCAND - len(PEER_CANDS)
    ties = jnp.zeros((1, tm), jnp.float32)
    removed = lambda rem: jnp.sum(jnp.where(rem == NEG_INF, 1.0, 0.0), axis=0, keepdims=True)
    for h in range(PEER_HEADS):
        masked = []
        for p in range(2):
            qs = q_ref[:, (2 * h + p) * half:(2 * h + p + 1) * half].astype(jnp.bfloat16)
            st = lax.dot_general(keys_ref[p], qs, NT_DIMS, preferred_element_type=jnp.float32)
            rem, vals = _extract_topk(st, PEER_NKEYS, PEER_TOPK, exact)
            if not exact:
                ties = ties + (removed(rem) - float(PEER_TOPK))
            for i, v in enumerate(vals):
                sv_ref[p, i:i + 1, :] = v
            masked.append(jnp.where(rem == NEG_INF, st, NEG_INF))
        comb_ref[...] = jnp.full(comb_ref.shape, NEG_INF, jnp.float32)
        for c, (a, b) in enumerate(PEER_CANDS):
            comb_ref[c:c + 1, :] = sv_ref[0, a:a + 1, :] + sv_ref[1, b:b + 1, :]
        rem, cvals = _extract_topk(comb_ref[...], PEER_NCAND, PEER_TOPK + 1, exact)
        if not exact:
            ties = ties + (removed(rem) - float(PEER_TOPK + 1 + n_pad))
        cmax = cvals[0]
        z = jnp.zeros_like(cmax)
        for v in cvals[:PEER_TOPK]:
            z = z + jnp.exp(v - cmax)
        cut = 0.5 * (cvals[PEER_TOPK - 1] + cvals[PEER_TOPK])
        t1_ref[h] = cut - masked[0]
        s2_ref[h] = masked[1]
        e1_ref[h] = jnp.exp(masked[0] - sv_ref[0, 0:1, :])
        e2_ref[h] = jnp.exp(masked[1] - sv_ref[1, 0:1, :]) / z
    return ties


def _peer_route_kernel(xn_ref, wpq_ref, keys_ref, t1_ref, s2_ref, e1_ref, e2_ref, q_ref, sv_ref, comb_ref):
    q_ref[...] = jnp.dot(xn_ref[...], wpq_ref[...], preferred_element_type=jnp.float32)
    tables = functools.partial(_peer_route_tables, q_ref, keys_ref, t1_ref, s2_ref, e1_ref, e2_ref, sv_ref, comb_ref)
    ties = tables(exact=False)

    @pl.when(jnp.max(ties) > 0.0)
    def _():
        tables(exact=True)


def peer_route(xn_bf16, wpq_bf16, keys_bf16, tm):
    n = xn_bf16.shape[0]
    tab = jax.ShapeDtypeStruct((PEER_HEADS, PEER_NKEYS, n), jnp.float32)
    tab_spec = pl.BlockSpec((PEER_HEADS, PEER_NKEYS, tm), lambda i: (0, 0, i))
    return pl.pallas_call(
        _peer_route_kernel,
        grid=(n // tm,),
        in_specs=[
            pl.BlockSpec((tm, D_MODEL), lambda i: (i, 0)),
            pl.BlockSpec((D_MODEL, PEER_HEADS * PEER_DQ), lambda i: (0, 0)),
            pl.BlockSpec((2, PEER_NKEYS, PEER_DQ // 2), lambda i: (0, 0, 0)),
        ],
        out_specs=[tab_spec] * 4,
        out_shape=[tab] * 4,
        scratch_shapes=[pltpu.VMEM((tm, PEER_HEADS * PEER_DQ), jnp.float32),
                        pltpu.VMEM((2, PEER_TOPK, tm), jnp.float32),
                        pltpu.VMEM((PEER_NCAND, tm), jnp.float32)],
        compiler_params=pltpu.CompilerParams(dimension_semantics=("parallel",),
                                             vmem_limit_bytes=VMEM_LIMIT),
        name="peer_route",
    )(xn_bf16, wpq_bf16, keys_bf16)


def _gelu_exact(x):
    return 0.5 * x * (1.0 + lax.erf(x * (1.0 / math.sqrt(2.0))))


def _peer_dense_kernel(xn_ref, u_ref, vt_ref, t1_ref, s2_ref, e1_ref, e2_ref, h_ref, nfin_ref, o_ref, acc_ref, p_ref):
    j = pl.program_id(1)
    n_e1 = u_ref.shape[0] // PEER_NKEYS
    tm = xn_ref.shape[0]

    @pl.when(j == 0)
    def _():
        acc_ref[...] = jnp.zeros_like(acc_ref)

    for eb in range(n_e1 // PEER_ELB):
        erows = slice(eb * PEER_ELB * PEER_NKEYS, (eb + 1) * PEER_ELB * PEER_NKEYS)
        at = lax.dot_general(u_ref[erows, :], xn_ref[...], NT_DIMS,
                             preferred_element_type=jnp.float32)
        for c in range(tm // LANES):
            cols = slice(c * LANES, (c + 1) * LANES)
            for r0 in range(0, PEER_NKEYS, PEER_SGR):
                accs = [jnp.zeros((PEER_SGR, LANES), jnp.float32) for _ in range(PEER_ELB)]
                for h in range(PEER_HEADS):
                    s2t = s2_ref[h, r0:r0 + PEER_SGR, cols]
                    e2t = e2_ref[h, r0:r0 + PEER_SGR, cols]
                    for i in range(PEER_ELB):
                        el = eb * PEER_ELB + i
                        picked = s2t >= t1_ref[h, el:el + 1, cols]
                        accs[i] = accs[i] + jnp.where(picked, e2t, 0.0) * e1_ref[h, el:el + 1, cols]
                for i in range(PEER_ELB):
                    lo = i * PEER_NKEYS + r0
                    act = _gelu_exact(at[lo:lo + PEER_SGR, cols])
                    p_ref[eb, lo:lo + PEER_SGR, cols] = (accs[i] * act).astype(jnp.bfloat16)
        acc_ref[...] += jnp.dot(vt_ref[:, erows], p_ref[eb], preferred_element_type=jnp.float32)

    @pl.when(j == pl.num_programs(1) - 1)
    def _():
        o_ref[...] = _rms_norm(h_ref[...] + acc_ref[...].T, nfin_ref[...])


def peer_dense(xn_bf16, u_bf16, vt_bf16, t1, s2, e1, e2, h, norm_final, tm, te):
    n = xn_bf16.shape[0]
    n_e1 = te // PEER_NKEYS
    return pl.pallas_call(
        _peer_dense_kernel,
        grid=(n // tm, PEER_EXPERTS // te),
        in_specs=[
            pl.BlockSpec((tm, D_MODEL), lambda i, j: (i, 0)),
            pl.BlockSpec((te, D_MODEL), lambda i, j: (j, 0)),
            pl.BlockSpec((D_MODEL, te), lambda i, j: (0, j)),
            pl.BlockSpec((PEER_HEADS, n_e1, tm), lambda i, j: (0, j, i)),
            pl.BlockSpec((PEER_HEADS, PEER_NKEYS, tm), lambda i, j: (0, 0, i)),
            pl.BlockSpec((PEER_HEADS, n_e1, tm), lambda i, j: (0, j, i)),
            pl.BlockSpec((PEER_HEADS, PEER_NKEYS, tm), lambda i, j: (0, 0, i)),
            pl.BlockSpec((tm, D_MODEL), lambda i, j: (i, 0)),
            pl.BlockSpec((1, D_MODEL), lambda i, j: (0, 0)),
        ],
        out_specs=pl.BlockSpec((tm, D_MODEL), lambda i, j: (i, 0)),
        out_shape=jax.ShapeDtypeStruct((n, D_MODEL), jnp.float32),
        scratch_shapes=[pltpu.VMEM((D_MODEL, tm), jnp.float32),
                        pltpu.VMEM((n_e1 // PEER_ELB, PEER_ELB * PEER_NKEYS, tm), jnp.bfloat16)],
        compiler_params=pltpu.CompilerParams(dimension_semantics=("parallel", "arbitrary"),
                                             vmem_limit_bytes=VMEM_LIMIT),
        name="peer_dense",
    )(xn_bf16, u_bf16, vt_bf16, t1, s2, e1, e2, h, norm_final.reshape(1, D_MODEL))


def peer_block(h, xn_bf16, wpq_bf16, keys_bf16, u_bf16, vt_bf16, norm_final):
    n = h.shape[0]
    tm = PEER_TM if n >= PEER_TM else PEER_TM_SMALL
    pad = -n % tm
    hp, xb = jnp.pad(h, ((0, pad), (0, 0))), jnp.pad(xn_bf16, ((0, pad), (0, 0)))
    t1, s2, e1, e2 = peer_route(xb, wpq_bf16, keys_bf16, PEER_TM_SMALL)
    return peer_dense(xb, u_bf16, vt_bf16, t1, s2, e1, e2, hp, norm_final, tm, PEER_TE)[:n]


def layer(x, pos, prev0, s0, attend, norm_mix, w_in_bf16, rw, w_out_bf16, norm_ffn, peer, norm_final):
    mu, w0, w_dec, a0, w_aaa, w_gate, k_k, k_a, r_k, gn_g, gn_b = rw
    b, t, d = x.shape
    n = b * t
    if t == 1:
        xr, posr = x.reshape(1, b, d), jnp.broadcast_to(pos, (b,))
    else:
        xr, posr = x, pos
    seq = lambda z: z.reshape(b, t, z.shape[-1])
    pr, q, k, v = map(seq, project(xr, norm_mix, w_in_bf16, posr, min(xr.shape[1], PROJ_TT)))
    r, w, k2, kk, bb, vv, g, bonus = rwkv_prep(pr, prev0, mu, w0, w_dec, a0, w_aaa, w_gate, k_k, k_a, r_k, min(t, 512))
    y, s_new = rwkv_scan(r, w, k2, kk, bb, vv, s0, RWKV_NB, min(t, 128))
    a_out = attend(q, k, v)
    flat = lambda z: z.reshape(n, z.shape[-1])
    h, xn = merge(flat(y), flat(bonus), flat(g), gn_g, gn_b, flat(a_out), flat(x), w_out_bf16, norm_ffn, min(n, 512))
    out = peer_block(h, xn, *peer, norm_final)
    heads = lambda z: z.reshape(b, t, H_ATT, D_HEAD)
    return out.reshape(b, t, d), heads(k), heads(v), s_new, pr[:, -1]


def kernel(x_prompt, x_sample, cache_k, cache_v, page_table, state_wkv, state_shift, norm_mix, w_in, mu_shift, w0, w_decay_up, a0, w_aaa_up, w_gate_up, k_k, k_a, r_k, gn_gain, gn_bias, w_out, norm_ffn, w_pq, peer_sub_keys, expert_u, expert_v, norm_final):
    l = 0
    bf16 = jnp.bfloat16
    rw = (mu_shift[l], w0[l], w_decay_up[l], a0[l], w_aaa_up[l], w_gate_up[l], k_k[l], k_a[l], r_k[l], gn_gain[l], gn_bias[l])
    peer = (w_pq[l].astype(bf16), peer_sub_keys[l].astype(bf16), expert_u[l].astype(bf16), expert_v[l].T.astype(bf16))
    shared = (norm_mix[l], w_in[l].astype(bf16), rw, w_out[l].astype(bf16), norm_ffn[l], peer, norm_final)
    bp, tp, _ = x_prompt.shape
    bs, ts, _ = x_sample.shape
    kmean = cached_block_kmeans(cache_k[l], page_table)
    y_p, k_p, v_p, s_p, sh_p = layer(
        x_prompt, jnp.arange(tp, dtype=jnp.int32), jnp.zeros((bp, W_RWKV_IN), jnp.float32),
        jnp.zeros((bp, H_RWKV, D_HEAD, D_HEAD), jnp.float32), moba_prompt, *shared)
    attend_cache = lambda q, k, v: moba_sample(q, k, v, kmean, cache_k[l], cache_v[l], page_table)
    y_s, k_s, v_s, s_s, sh_s = layer(
        x_sample, PAST_LEN + jnp.arange(ts, dtype=jnp.int32), state_shift[l], state_wkv[l], attend_cache, *shared)
    return (y_p, y_s, k_p[None], v_p[None], s_p[None], sh_p[None], k_s[None], v_s[None], s_s[None], sh_s[None])
```

```python
import functools
import math

import jax
import jax.numpy as jnp
from jax import lax
from jax.experimental import pallas as pl
from jax.experimental.pallas import tpu as pltpu

D_MODEL = 1024
DEPTH = 1
PAST_LEN = 16384
PAGE_SIZE = 128
D_HEAD = 64
H_RWKV = 8
H_ATT = 8
W_RWKV = H_RWKV * D_HEAD
W_ATT = H_ATT * D_HEAD
D_DECAY_LORA = 64
D_AAA_LORA = 64
D_GATE_LORA = 128
W_RWKV_IN = 3 * W_RWKV + D_DECAY_LORA + D_AAA_LORA + D_GATE_LORA
W_IN = W_RWKV_IN + 3 * W_ATT
GN_EPS = 64e-5
NORM_EPS = 1e-6
MOBA_BLOCK = 256
MOBA_TOPK = 3
Q_CHUNK = 64
ROT_DIM = D_HEAD // 4
ROPE_THETA = 500000.0
PEER_HEADS = 8
PEER_NKEYS = 128
PEER_TOPK = 16
PEER_DQ = 256
PEER_EXPERTS = PEER_NKEYS * PEER_NKEYS
PEER_TM = 512
PEER_TM_SMALL = 256
PEER_TE = 2048
PEER_ELB = 4
PEER_SGR = 32
NEG_INF = float("-inf")
VMEM_LIMIT = 56 * 1024 * 1024
LANES = 128
SUBLANES = 8
PROJ_TT = 512
RWKV_NB = 8
NT_DIMS = (((1,), (1,)), ((), ()))


def _rms_norm(x, g):
    return x * lax.rsqrt(jnp.mean(x * x, axis=-1, keepdims=True) + NORM_EPS) * g


def rope_tables(pos):
    half = ROT_DIM // 2
    inv = 1.0 / (ROPE_THETA ** (jnp.arange(half, dtype=jnp.float32) * 2.0 / ROT_DIM))
    ang = pos.astype(jnp.float32)[:, None] * inv[None, :]
    lane = jnp.arange(LANES) % D_HEAD
    idx = lane % half
    cos = jnp.where(lane[None, :] < ROT_DIM, jnp.cos(ang)[:, idx], 1.0)
    sin = jnp.sin(ang)[:, idx]
    sin_lo = jnp.where(lane[None, :] < half, -sin, 0.0)
    sin_hi = jnp.where((lane[None, :] >= half) & (lane[None, :] < ROT_DIM), sin, 0.0)
    return cos, sin_lo, sin_hi


def _project_kernel(x_ref, g_ref, w_ref, cos_ref, slo_ref, shi_ref, pr_o, q_o, k_o, v_o):
    xn = _rms_norm(x_ref[0], g_ref[...])
    p = jnp.dot(xn.astype(jnp.bfloat16), w_ref[...], preferred_element_type=jnp.float32)
    pr_o[0] = p[:, :W_RWKV_IN]
    reps = W_ATT // LANES
    cos = jnp.concatenate([cos_ref[...]] * reps, axis=1)
    slo = jnp.concatenate([slo_ref[...]] * reps, axis=1)
    shi = jnp.concatenate([shi_ref[...]] * reps, axis=1)
    half = ROT_DIM // 2

    def rope(z):
        ahead = pltpu.roll(z, W_ATT - half, 1)
        behind = pltpu.roll(z, half, 1)
        return z * cos + ahead * slo + behind * shi

    q_o[0] = rope(p[:, W_RWKV_IN:W_RWKV_IN + W_ATT])
    k_o[0] = rope(p[:, W_RWKV_IN + W_ATT:W_RWKV_IN + 2 * W_ATT])
    v_o[0] = p[:, W_RWKV_IN + 2 * W_ATT:]


def project(x, norm_g, w_in_bf16, pos, tt):
    b, t, d = x.shape
    cos, slo, shi = rope_tables(pos)
    f32 = jnp.float32
    tab_spec = pl.BlockSpec((tt, LANES), lambda bi, ti: (ti, 0))
    out_spec = lambda w: pl.BlockSpec((1, tt, w), lambda bi, ti: (bi, ti, 0))
    return pl.pallas_call(
        _project_kernel,
        grid=(b, t // tt),
        in_specs=[
            pl.BlockSpec((1, tt, d), lambda bi, ti: (bi, ti, 0)),
            pl.BlockSpec((1, d), lambda bi, ti: (0, 0)),
            pl.BlockSpec((d, W_IN), lambda bi, ti: (0, 0)),
            tab_spec, tab_spec, tab_spec,
        ],
        out_specs=[out_spec(W_RWKV_IN), out_spec(W_ATT), out_spec(W_ATT), out_spec(W_ATT)],
        out_shape=[jax.ShapeDtypeStruct((b, t, W_RWKV_IN), f32)] + [jax.ShapeDtypeStruct((b, t, W_ATT), f32)] * 3,
        compiler_params=pltpu.CompilerParams(dimension_semantics=("parallel", "parallel"),
                                             vmem_limit_bytes=VMEM_LIMIT),
        cost_estimate=pl.CostEstimate(flops=2 * b * t * d * W_IN, transcendentals=b * t,
                                      bytes_accessed=4 * b * t * (d + W_IN) + 2 * d * W_IN + 12 * t * LANES),
        name="project",
    )(x, norm_g.reshape(1, d), w_in_bf16, cos, slo, shi)


N_PAIR = W_RWKV // LANES
HI = lax.Precision.HIGHEST


def _sigmoid(x):
    return 1.0 / (1.0 + jnp.exp(-x))


def _softplus(x):
    return jnp.maximum(x, 0.0) + jnp.log(1.0 + jnp.exp(-jnp.abs(x)))


def _rwkv_prep_kernel(p_ref, prev0_ref, mu_ref, w0_ref, wdec_ref, a0_ref, waaa_ref, wgate_ref, kk_w_ref, ka_ref,
                      rk_ref, seg_ref, r_o, w_o, k_o, kk_o, b_o, v_o, g_o, bonus_o, carry_ref):
    ti = pl.program_id(1)

    @pl.when(ti == 0)
    def _():
        carry_ref[...] = prev0_ref[0]

    p = p_ref[0]
    row = lax.broadcasted_iota(jnp.int32, p.shape, 0)
    if p.shape[0] == 1:
        prev = carry_ref[...]
    else:
        prev = jnp.where(row == 0, carry_ref[...], pltpu.roll(p, 1, 0))
    carry_ref[...] = p[p.shape[0] - 1:, :]
    ps = p + (prev - p) * mu_ref[...]
    r = ps[:, 0:W_RWKV]
    k = ps[:, W_RWKV:2 * W_RWKV]
    v = ps[:, 2 * W_RWKV:3 * W_RWKV]
    lora = ps[:, 3 * W_RWKV:3 * W_RWKV + LANES]
    gd = ps[:, 3 * W_RWKV + LANES:]
    dec = jnp.dot(jnp.tanh(lora).astype(jnp.bfloat16), wdec_ref[...], preferred_element_type=jnp.float32)
    logw = -jnp.exp(-_softplus(-(w0_ref[...] + dec)) - 0.5)
    a = _sigmoid(a0_ref[...] + jnp.dot(lora.astype(jnp.bfloat16), waaa_ref[...], preferred_element_type=jnp.float32))
    g = jnp.dot(_sigmoid(gd).astype(jnp.bfloat16), wgate_ref[...], preferred_element_type=jnp.float32)
    kk = k * kk_w_ref[...]
    sumsq = jnp.dot(kk * kk, seg_ref[...], precision=HI, preferred_element_type=jnp.float32)
    kk = kk / jnp.maximum(jnp.sqrt(sumsq), 1e-12)
    k2 = k * (1.0 + (a - 1.0) * ka_ref[...])
    rkk = jnp.dot(r * k2 * rk_ref[...], seg_ref[...], precision=HI, preferred_element_type=jnp.float32)
    r_o[0] = r
    w_o[0] = jnp.exp(logw)
    k_o[0] = k2
    kk_o[0] = kk
    b_o[0] = kk * a
    v_o[0] = v
    g_o[0] = g
    bonus_o[0] = rkk * v


def rwkv_prep(p, prev0, mu, w0, w_dec, a0, w_aaa, w_gate, k_k, k_a, r_k, tt):
    b, t, _ = p.shape
    f32 = jnp.float32
    wdec_pad = jnp.concatenate([w_dec, jnp.zeros_like(w_aaa)], axis=0).astype(jnp.bfloat16)
    waaa_pad = jnp.concatenate([jnp.zeros_like(w_dec), w_aaa], axis=0).astype(jnp.bfloat16)
    head = jnp.arange(W_RWKV) // D_HEAD
    seg = (head[:, None] == head[None, :]).astype(f32)
    row = lambda z: z.reshape(1, -1).astype(f32)
    vec_spec = lambda n: pl.BlockSpec((1, n), lambda bi, ti: (0, 0))
    mat_spec = lambda m, n: pl.BlockSpec((m, n), lambda bi, ti: (0, 0))
    out_spec = pl.BlockSpec((1, tt, W_RWKV), lambda bi, ti: (bi, ti, 0))
    out = jax.ShapeDtypeStruct((b, t, W_RWKV), f32)
    return pl.pallas_call(
        _rwkv_prep_kernel,
        grid=(b, t // tt),
        in_specs=[
            pl.BlockSpec((1, tt, W_RWKV_IN), lambda bi, ti: (bi, ti, 0)),
            pl.BlockSpec((1, 1, W_RWKV_IN), lambda bi, ti: (bi, 0, 0)),
            vec_spec(W_RWKV_IN), vec_spec(W_RWKV), mat_spec(LANES, W_RWKV), vec_spec(W_RWKV),
            mat_spec(LANES, W_RWKV), mat_spec(D_GATE_LORA, W_RWKV), vec_spec(W_RWKV), vec_spec(W_RWKV),
            vec_spec(W_RWKV), mat_spec(W_RWKV, W_RWKV),
        ],
        out_specs=[out_spec] * 8,
        out_shape=[out] * 8,
        scratch_shapes=[pltpu.VMEM((1, W_RWKV_IN), f32)],
        compiler_params=pltpu.CompilerParams(dimension_semantics=("parallel", "arbitrary"),
                                             vmem_limit_bytes=VMEM_LIMIT),
        name="rwkv_prep",
    )(p, prev0.reshape(b, 1, W_RWKV_IN), row(mu), row(w0), wdec_pad, row(a0), waaa_pad, w_gate.astype(jnp.bfloat16),
      row(k_k), row(k_a), row(r_k), seg)


def _rwkv_scan_kernel(r_ref, w_ref, k_ref, kk_ref, b_ref, v_ref, s0_ref, y_ref, sT_ref, s_ref):
    ci = pl.program_id(1)
    nb, tc = r_ref.shape[0], r_ref.shape[1]
    tiles = [(bi, p) for bi in range(nb) for p in range(N_PAIR)]

    @pl.when(ci == 0)
    def _():
        for i, (bi, p) in enumerate(tiles):
            s_ref[i] = s0_ref[bi, p]

    lane = lax.broadcasted_iota(jnp.int32, (D_HEAD, LANES), 1)
    sub = lax.broadcasted_iota(jnp.int32, (D_HEAD, LANES), 0)
    first = lane < D_HEAD
    eye2 = (sub == (lane % D_HEAD))
    sub128 = lax.broadcasted_iota(jnp.int32, (LANES, LANES), 0)
    lane128 = lax.broadcasted_iota(jnp.int32, (LANES, LANES), 1)
    eye128 = sub128 == lane128
    seg = jnp.where((sub128 // D_HEAD) == (lane128 // D_HEAD), 1.0, 0.0).astype(jnp.bfloat16)

    def head_sums(xs):
        parts = []
        for x in xs:
            hi = x.astype(jnp.bfloat16)
            parts += [hi, (x - hi.astype(jnp.float32)).astype(jnp.bfloat16)]
        both = jnp.dot(jnp.concatenate(parts, axis=0), seg, preferred_element_type=jnp.float32)
        return [both[i * LANES:i * LANES + D_HEAD] + both[i * LANES + D_HEAD:(i + 1) * LANES] for i in range(len(xs))]

    grp = min(tc, SUBLANES)

    def steps(gi, carry):
        rows = pl.ds(pl.multiple_of(gi * grp, grp), grp)
        lanes = lambda p: slice(p * LANES, (p + 1) * LANES)
        load = lambda ref: [ref[bi, rows, lanes(p)] for bi, p in tiles]
        r_g, w_g, k_g, kk_g, b_g, v_g = load(r_ref), load(w_ref), load(k_ref), load(kk_ref), load(b_ref), load(v_ref)
        states = [s_ref[i] for i in range(len(tiles))]
        yrows = [[] for _ in tiles]
        for j in range(grp):
            row = slice(j, j + 1)
            sks = head_sums([s * kk[row] for s, kk in zip(states, kk_g)])
            for i in range(len(tiles)):
                vcol = jnp.sum(jnp.where(eye128, v_g[i][row], 0.0), axis=1, keepdims=True)
                vmat = jnp.where(first, vcol[0:D_HEAD], vcol[D_HEAD:LANES])
                states[i] = states[i] * w_g[i][row] - sks[i] * b_g[i][row] + vmat * k_g[i][row]
            ys = head_sums([s * r[row] for s, r in zip(states, r_g)])
            for i in range(len(tiles)):
                yrows[i].append(jnp.sum(jnp.where(eye2, ys[i], 0.0), axis=0, keepdims=True))
        for i, (bi, p) in enumerate(tiles):
            y_ref[bi, rows, lanes(p)] = yrows[i][0] if grp == 1 else jnp.concatenate(yrows[i], axis=0)
            s_ref[i] = states[i]
        return carry

    lax.fori_loop(0, tc // grp, steps, 0)

    @pl.when(ci == pl.num_programs(1) - 1)
    def _():
        for i, (bi, p) in enumerate(tiles):
            sT_ref[bi, p] = s_ref[i]


def rwkv_scan(r, w, k, kk, bb, v, s0, nb, tc):
    b, t, _ = r.shape
    f32 = jnp.float32
    pair = lambda s: s.reshape(b, N_PAIR, 2, D_HEAD, D_HEAD).transpose(0, 1, 3, 2, 4).reshape(b, N_PAIR, D_HEAD, LANES)
    unpair = lambda s: s.reshape(b, N_PAIR, D_HEAD, 2, D_HEAD).transpose(0, 1, 3, 2, 4).reshape(b, H_RWKV, D_HEAD, D_HEAD)
    seq_spec = pl.BlockSpec((nb, tc, W_RWKV), lambda bi, ci: (bi, ci, 0))
    st_spec = pl.BlockSpec((nb, N_PAIR, D_HEAD, LANES), lambda bi, ci: (bi, 0, 0, 0))
    y, s_fin = pl.pallas_call(
        _rwkv_scan_kernel,
        grid=(b // nb, t // tc),
        in_specs=[seq_spec] * 6 + [st_spec],
        out_specs=[seq_spec, st_spec],
        out_shape=[jax.ShapeDtypeStruct((b, t, W_RWKV), f32), jax.ShapeDtypeStruct((b, N_PAIR, D_HEAD, LANES), f32)],
        scratch_shapes=[pltpu.VMEM((nb * N_PAIR, D_HEAD, LANES), f32)],
        compiler_params=pltpu.CompilerParams(dimension_semantics=("parallel", "arbitrary"),
                                             vmem_limit_bytes=VMEM_LIMIT),
        cost_estimate=pl.CostEstimate(flops=b * t * N_PAIR * (4 * LANES ** 3 + 20 * D_HEAD * LANES), transcendentals=0,
                                      bytes_accessed=4 * (7 * b * t * W_RWKV + 2 * b * W_RWKV * D_HEAD)),
        name="rwkv_scan",
    )(r, w, k, kk, bb, v, pair(s0.astype(f32)))
    return y, unpair(s_fin)


def _merge_kernel(y_ref, bonus_ref, g_ref, gng_ref, gnb_ref, seg_ref, a_ref, x_ref, wr_ref, wa_ref, nf_ref, h_o, xn_o):
    y = y_ref[...]
    inv = 1.0 / D_HEAD
    mean = jnp.dot(y, seg_ref[...], precision=HI, preferred_element_type=jnp.float32) * inv
    dlt = y - mean
    var = jnp.dot(dlt * dlt, seg_ref[...], precision=HI, preferred_element_type=jnp.float32) * inv
    yn = dlt * lax.rsqrt(var + GN_EPS) * gng_ref[...] + gnb_ref[...]
    r_out = (yn + bonus_ref[...]) * g_ref[...]
    h = (x_ref[...] + jnp.dot(r_out.astype(jnp.bfloat16), wr_ref[...], preferred_element_type=jnp.float32)
         + jnp.dot(a_ref[...].astype(jnp.bfloat16), wa_ref[...], preferred_element_type=jnp.float32))
    h_o[...] = h
    xn_o[...] = _rms_norm(h, nf_ref[...]).astype(jnp.bfloat16)


def merge(y, bonus, g, gn_g, gn_b, a_out, x, w_out_bf16, norm_ffn, tt):
    n, d = x.shape
    head = jnp.arange(W_RWKV) // D_HEAD
    seg = (head[:, None] == head[None, :]).astype(jnp.float32)
    row = lambda w: pl.BlockSpec((tt, w), lambda i: (i, 0))
    vec = lambda w: pl.BlockSpec((1, w), lambda i: (0, 0))
    mat = lambda r, c: pl.BlockSpec((r, c), lambda i: (0, 0))
    return pl.pallas_call(
        _merge_kernel,
        grid=(n // tt,),
        in_specs=[row(W_RWKV), row(W_RWKV), row(W_RWKV), vec(W_RWKV), vec(W_RWKV), mat(W_RWKV, W_RWKV),
                  row(W_ATT), row(d), mat(W_RWKV, d), mat(W_ATT, d), vec(d)],
        out_specs=[row(d), row(d)],
        out_shape=[jax.ShapeDtypeStruct((n, d), jnp.float32), jax.ShapeDtypeStruct((n, d), jnp.bfloat16)],
        compiler_params=pltpu.CompilerParams(dimension_semantics=("parallel",), vmem_limit_bytes=VMEM_LIMIT),
        name="merge",
    )(y, bonus, g, gn_g.reshape(1, -1), gn_b.reshape(1, -1), seg, a_out, x, w_out_bf16[:W_RWKV], w_out_bf16[W_RWKV:],
      norm_ffn.reshape(1, d))


def _moba_block_kernel(q_ref, k_ref, v_ref, o_ref, *, qi):
    scale = D_HEAD ** -0.5
    nb = qi + 1
    k_all = k_ref[0]
    kb = k_all.astype(jnp.bfloat16)
    vt = v_ref[0].T.astype(jnp.bfloat16)
    q2 = q_ref[0]
    lane = lax.broadcasted_iota(jnp.int32, q2.shape, 1)
    key_row = lax.broadcasted_iota(jnp.int32, (MOBA_BLOCK, MOBA_BLOCK), 0)
    qry_col = lax.broadcasted_iota(jnp.int32, (MOBA_BLOCK, MOBA_BLOCK), 1)
    row = lax.broadcasted_iota(jnp.int32, (LANES, MOBA_BLOCK), 0)
    blk = lambda z, j: z[j * MOBA_BLOCK:(j + 1) * MOBA_BLOCK]
    outs = []
    for s in range(2):
        qm = jnp.where((lane // D_HEAD) == s, q2, 0.0)
        qmb = qm.astype(jnp.bfloat16)
        picked = [None] * qi
        if qi > MOBA_TOPK:
            kmean = jnp.concatenate([jnp.mean(blk(k_all, j), axis=0, keepdims=True) for j in range(qi)], axis=0)
            gate = lax.dot_general(kmean, qm, NT_DIMS, precision=HI, preferred_element_type=jnp.float32)
            for j in range(qi):
                ahead = jnp.zeros((1, MOBA_BLOCK), jnp.float32)
                for jp in range(qi):
                    if jp != j:
                        beats = (gate[jp:jp + 1] >= gate[j:j + 1]) if jp < j else (gate[jp:jp + 1] > gate[j:j + 1])
                        ahead = ahead + jnp.where(beats, 1.0, 0.0)
                picked[j] = ahead < float(MOBA_TOPK)
        scores = []
        for j in range(nb):
            st = lax.dot_general(blk(kb, j), qmb, NT_DIMS, preferred_element_type=jnp.float32) * scale
            if j == qi:
                st = jnp.where(key_row <= qry_col, st, NEG_INF)
            elif picked[j] is not None:
                st = jnp.where(picked[j], st, NEG_INF)
            scores.append(st)
        m = functools.reduce(jnp.maximum, [jnp.max(st, axis=0, keepdims=True) for st in scores])
        ps = [jnp.exp(st - m) for st in scores]
        denom = functools.reduce(jnp.add, [jnp.sum(p, axis=0, keepdims=True) for p in ps])
        acc = functools.reduce(jnp.add, [
            jnp.dot(vt[:, j * MOBA_BLOCK:(j + 1) * MOBA_BLOCK], ps[j].astype(jnp.bfloat16),
                    preferred_element_type=jnp.float32) for j in range(nb)])
        outs.append(acc / denom)
    o_ref[0] = jnp.where((row // D_HEAD) == 0, outs[0], outs[1]).T


def moba_prompt(q, k, v):
    b, t, w = q.shape
    outs = []
    for qi in range(t // MOBA_BLOCK):
        n_keys = (qi + 1) * MOBA_BLOCK
        outs.append(pl.pallas_call(
            functools.partial(_moba_block_kernel, qi=qi),
            grid=(b, w // LANES),
            in_specs=[
                pl.BlockSpec((1, MOBA_BLOCK, LANES), lambda bi, hp, qi=qi: (bi, qi, hp)),
                pl.BlockSpec((1, n_keys, LANES), lambda bi, hp: (bi, 0, hp)),
                pl.BlockSpec((1, n_keys, LANES), lambda bi, hp: (bi, 0, hp)),
            ],
            out_specs=pl.BlockSpec((1, MOBA_BLOCK, LANES), lambda bi, hp: (bi, 0, hp)),
            out_shape=jax.ShapeDtypeStruct((b, MOBA_BLOCK, w), jnp.float32),
            compiler_params=pltpu.CompilerParams(dimension_semantics=("parallel", "parallel"),
                                                 vmem_limit_bytes=VMEM_LIMIT),
            name=f"moba_prompt_q{qi}",
        )(q, k, v))
    return jnp.concatenate(outs, axis=1)


PAGES_PER_BLOCK = MOBA_BLOCK // PAGE_SIZE


def _sample_select_kernel(q_ref, km_ref, idx_ref):
    gate = jnp.sum(km_ref[0] * q_ref[...], axis=-1, keepdims=True)
    nb = gate.shape[0]
    blocks = lax.broadcasted_iota(jnp.int32, gate.shape, 0).astype(jnp.float32)
    for i in range(MOBA_TOPK):
        m = jnp.max(gate, axis=0, keepdims=True)
        first = jnp.min(jnp.where(gate == m, blocks, float(nb)), axis=0, keepdims=True)
        gate = jnp.where(blocks == first, NEG_INF, gate)
        idx_ref[0, i] = first[0]


def sample_select(q, kmean):
    bd, nb, h, dh = kmean.shape
    idx = pl.pallas_call(
        _sample_select_kernel,
        grid=(bd,),
        in_specs=[pl.BlockSpec((1, h, dh), lambda b: (b, 0, 0)), pl.BlockSpec((1, nb, h, dh), lambda b: (b, 0, 0, 0))],
        out_specs=pl.BlockSpec((1, MOBA_TOPK, h, 1), lambda b: (b, 0, 0, 0)),
        out_shape=jax.ShapeDtypeStruct((bd, MOBA_TOPK, h, 1), jnp.float32),
        compiler_params=pltpu.CompilerParams(dimension_semantics=("parallel",), vmem_limit_bytes=VMEM_LIMIT),
        name="sample_select",
    )(q, kmean)
    return idx[..., 0].astype(jnp.int32).transpose(0, 2, 1)


def _sample_attend_kernel(q_ref, ko_ref, vo_ref, ks_ref, vs_ref, o_ref):
    scale = D_HEAD ** -0.5
    for h in range(H_ATT):
        qh = q_ref[0, h:h + 1, :]
        s_sel = jnp.sum(ks_ref[0, h] * qh, axis=1, keepdims=True) * scale
        s_own = jnp.sum(ko_ref[0, h:h + 1, :] * qh, axis=1, keepdims=True) * scale
        m = jnp.maximum(jnp.max(s_sel, axis=0, keepdims=True), s_own)
        p_sel = jnp.exp(s_sel - m)
        p_own = jnp.exp(s_own - m)
        denom = jnp.sum(p_sel, axis=0, keepdims=True) + p_own
        acc = jnp.sum(p_sel * vs_ref[0, h], axis=0, keepdims=True) + p_own * vo_ref[0, h:h + 1, :]
        o_ref[0, h:h + 1, :] = acc / denom


def sample_attend(q, k_own, v_own, k_sel, v_sel):
    bd, h, n_keys, dh = k_sel.shape
    tok = pl.BlockSpec((1, h, dh), lambda b: (b, 0, 0))
    sel = pl.BlockSpec((1, h, n_keys, dh), lambda b: (b, 0, 0, 0))
    return pl.pallas_call(
        _sample_attend_kernel,
        grid=(bd,),
        in_specs=[tok, tok, tok, sel, sel],
        out_specs=tok,
        out_shape=jax.ShapeDtypeStruct((bd, h, dh), jnp.float32),
        compiler_params=pltpu.CompilerParams(dimension_semantics=("parallel",), vmem_limit_bytes=VMEM_LIMIT),
        name="sample_attend",
    )(q, k_own, v_own, k_sel, v_sel)


def cached_block_kmeans(cache_k, page_table):
    bd, n_pages = page_table.shape
    blocks = cache_k[page_table].reshape(bd, n_pages // PAGES_PER_BLOCK, MOBA_BLOCK, H_ATT, D_HEAD)
    return jnp.mean(blocks, axis=2)


def moba_sample(q, k, v, kmean, cache_k, cache_v, page_table):
    bd, s, w = q.shape
    assert s == 1 and PAST_LEN % MOBA_BLOCK == 0 and PAST_LEN // MOBA_BLOCK >= MOBA_TOPK
    heads = lambda z: z.reshape(bd, H_ATT, D_HEAD)
    sel = sample_select(heads(q), kmean)
    logical = sel[..., None] * PAGES_PER_BLOCK + jnp.arange(PAGES_PER_BLOCK)
    phys = page_table[jnp.arange(bd)[:, None, None, None], logical]
    hi = jnp.arange(H_ATT)[None, :, None, None]
    gather = lambda c: c[phys, :, hi, :].reshape(bd, H_ATT, MOBA_TOPK * MOBA_BLOCK, D_HEAD)
    out = sample_attend(heads(q), heads(k), heads(v), gather(cache_k), gather(cache_v))
    return out.reshape(bd, 1, w)


PEER_CANDS = [(a, b) for a in range(PEER_TOPK) for b in range(PEER_TOPK) if (a + 1) * (b + 1) <= PEER_TOPK]
PEER_NCAND = -(-len(PEER_CANDS) // 8) * 8


def _extract_topk(x, n_rows, k, exact):
    rows = lax.broadcasted_iota(jnp.int32, x.shape, 0).astype(jnp.float32)
    vals = []
    for _ in range(k):
        m = jnp.max(x, axis=0, keepdims=True)
        if exact:
            first = jnp.min(jnp.where(x == m, rows, float(n_rows)), axis=0, keepdims=True)
            x = jnp.where(rows == first, NEG_INF, x)
        else:
            x = jnp.where(x == m, NEG_INF, x)
        vals.append(m)
    return x, vals


def _peer_route_tables(q_ref, keys_ref, t1_ref, s2_ref, e1_ref, e2_ref, sv_ref, comb_ref, exact):
    half = PEER_DQ // 2
    tm = q_ref.shape[0]
    n_pad = PEER_NCAND - len(PEER_CANDS)
    ties = jnp.zeros((1, tm), jnp.float32)
    removed = lambda rem: jnp.sum(jnp.where(rem == NEG_INF, 1.0, 0.0), axis=0, keepdims=True)
    for h in range(PEER_HEADS):
        masked = []
        for p in range(2):
            qs = q_ref[:, (2 * h + p) * half:(2 * h + p + 1) * half].astype(jnp.bfloat16)
            st = lax.dot_general(keys_ref[p], qs, NT_DIMS, preferred_element_type=jnp.float32)
            rem, vals = _extract_topk(st, PEER_NKEYS, PEER_TOPK, exact)
            if not exact:
                ties = ties + (removed(rem) - float(PEER_TOPK))
            for i, v in enumerate(vals):
                sv_ref[p, i:i + 1, :] = v
            masked.append(jnp.where(rem == NEG_INF, st, NEG_INF))
        comb_ref[...] = jnp.full(comb_ref.shape, NEG_INF, jnp.float32)
        for c, (a, b) in enumerate(PEER_CANDS):
            comb_ref[c:c + 1, :] = sv_ref[0, a:a + 1, :] + sv_ref[1, b:b + 1, :]
        rem, cvals = _extract_topk(comb_ref[...], PEER_NCAND, PEER_TOPK + 1, exact)
        if not exact:
            ties = ties + (removed(rem) - float(PEER_TOPK + 1 + n_pad))
        cmax = cvals[0]
        z = jnp.zeros_like(cmax)
        for v in cvals[:PEER_TOPK]:
            z = z + jnp.exp(v - cmax)
        cut = 0.5 * (cvals[PEER_TOPK - 1] + cvals[PEER_TOPK])
        t1_ref[h] = cut - masked[0]
        s2_ref[h] = masked[1]
        e1_ref[h] = jnp.exp(masked[0] - sv_ref[0, 0:1, :])
        e2_ref[h] = jnp.exp(masked[1] - sv_ref[1, 0:1, :]) / z
    return ties


def _peer_route_kernel(xn_ref, wpq_ref, keys_ref, t1_ref, s2_ref, e1_ref, e2_ref, q_ref, sv_ref, comb_ref):
    q_ref[...] = jnp.dot(xn_ref[...], wpq_ref[...], preferred_element_type=jnp.float32)
    tables = functools.partial(_peer_route_tables, q_ref, keys_ref, t1_ref, s2_ref, e1_ref, e2_ref, sv_ref, comb_ref)
    ties = tables(exact=False)

    @pl.when(jnp.max(ties) > 0.0)
    def _():
        tables(exact=True)


def peer_route(xn_bf16, wpq_bf16, keys_bf16, tm):
    n = xn_bf16.shape[0]
    tab = jax.ShapeDtypeStruct((PEER_HEADS, PEER_NKEYS, n), jnp.float32)
    tab_spec = pl.BlockSpec((PEER_HEADS, PEER_NKEYS, tm), lambda i: (0, 0, i))
    return pl.pallas_call(
        _peer_route_kernel,
        grid=(n // tm,),
        in_specs=[
            pl.BlockSpec((tm, D_MODEL), lambda i: (i, 0)),
            pl.BlockSpec((D_MODEL, PEER_HEADS * PEER_DQ), lambda i: (0, 0)),
            pl.BlockSpec((2, PEER_NKEYS, PEER_DQ // 2), lambda i: (0, 0, 0)),
        ],
        out_specs=[tab_spec] * 4,
        out_shape=[tab] * 4,
        scratch_shapes=[pltpu.VMEM((tm, PEER_HEADS * PEER_DQ), jnp.float32),
                        pltpu.VMEM((2, PEER_TOPK, tm), jnp.float32),
                        pltpu.VMEM((PEER_NCAND, tm), jnp.float32)],
        compiler_params=pltpu.CompilerParams(dimension_semantics=("parallel",),
                                             vmem_limit_bytes=VMEM_LIMIT),
        cost_estimate=pl.CostEstimate(
            flops=2 * n * PEER_HEADS * PEER_DQ * (D_MODEL + PEER_NKEYS) + 6 * n * PEER_HEADS * PEER_NKEYS * PEER_TOPK,
            transcendentals=2 * n * PEER_HEADS * PEER_NKEYS,
            bytes_accessed=2 * n * D_MODEL + 2 * D_MODEL * PEER_HEADS * PEER_DQ + 16 * n * PEER_HEADS * PEER_NKEYS),
        name="peer_route",
    )(xn_bf16, wpq_bf16, keys_bf16)


def _gelu_exact(x):
    return 0.5 * x * (1.0 + lax.erf(x * (1.0 / math.sqrt(2.0))))


def _peer_dense_kernel(xn_ref, u_ref, vt_ref, t1_ref, s2_ref, e1_ref, e2_ref, h_ref, nfin_ref, o_ref, acc_ref, p_ref):
    j = pl.program_id(1)
    n_e1 = u_ref.shape[0] // PEER_NKEYS
    tm = xn_ref.shape[0]

    @pl.when(j == 0)
    def _():
        acc_ref[...] = jnp.zeros_like(acc_ref)

    for eb in range(n_e1 // PEER_ELB):
        erows = slice(eb * PEER_ELB * PEER_NKEYS, (eb + 1) * PEER_ELB * PEER_NKEYS)
        at = lax.dot_general(u_ref[erows, :], xn_ref[...], NT_DIMS,
                             preferred_element_type=jnp.float32)
        for c in range(tm // LANES):
            cols = slice(c * LANES, (c + 1) * LANES)
            for r0 in range(0, PEER_NKEYS, PEER_SGR):
                accs = [jnp.zeros((PEER_SGR, LANES), jnp.float32) for _ in range(PEER_ELB)]
                for h in range(PEER_HEADS):
                    s2t = s2_ref[h, r0:r0 + PEER_SGR, cols]
                    e2t = e2_ref[h, r0:r0 + PEER_SGR, cols]
                    for i in range(PEER_ELB):
                        el = eb * PEER_ELB + i
                        picked = s2t >= t1_ref[h, el:el + 1, cols]
                        accs[i] = accs[i] + jnp.where(picked, e2t, 0.0) * e1_ref[h, el:el + 1, cols]
                for i in range(PEER_ELB):
                    lo = i * PEER_NKEYS + r0
                    act = _gelu_exact(at[lo:lo + PEER_SGR, cols])
                    p_ref[eb, lo:lo + PEER_SGR, cols] = (accs[i] * act).astype(jnp.bfloat16)
        acc_ref[...] += jnp.dot(vt_ref[:, erows], p_ref[eb], preferred_element_type=jnp.float32)

    @pl.when(j == pl.num_programs(1) - 1)
    def _():
        o_ref[...] = _rms_norm(h_ref[...] + acc_ref[...].T, nfin_ref[...])


def peer_dense(xn_bf16, u_bf16, vt_bf16, t1, s2, e1, e2, h, norm_final, tm, te):
    n = xn_bf16.shape[0]
    n_e1 = te // PEER_NKEYS
    return pl.pallas_call(
        _peer_dense_kernel,
        grid=(n // tm, PEER_EXPERTS // te),
        in_specs=[
            pl.BlockSpec((tm, D_MODEL), lambda i, j: (i, 0)),
            pl.BlockSpec((te, D_MODEL), lambda i, j: (j, 0)),
            pl.BlockSpec((D_MODEL, te), lambda i, j: (0, j)),
            pl.BlockSpec((PEER_HEADS, n_e1, tm), lambda i, j: (0, j, i)),
            pl.BlockSpec((PEER_HEADS, PEER_NKEYS, tm), lambda i, j: (0, 0, i)),
            pl.BlockSpec((PEER_HEADS, n_e1, tm), lambda i, j: (0, j, i)),
            pl.BlockSpec((PEER_HEADS, PEER_NKEYS, tm), lambda i, j: (0, 0, i)),
            pl.BlockSpec((tm, D_MODEL), lambda i, j: (i, 0)),
            pl.BlockSpec((1, D_MODEL), lambda i, j: (0, 0)),
        ],
        out_specs=pl.BlockSpec((tm, D_MODEL), lambda i, j: (i, 0)),
        out_shape=jax.ShapeDtypeStruct((n, D_MODEL), jnp.float32),
        scratch_shapes=[pltpu.VMEM((D_MODEL, tm), jnp.float32),
                        pltpu.VMEM((n_e1 // PEER_ELB, PEER_ELB * PEER_NKEYS, tm), jnp.bfloat16)],
        compiler_params=pltpu.CompilerParams(dimension_semantics=("parallel", "arbitrary"),
                                             vmem_limit_bytes=VMEM_LIMIT),
        cost_estimate=pl.CostEstimate(
            flops=(4 * D_MODEL + 40) * n * PEER_EXPERTS, transcendentals=n * PEER_EXPERTS,
            bytes_accessed=(n // tm) * 4 * PEER_EXPERTS * D_MODEL + n * (10 * D_MODEL + 16 * PEER_HEADS * PEER_NKEYS)),
        name="peer_dense",
    )(xn_bf16, u_bf16, vt_bf16, t1, s2, e1, e2, h, norm_final.reshape(1, D_MODEL))


def peer_block(h, xn_bf16, wpq_bf16, keys_bf16, u_bf16, vt_bf16, norm_final):
    n = h.shape[0]
    tm = PEER_TM if n >= PEER_TM else PEER_TM_SMALL
    pad = -n % tm
    hp, xb = jnp.pad(h, ((0, pad), (0, 0))), jnp.pad(xn_bf16, ((0, pad), (0, 0)))
    t1, s2, e1, e2 = peer_route(xb, wpq_bf16, keys_bf16, PEER_TM_SMALL)
    return peer_dense(xb, u_bf16, vt_bf16, t1, s2, e1, e2, hp, norm_final, tm, PEER_TE)[:n]


def layer(x, pos, prev0, s0, attend, norm_mix, w_in_bf16, rw, w_out_bf16, norm_ffn, peer, norm_final):
    mu, w0, w_dec, a0, w_aaa, w_gate, k_k, k_a, r_k, gn_g, gn_b = rw
    b, t, d = x.shape
    n = b * t
    if t == 1:
        xr, posr = x.reshape(1, b, d), jnp.broadcast_to(pos, (b,))
    else:
        xr, posr = x, pos
    seq = lambda z: z.reshape(b, t, z.shape[-1])
    pr, q, k, v = map(seq, project(xr, norm_mix, w_in_bf16, posr, min(xr.shape[1], PROJ_TT)))
    r, w, k2, kk, bb, vv, g, bonus = rwkv_prep(pr, prev0, mu, w0, w_dec, a0, w_aaa, w_gate, k_k, k_a, r_k, min(t, 512))
    y, s_new = rwkv_scan(r, w, k2, kk, bb, vv, s0, RWKV_NB, min(t, 128))
    a_out = attend(q, k, v)
    flat = lambda z: z.reshape(n, z.shape[-1])
    h, xn = merge(flat(y), flat(bonus), flat(g), gn_g, gn_b, flat(a_out), flat(x), w_out_bf16, norm_ffn, min(n, 512))
    out = peer_block(h, xn, *peer, norm_final)
    heads = lambda z: z.reshape(b, t, H_ATT, D_HEAD)
    return out.reshape(b, t, d), heads(k), heads(v), s_new, pr[:, -1]


def kernel(x_prompt, x_sample, cache_k, cache_v, page_table, state_wkv, state_shift, norm_mix, w_in, mu_shift, w0, w_decay_up, a0, w_aaa_up, w_gate_up, k_k, k_a, r_k, gn_gain, gn_bias, w_out, norm_ffn, w_pq, peer_sub_keys, expert_u, expert_v, norm_final):
    l = 0
    bf16 = jnp.bfloat16
    rw = (mu_shift[l], w0[l], w_decay_up[l], a0[l], w_aaa_up[l], w_gate_up[l], k_k[l], k_a[l], r_k[l], gn_gain[l], gn_bias[l])
    peer = (w_pq[l].astype(bf16), peer_sub_keys[l].astype(bf16), expert_u[l].astype(bf16), expert_v[l].T.astype(bf16))
    shared = (norm_mix[l], w_in[l].astype(bf16), rw, w_out[l].astype(bf16), norm_ffn[l], peer, norm_final)
    bp, tp, _ = x_prompt.shape
    bs, ts, _ = x_sample.shape
    kmean = cached_block_kmeans(cache_k[l], page_table)
    y_p, k_p, v_p, s_p, sh_p = layer(
        x_prompt, jnp.arange(tp, dtype=jnp.int32), jnp.zeros((bp, W_RWKV_IN), jnp.float32),
        jnp.zeros((bp, H_RWKV, D_HEAD, D_HEAD), jnp.float32), moba_prompt, *shared)
    attend_cache = lambda q, k, v: moba_sample(q, k, v, kmean, cache_k[l], cache_v[l], page_table)
    y_s, k_s, v_s, s_s, sh_s = layer(
        x_sample, PAST_LEN + jnp.arange(ts, dtype=jnp.int32), state_shift[l], state_wkv[l], attend_cache, *shared)
    return (y_p, y_s, k_p[None], v_p[None], s_p[None], sh_p[None], k_s[None], v_s[None], s_s[None], sh_s[None])
```

```python
import functools
import math

import jax
import jax.numpy as jnp
from jax import lax
from jax.experimental import pallas as pl
from jax.experimental.pallas import tpu as pltpu

D_MODEL = 1024
DEPTH = 1
PAST_LEN = 16384
PAGE_SIZE = 128
D_HEAD = 64
H_RWKV = 8
H_ATT = 8
W_RWKV = H_RWKV * D_HEAD
W_ATT = H_ATT * D_HEAD
D_DECAY_LORA = 64
D_AAA_LORA = 64
D_GATE_LORA = 128
W_RWKV_IN = 3 * W_RWKV + D_DECAY_LORA + D_AAA_LORA + D_GATE_LORA
W_IN = W_RWKV_IN + 3 * W_ATT
GN_EPS = 64e-5
NORM_EPS = 1e-6
MOBA_BLOCK = 256
MOBA_TOPK = 3
Q_CHUNK = 64
ROT_DIM = D_HEAD // 4
ROPE_THETA = 500000.0
PEER_HEADS = 8
PEER_NKEYS = 128
PEER_TOPK = 16
PEER_DQ = 256
PEER_EXPERTS = PEER_NKEYS * PEER_NKEYS
PEER_TM = 512
PEER_TM_SMALL = 256
PEER_TE = 2048
PEER_ELB = 4
PEER_SGR = 32
NEG_INF = float("-inf")
VMEM_LIMIT = 56 * 1024 * 1024
LANES = 128
SUBLANES = 8
PROJ_TT = 512
RWKV_NB = 8
NT_DIMS = (((1,), (1,)), ((), ()))


def _rms_norm(x, g):
    return x * lax.rsqrt(jnp.mean(x * x, axis=-1, keepdims=True) + NORM_EPS) * g


def rope_tables(pos):
    half = ROT_DIM // 2
    inv = 1.0 / (ROPE_THETA ** (jnp.arange(half, dtype=jnp.float32) * 2.0 / ROT_DIM))
    ang = pos.astype(jnp.float32)[:, None] * inv[None, :]
    lane = jnp.arange(LANES) % D_HEAD
    idx = lane % half
    cos = jnp.where(lane[None, :] < ROT_DIM, jnp.cos(ang)[:, idx], 1.0)
    sin = jnp.sin(ang)[:, idx]
    sin_lo = jnp.where(lane[None, :] < half, -sin, 0.0)
    sin_hi = jnp.where((lane[None, :] >= half) & (lane[None, :] < ROT_DIM), sin, 0.0)
    return cos, sin_lo, sin_hi


def _project_kernel(x_ref, g_ref, w_ref, cos_ref, slo_ref, shi_ref, pr_o, q_o, k_o, v_o):
    xn = _rms_norm(x_ref[0], g_ref[...])
    p = jnp.dot(xn.astype(jnp.bfloat16), w_ref[...], preferred_element_type=jnp.float32)
    pr_o[0] = p[:, :W_RWKV_IN]
    reps = W_ATT // LANES
    cos = jnp.concatenate([cos_ref[...]] * reps, axis=1)
    slo = jnp.concatenate([slo_ref[...]] * reps, axis=1)
    shi = jnp.concatenate([shi_ref[...]] * reps, axis=1)
    half = ROT_DIM // 2

    def rope(z):
        ahead = pltpu.roll(z, W_ATT - half, 1)
        behind = pltpu.roll(z, half, 1)
        return z * cos + ahead * slo + behind * shi

    q_o[0] = rope(p[:, W_RWKV_IN:W_RWKV_IN + W_ATT])
    k_o[0] = rope(p[:, W_RWKV_IN + W_ATT:W_RWKV_IN + 2 * W_ATT])
    v_o[0] = p[:, W_RWKV_IN + 2 * W_ATT:]


def project(x, norm_g, w_in_bf16, pos, tt):
    b, t, d = x.shape
    cos, slo, shi = rope_tables(pos)
    f32 = jnp.float32
    tab_spec = pl.BlockSpec((tt, LANES), lambda bi, ti: (ti, 0))
    out_spec = lambda w: pl.BlockSpec((1, tt, w), lambda bi, ti: (bi, ti, 0))
    return pl.pallas_call(
        _project_kernel,
        grid=(b, t // tt),
        in_specs=[
            pl.BlockSpec((1, tt, d), lambda bi, ti: (bi, ti, 0)),
            pl.BlockSpec((1, d), lambda bi, ti: (0, 0)),
            pl.BlockSpec((d, W_IN), lambda bi, ti: (0, 0)),
            tab_spec, tab_spec, tab_spec,
        ],
        out_specs=[out_spec(W_RWKV_IN), out_spec(W_ATT), out_spec(W_ATT), out_spec(W_ATT)],
        out_shape=[jax.ShapeDtypeStruct((b, t, W_RWKV_IN), f32)] + [jax.ShapeDtypeStruct((b, t, W_ATT), f32)] * 3,
        compiler_params=pltpu.CompilerParams(dimension_semantics=("parallel", "parallel"),
                                             vmem_limit_bytes=VMEM_LIMIT),
        cost_estimate=pl.CostEstimate(flops=2 * b * t * d * W_IN, transcendentals=b * t,
                                      bytes_accessed=4 * b * t * (d + W_IN) + 2 * d * W_IN + 12 * t * LANES),
        name="project",
    )(x, norm_g.reshape(1, d), w_in_bf16, cos, slo, shi)


N_PAIR = W_RWKV // LANES
HI = lax.Precision.HIGHEST


def _sigmoid(x):
    return 1.0 / (1.0 + jnp.exp(-x))


def _softplus(x):
    return jnp.maximum(x, 0.0) + jnp.log(1.0 + jnp.exp(-jnp.abs(x)))


def _rwkv_prep_kernel(p_ref, prev0_ref, mu_ref, w0_ref, wdec_ref, a0_ref, waaa_ref, wgate_ref, kk_w_ref, ka_ref,
                      rk_ref, seg_ref, r_o, w_o, k_o, kk_o, b_o, v_o, g_o, bonus_o, carry_ref):
    ti = pl.program_id(1)

    @pl.when(ti == 0)
    def _():
        carry_ref[...] = prev0_ref[0]

    p = p_ref[0]
    row = lax.broadcasted_iota(jnp.int32, p.shape, 0)
    if p.shape[0] == 1:
        prev = carry_ref[...]
    else:
        prev = jnp.where(row == 0, carry_ref[...], pltpu.roll(p, 1, 0))
    carry_ref[...] = p[p.shape[0] - 1:, :]
    ps = p + (prev - p) * mu_ref[...]
    r = ps[:, 0:W_RWKV]
    k = ps[:, W_RWKV:2 * W_RWKV]
    v = ps[:, 2 * W_RWKV:3 * W_RWKV]
    lora = ps[:, 3 * W_RWKV:3 * W_RWKV + LANES]
    gd = ps[:, 3 * W_RWKV + LANES:]
    dec = jnp.dot(jnp.tanh(lora).astype(jnp.bfloat16), wdec_ref[...], preferred_element_type=jnp.float32)
    logw = -jnp.exp(-_softplus(-(w0_ref[...] + dec)) - 0.5)
    a = _sigmoid(a0_ref[...] + jnp.dot(lora.astype(jnp.bfloat16), waaa_ref[...], preferred_element_type=jnp.float32))
    g = jnp.dot(_sigmoid(gd).astype(jnp.bfloat16), wgate_ref[...], preferred_element_type=jnp.float32)
    kk = k * kk_w_ref[...]
    sumsq = jnp.dot(kk * kk, seg_ref[...], precision=HI, preferred_element_type=jnp.float32)
    kk = kk / jnp.maximum(jnp.sqrt(sumsq), 1e-12)
    k2 = k * (1.0 + (a - 1.0) * ka_ref[...])
    rkk = jnp.dot(r * k2 * rk_ref[...], seg_ref[...], precision=HI, preferred_element_type=jnp.float32)
    r_o[0] = r
    w_o[0] = jnp.exp(logw)
    k_o[0] = k2
    kk_o[0] = kk
    b_o[0] = kk * a
    v_o[0] = v
    g_o[0] = g
    bonus_o[0] = rkk * v


def rwkv_prep(p, prev0, mu, w0, w_dec, a0, w_aaa, w_gate, k_k, k_a, r_k, tt):
    b, t, _ = p.shape
    f32 = jnp.float32
    wdec_pad = jnp.concatenate([w_dec, jnp.zeros_like(w_aaa)], axis=0).astype(jnp.bfloat16)
    waaa_pad = jnp.concatenate([jnp.zeros_like(w_dec), w_aaa], axis=0).astype(jnp.bfloat16)
    head = jnp.arange(W_RWKV) // D_HEAD
    seg = (head[:, None] == head[None, :]).astype(f32)
    row = lambda z: z.reshape(1, -1).astype(f32)
    vec_spec = lambda n: pl.BlockSpec((1, n), lambda bi, ti: (0, 0))
    mat_spec = lambda m, n: pl.BlockSpec((m, n), lambda bi, ti: (0, 0))
    out_spec = pl.BlockSpec((1, tt, W_RWKV), lambda bi, ti: (bi, ti, 0))
    out = jax.ShapeDtypeStruct((b, t, W_RWKV), f32)
    return pl.pallas_call(
        _rwkv_prep_kernel,
        grid=(b, t // tt),
        in_specs=[
            pl.BlockSpec((1, tt, W_RWKV_IN), lambda bi, ti: (bi, ti, 0)),
            pl.BlockSpec((1, 1, W_RWKV_IN), lambda bi, ti: (bi, 0, 0)),
            vec_spec(W_RWKV_IN), vec_spec(W_RWKV), mat_spec(LANES, W_RWKV), vec_spec(W_RWKV),
            mat_spec(LANES, W_RWKV), mat_spec(D_GATE_LORA, W_RWKV), vec_spec(W_RWKV), vec_spec(W_RWKV),
            vec_spec(W_RWKV), mat_spec(W_RWKV, W_RWKV),
        ],
        out_specs=[out_spec] * 8,
        out_shape=[out] * 8,
        scratch_shapes=[pltpu.VMEM((1, W_RWKV_IN), f32)],
        compiler_params=pltpu.CompilerParams(dimension_semantics=("parallel", "arbitrary"),
                                             vmem_limit_bytes=VMEM_LIMIT),
        name="rwkv_prep",
    )(p, prev0.reshape(b, 1, W_RWKV_IN), row(mu), row(w0), wdec_pad, row(a0), waaa_pad, w_gate.astype(jnp.bfloat16),
      row(k_k), row(k_a), row(r_k), seg)


def _rwkv_scan_kernel(r_ref, w_ref, k_ref, kk_ref, b_ref, v_ref, s0_ref, y_ref, sT_ref, s_ref):
    ci = pl.program_id(1)
    nb, tc = r_ref.shape[0], r_ref.shape[1]
    tiles = [(bi, p) for bi in range(nb) for p in range(N_PAIR)]

    @pl.when(ci == 0)
    def _():
        for i, (bi, p) in enumerate(tiles):
            s_ref[i] = s0_ref[bi, p]

    lane = lax.broadcasted_iota(jnp.int32, (D_HEAD, LANES), 1)
    sub = lax.broadcasted_iota(jnp.int32, (D_HEAD, LANES), 0)
    first = lane < D_HEAD
    eye2 = (sub == (lane % D_HEAD))
    sub128 = lax.broadcasted_iota(jnp.int32, (LANES, LANES), 0)
    lane128 = lax.broadcasted_iota(jnp.int32, (LANES, LANES), 1)
    eye128 = sub128 == lane128
    seg = jnp.where((sub128 // D_HEAD) == (lane128 // D_HEAD), 1.0, 0.0).astype(jnp.bfloat16)

    def head_sums(xs):
        parts = []
        for x in xs:
            hi = x.astype(jnp.bfloat16)
            parts += [hi, (x - hi.astype(jnp.float32)).astype(jnp.bfloat16)]
        both = jnp.dot(jnp.concatenate(parts, axis=0), seg, preferred_element_type=jnp.float32)
        return [both[i * LANES:i * LANES + D_HEAD] + both[i * LANES + D_HEAD:(i + 1) * LANES] for i in range(len(xs))]

    grp = min(tc, SUBLANES)

    def steps(gi, carry):
        rows = pl.ds(pl.multiple_of(gi * grp, grp), grp)
        lanes = lambda p: slice(p * LANES, (p + 1) * LANES)
        load = lambda ref: [ref[bi, rows, lanes(p)] for bi, p in tiles]
        r_g, w_g, k_g, kk_g, b_g, v_g = load(r_ref), load(w_ref), load(k_ref), load(kk_ref), load(b_ref), load(v_ref)
        states = [s_ref[i] for i in range(len(tiles))]
        yrows = [[] for _ in tiles]
        for j in range(grp):
            row = slice(j, j + 1)
            sks = head_sums([s * kk[row] for s, kk in zip(states, kk_g)])
            for i in range(len(tiles)):
                vcol = jnp.sum(jnp.where(eye128, v_g[i][row], 0.0), axis=1, keepdims=True)
                vmat = jnp.where(first, vcol[0:D_HEAD], vcol[D_HEAD:LANES])
                states[i] = states[i] * w_g[i][row] - sks[i] * b_g[i][row] + vmat * k_g[i][row]
            ys = head_sums([s * r[row] for s, r in zip(states, r_g)])
            for i in range(len(tiles)):
                yrows[i].append(jnp.sum(jnp.where(eye2, ys[i], 0.0), axis=0, keepdims=True))
        for i, (bi, p) in enumerate(tiles):
            y_ref[bi, rows, lanes(p)] = yrows[i][0] if grp == 1 else jnp.concatenate(yrows[i], axis=0)
            s_ref[i] = states[i]
        return carry

    lax.fori_loop(0, tc // grp, steps, 0)

    @pl.when(ci == pl.num_programs(1) - 1)
    def _():
        for i, (bi, p) in enumerate(tiles):
            sT_ref[bi, p] = s_ref[i]


def rwkv_scan(r, w, k, kk, bb, v, s0, nb, tc):
    b, t, _ = r.shape
    f32 = jnp.float32
    pair = lambda s: s.reshape(b, N_PAIR, 2, D_HEAD, D_HEAD).transpose(0, 1, 3, 2, 4).reshape(b, N_PAIR, D_HEAD, LANES)
    unpair = lambda s: s.reshape(b, N_PAIR, D_HEAD, 2, D_HEAD).transpose(0, 1, 3, 2, 4).reshape(b, H_RWKV, D_HEAD, D_HEAD)
    seq_spec = pl.BlockSpec((nb, tc, W_RWKV), lambda bi, ci: (bi, ci, 0))
    st_spec = pl.BlockSpec((nb, N_PAIR, D_HEAD, LANES), lambda bi, ci: (bi, 0, 0, 0))
    y, s_fin = pl.pallas_call(
        _rwkv_scan_kernel,
        grid=(b // nb, t // tc),
        in_specs=[seq_spec] * 6 + [st_spec],
        out_specs=[seq_spec, st_spec],
        out_shape=[jax.ShapeDtypeStruct((b, t, W_RWKV), f32), jax.ShapeDtypeStruct((b, N_PAIR, D_HEAD, LANES), f32)],
        scratch_shapes=[pltpu.VMEM((nb * N_PAIR, D_HEAD, LANES), f32)],
        compiler_params=pltpu.CompilerParams(dimension_semantics=("parallel", "arbitrary"),
                                             vmem_limit_bytes=VMEM_LIMIT),
        cost_estimate=pl.CostEstimate(flops=b * t * N_PAIR * (4 * LANES ** 3 + 20 * D_HEAD * LANES), transcendentals=0,
                                      bytes_accessed=4 * (7 * b * t * W_RWKV + 2 * b * W_RWKV * D_HEAD)),
        name="rwkv_scan",
    )(r, w, k, kk, bb, v, pair(s0.astype(f32)))
    return y, unpair(s_fin)


def _merge_kernel(y_ref, bonus_ref, g_ref, gng_ref, gnb_ref, seg_ref, a_ref, x_ref, wr_ref, wa_ref, nf_ref, h_o, xn_o):
    y = y_ref[...]
    inv = 1.0 / D_HEAD
    mean = jnp.dot(y, seg_ref[...], precision=HI, preferred_element_type=jnp.float32) * inv
    dlt = y - mean
    var = jnp.dot(dlt * dlt, seg_ref[...], precision=HI, preferred_element_type=jnp.float32) * inv
    yn = dlt * lax.rsqrt(var + GN_EPS) * gng_ref[...] + gnb_ref[...]
    r_out = (yn + bonus_ref[...]) * g_ref[...]
    h = (x_ref[...] + jnp.dot(r_out.astype(jnp.bfloat16), wr_ref[...], preferred_element_type=jnp.float32)
         + jnp.dot(a_ref[...].astype(jnp.bfloat16), wa_ref[...], preferred_element_type=jnp.float32))
    h_o[...] = h
    xn_o[...] = _rms_norm(h, nf_ref[...]).astype(jnp.bfloat16)


def merge(y, bonus, g, gn_g, gn_b, a_out, x, w_out_bf16, norm_ffn, tt):
    n, d = x.shape
    head = jnp.arange(W_RWKV) // D_HEAD
    seg = (head[:, None] == head[None, :]).astype(jnp.float32)
    row = lambda w: pl.BlockSpec((tt, w), lambda i: (i, 0))
    vec = lambda w: pl.BlockSpec((1, w), lambda i: (0, 0))
    mat = lambda r, c: pl.BlockSpec((r, c), lambda i: (0, 0))
    return pl.pallas_call(
        _merge_kernel,
        grid=(n // tt,),
        in_specs=[row(W_RWKV), row(W_RWKV), row(W_RWKV), vec(W_RWKV), vec(W_RWKV), mat(W_RWKV, W_RWKV),
                  row(W_ATT), row(d), mat(W_RWKV, d), mat(W_ATT, d), vec(d)],
        out_specs=[row(d), row(d)],
        out_shape=[jax.ShapeDtypeStruct((n, d), jnp.float32), jax.ShapeDtypeStruct((n, d), jnp.bfloat16)],
        compiler_params=pltpu.CompilerParams(dimension_semantics=("parallel",), vmem_limit_bytes=VMEM_LIMIT),
        name="merge",
    )(y, bonus, g, gn_g.reshape(1, -1), gn_b.reshape(1, -1), seg, a_out, x, w_out_bf16[:W_RWKV], w_out_bf16[W_RWKV:],
      norm_ffn.reshape(1, d))


def _moba_block_kernel(q_ref, k_ref, v_ref, o_ref, *, qi):
    scale = D_HEAD ** -0.5
    nb = qi + 1
    k_all = k_ref[0]
    kb = k_all.astype(jnp.bfloat16)
    vt = v_ref[0].T.astype(jnp.bfloat16)
    q2 = q_ref[0]
    lane = lax.broadcasted_iota(jnp.int32, q2.shape, 1)
    key_row = lax.broadcasted_iota(jnp.int32, (MOBA_BLOCK, MOBA_BLOCK), 0)
    qry_col = lax.broadcasted_iota(jnp.int32, (MOBA_BLOCK, MOBA_BLOCK), 1)
    row = lax.broadcasted_iota(jnp.int32, (LANES, MOBA_BLOCK), 0)
    blk = lambda z, j: z[j * MOBA_BLOCK:(j + 1) * MOBA_BLOCK]
    outs = []
    for s in range(2):
        qm = jnp.where((lane // D_HEAD) == s, q2, 0.0)
        qmb = qm.astype(jnp.bfloat16)
        picked = [None] * qi
        if qi > MOBA_TOPK:
            kmean = jnp.concatenate([jnp.mean(blk(k_all, j), axis=0, keepdims=True) for j in range(qi)], axis=0)
            gate = lax.dot_general(kmean, qm, NT_DIMS, precision=HI, preferred_element_type=jnp.float32)
            for j in range(qi):
                ahead = jnp.zeros((1, MOBA_BLOCK), jnp.float32)
                for jp in range(qi):
                    if jp != j:
                        beats = (gate[jp:jp + 1] >= gate[j:j + 1]) if jp < j else (gate[jp:jp + 1] > gate[j:j + 1])
                        ahead = ahead + jnp.where(beats, 1.0, 0.0)
                picked[j] = ahead < float(MOBA_TOPK)
        scores = []
        for j in range(nb):
            st = lax.dot_general(blk(kb, j), qmb, NT_DIMS, preferred_element_type=jnp.float32) * scale
            if j == qi:
                st = jnp.where(key_row <= qry_col, st, NEG_INF)
            elif picked[j] is not None:
                st = jnp.where(picked[j], st, NEG_INF)
            scores.append(st)
        m = functools.reduce(jnp.maximum, [jnp.max(st, axis=0, keepdims=True) for st in scores])
        ps = [jnp.exp(st - m) for st in scores]
        denom = functools.reduce(jnp.add, [jnp.sum(p, axis=0, keepdims=True) for p in ps])
        acc = functools.reduce(jnp.add, [
            jnp.dot(vt[:, j * MOBA_BLOCK:(j + 1) * MOBA_BLOCK], ps[j].astype(jnp.bfloat16),
                    preferred_element_type=jnp.float32) for j in range(nb)])
        outs.append(acc / denom)
    o_ref[0] = jnp.where((row // D_HEAD) == 0, outs[0], outs[1]).T


def moba_prompt(q, k, v):
    b, t, w = q.shape
    outs = []
    for qi in range(t // MOBA_BLOCK):
        n_keys = (qi + 1) * MOBA_BLOCK
        outs.append(pl.pallas_call(
            functools.partial(_moba_block_kernel, qi=qi),
            grid=(b, w // LANES),
            in_specs=[
                pl.BlockSpec((1, MOBA_BLOCK, LANES), lambda bi, hp, qi=qi: (bi, qi, hp)),
                pl.BlockSpec((1, n_keys, LANES), lambda bi, hp: (bi, 0, hp)),
                pl.BlockSpec((1, n_keys, LANES), lambda bi, hp: (bi, 0, hp)),
            ],
            out_specs=pl.BlockSpec((1, MOBA_BLOCK, LANES), lambda bi, hp: (bi, 0, hp)),
            out_shape=jax.ShapeDtypeStruct((b, MOBA_BLOCK, w), jnp.float32),
            compiler_params=pltpu.CompilerParams(dimension_semantics=("parallel", "parallel"),
                                                 vmem_limit_bytes=VMEM_LIMIT),
            name=f"moba_prompt_q{qi}",
        )(q, k, v))
    return jnp.concatenate(outs, axis=1)


PAGES_PER_BLOCK = MOBA_BLOCK // PAGE_SIZE


def _sample_select_kernel(q_ref, km_ref, idx_ref):
    gate = jnp.sum(km_ref[0] * q_ref[...], axis=-1, keepdims=True)
    nb = gate.shape[0]
    blocks = lax.broadcasted_iota(jnp.int32, gate.shape, 0).astype(jnp.float32)
    for i in range(MOBA_TOPK):
        m = jnp.max(gate, axis=0, keepdims=True)
        first = jnp.min(jnp.where(gate == m, blocks, float(nb)), axis=0, keepdims=True)
        gate = jnp.where(blocks == first, NEG_INF, gate)
        idx_ref[0, i] = first[0]


def sample_select(q, kmean):
    bd, nb, h, dh = kmean.shape
    idx = pl.pallas_call(
        _sample_select_kernel,
        grid=(bd,),
        in_specs=[pl.BlockSpec((1, h, dh), lambda b: (b, 0, 0)), pl.BlockSpec((1, nb, h, dh), lambda b: (b, 0, 0, 0))],
        out_specs=pl.BlockSpec((1, MOBA_TOPK, h, 1), lambda b: (b, 0, 0, 0)),
        out_shape=jax.ShapeDtypeStruct((bd, MOBA_TOPK, h, 1), jnp.float32),
        compiler_params=pltpu.CompilerParams(dimension_semantics=("parallel",), vmem_limit_bytes=VMEM_LIMIT),
        name="sample_select",
    )(q, kmean)
    return idx[..., 0].astype(jnp.int32).transpose(0, 2, 1)


def _sample_attend_kernel(q_ref, ko_ref, vo_ref, ks_ref, vs_ref, o_ref):
    scale = D_HEAD ** -0.5
    for h in range(H_ATT):
        qh = q_ref[0, h:h + 1, :]
        s_sel = jnp.sum(ks_ref[0, h] * qh, axis=1, keepdims=True) * scale
        s_own = jnp.sum(ko_ref[0, h:h + 1, :] * qh, axis=1, keepdims=True) * scale
        m = jnp.maximum(jnp.max(s_sel, axis=0, keepdims=True), s_own)
        p_sel = jnp.exp(s_sel - m)
        p_own = jnp.exp(s_own - m)
        denom = jnp.sum(p_sel, axis=0, keepdims=True) + p_own
        acc = jnp.sum(p_sel * vs_ref[0, h], axis=0, keepdims=True) + p_own * vo_ref[0, h:h + 1, :]
        o_ref[0, h:h + 1, :] = acc / denom


def sample_attend(q, k_own, v_own, k_sel, v_sel):
    bd, h, n_keys, dh = k_sel.shape
    tok = pl.BlockSpec((1, h, dh), lambda b: (b, 0, 0))
    sel = pl.BlockSpec((1, h, n_keys, dh), lambda b: (b, 0, 0, 0))
    return pl.pallas_call(
        _sample_attend_kernel,
        grid=(bd,),
        in_specs=[tok, tok, tok, sel, sel],
        out_specs=tok,
        out_shape=jax.ShapeDtypeStruct((bd, h, dh), jnp.float32),
        compiler_params=pltpu.CompilerParams(dimension_semantics=("parallel",), vmem_limit_bytes=VMEM_LIMIT),
        name="sample_attend",
    )(q, k_own, v_own, k_sel, v_sel)


def cached_block_kmeans(pages):
    bd, n_pages = pages.shape[:2]
    return jnp.mean(pages.reshape(bd, n_pages // PAGES_PER_BLOCK, MOBA_BLOCK, H_ATT, D_HEAD), axis=2)


def moba_sample(q, k, v, kmean, cache_k, cache_v, page_table):
    bd, s, w = q.shape
    assert s == 1 and PAST_LEN % MOBA_BLOCK == 0 and PAST_LEN // MOBA_BLOCK >= MOBA_TOPK
    heads = lambda z: z.reshape(bd, H_ATT, D_HEAD)
    sel = sample_select(heads(q), kmean)
    logical = sel[..., None] * PAGES_PER_BLOCK + jnp.arange(PAGES_PER_BLOCK)
    phys = page_table[jnp.arange(bd)[:, None, None, None], logical]
    hi = jnp.arange(H_ATT)[None, :, None, None]
    gather = lambda c: c[phys, :, hi, :].reshape(bd, H_ATT, MOBA_TOPK * MOBA_BLOCK, D_HEAD)
    out = sample_attend(heads(q), heads(k), heads(v), gather(cache_k), gather(cache_v))
    return out.reshape(bd, 1, w)


PEER_CANDS = [(a, b) for a in range(PEER_TOPK) for b in range(PEER_TOPK) if (a + 1) * (b + 1) <= PEER_TOPK]
PEER_NCAND = -(-len(PEER_CANDS) // 8) * 8


def _extract_topk(x, n_rows, k, exact):
    rows = lax.broadcasted_iota(jnp.int32, x.shape, 0).astype(jnp.float32)
    vals = []
    for _ in range(k):
        m = jnp.max(x, axis=0, keepdims=True)
        if exact:
            first = jnp.min(jnp.where(x == m, rows, float(n_rows)), axis=0, keepdims=True)
            x = jnp.where(rows == first, NEG_INF, x)
        else:
            x = jnp.where(x == m, NEG_INF, x)
        vals.append(m)
    return x, vals


def _peer_route_tables(q_ref, keys_ref, t1_ref, s2_ref, e1_ref, e2_ref, sv_ref, comb_ref, exact):
    half = PEER_DQ // 2
    tm = q_ref.shape[0]
    n_pad = PEER_NCAND - len(PEER_CANDS)
    ties = jnp.zeros((1, tm), jnp.float32)
    removed = lambda rem: jnp.sum(jnp.where(rem == NEG_INF, 1.0, 0.0), axis=0, keepdims=True)
    for h in range(PEER_HEADS):
        masked = []
        for p in range(2):
            qs = q_ref[:, (2 * h + p) * half:(2 * h + p + 1) * half].astype(jnp.bfloat16)
            st = lax.dot_general(keys_ref[p], qs, NT_DIMS, preferred_element_type=jnp.float32)
            rem, vals = _extract_topk(st, PEER_NKEYS, PEER_TOPK, exact)
            if not exact:
                ties = ties + (removed(rem) - float(PEER_TOPK))
            for i, v in enumerate(vals):
                sv_ref[p, i:i + 1, :] = v
            masked.append(jnp.where(rem == NEG_INF, st, NEG_INF))
        comb_ref[...] = jnp.full(comb_ref.shape, NEG_INF, jnp.float32)
        for c, (a, b) in enumerate(PEER_CANDS):
            comb_ref[c:c + 1, :] = sv_ref[0, a:a + 1, :] + sv_ref[1, b:b + 1, :]
        rem, cvals = _extract_topk(comb_ref[...], PEER_NCAND, PEER_TOPK + 1, exact)
        if not exact:
            ties = ties + (removed(rem) - float(PEER_TOPK + 1 + n_pad))
        cmax = cvals[0]
        z = jnp.zeros_like(cmax)
        for v in cvals[:PEER_TOPK]:
            z = z + jnp.exp(v - cmax)
        cut = 0.5 * (cvals[PEER_TOPK - 1] + cvals[PEER_TOPK])
        t1_ref[h] = cut - masked[0]
        s2_ref[h] = masked[1]
        e1_ref[h] = jnp.exp(masked[0] - sv_ref[0, 0:1, :])
        e2_ref[h] = jnp.exp(masked[1] - sv_ref[1, 0:1, :]) / z
    return ties


def _peer_route_kernel(xn_ref, wpq_ref, keys_ref, t1_ref, s2_ref, e1_ref, e2_ref, q_ref, sv_ref, comb_ref):
    q_ref[...] = jnp.dot(xn_ref[...], wpq_ref[...], preferred_element_type=jnp.float32)
    tables = functools.partial(_peer_route_tables, q_ref, keys_ref, t1_ref, s2_ref, e1_ref, e2_ref, sv_ref, comb_ref)
    ties = tables(exact=False)

    @pl.when(jnp.max(ties) > 0.0)
    def _():
        tables(exact=True)


def peer_route(xn_bf16, wpq_bf16, keys_bf16, tm):
    n = xn_bf16.shape[0]
    tab = jax.ShapeDtypeStruct((PEER_HEADS, PEER_NKEYS, n), jnp.float32)
    tab_spec = pl.BlockSpec((PEER_HEADS, PEER_NKEYS, tm), lambda i: (0, 0, i))
    return pl.pallas_call(
        _peer_route_kernel,
        grid=(n // tm,),
        in_specs=[
            pl.BlockSpec((tm, D_MODEL), lambda i: (i, 0)),
            pl.BlockSpec((D_MODEL, PEER_HEADS * PEER_DQ), lambda i: (0, 0)),
            pl.BlockSpec((2, PEER_NKEYS, PEER_DQ // 2), lambda i: (0, 0, 0)),
        ],
        out_specs=[tab_spec] * 4,
        out_shape=[tab] * 4,
        scratch_shapes=[pltpu.VMEM((tm, PEER_HEADS * PEER_DQ), jnp.float32),
                        pltpu.VMEM((2, PEER_TOPK, tm), jnp.float32),
                        pltpu.VMEM((PEER_NCAND, tm), jnp.float32)],
        compiler_params=pltpu.CompilerParams(dimension_semantics=("parallel",),
                                             vmem_limit_bytes=VMEM_LIMIT),
        cost_estimate=pl.CostEstimate(
            flops=2 * n * PEER_HEADS * PEER_DQ * (D_MODEL + PEER_NKEYS) + 6 * n * PEER_HEADS * PEER_NKEYS * PEER_TOPK,
            transcendentals=2 * n * PEER_HEADS * PEER_NKEYS,
            bytes_accessed=2 * n * D_MODEL + 2 * D_MODEL * PEER_HEADS * PEER_DQ + 16 * n * PEER_HEADS * PEER_NKEYS),
        name="peer_route",
    )(xn_bf16, wpq_bf16, keys_bf16)


def _gelu_exact(x):
    return 0.5 * x * (1.0 + lax.erf(x * (1.0 / math.sqrt(2.0))))


def _peer_dense_kernel(xn_ref, u_ref, vt_ref, t1_ref, s2_ref, e1_ref, e2_ref, h_ref, nfin_ref, o_ref, acc_ref, p_ref):
    j = pl.program_id(1)
    n_e1 = u_ref.shape[0] // PEER_NKEYS
    tm = xn_ref.shape[0]

    @pl.when(j == 0)
    def _():
        acc_ref[...] = jnp.zeros_like(acc_ref)

    for eb in range(n_e1 // PEER_ELB):
        erows = slice(eb * PEER_ELB * PEER_NKEYS, (eb + 1) * PEER_ELB * PEER_NKEYS)
        at = lax.dot_general(u_ref[erows, :], xn_ref[...], NT_DIMS,
                             preferred_element_type=jnp.float32)
        for c in range(tm // LANES):
            cols = slice(c * LANES, (c + 1) * LANES)
            for r0 in range(0, PEER_NKEYS, PEER_SGR):
                accs = [jnp.zeros((PEER_SGR, LANES), jnp.float32) for _ in range(PEER_ELB)]
                for h in range(PEER_HEADS):
                    s2t = s2_ref[h, r0:r0 + PEER_SGR, cols]
                    e2t = e2_ref[h, r0:r0 + PEER_SGR, cols]
                    for i in range(PEER_ELB):
                        el = eb * PEER_ELB + i
                        picked = s2t >= t1_ref[h, el:el + 1, cols]
                        accs[i] = accs[i] + jnp.where(picked, e2t, 0.0) * e1_ref[h, el:el + 1, cols]
                for i in range(PEER_ELB):
                    lo = i * PEER_NKEYS + r0
                    act = _gelu_exact(at[lo:lo + PEER_SGR, cols])
                    p_ref[eb, lo:lo + PEER_SGR, cols] = (accs[i] * act).astype(jnp.bfloat16)
        acc_ref[...] += jnp.dot(vt_ref[:, erows], p_ref[eb], preferred_element_type=jnp.float32)

    @pl.when(j == pl.num_programs(1) - 1)
    def _():
        o_ref[...] = _rms_norm(h_ref[...] + acc_ref[...].T, nfin_ref[...])


def peer_dense(xn_bf16, u_bf16, vt_bf16, t1, s2, e1, e2, h, norm_final, tm, te):
    n = xn_bf16.shape[0]
    n_e1 = te // PEER_NKEYS
    return pl.pallas_call(
        _peer_dense_kernel,
        grid=(n // tm, PEER_EXPERTS // te),
        in_specs=[
            pl.BlockSpec((tm, D_MODEL), lambda i, j: (i, 0)),
            pl.BlockSpec((te, D_MODEL), lambda i, j: (j, 0)),
            pl.BlockSpec((D_MODEL, te), lambda i, j: (0, j)),
            pl.BlockSpec((PEER_HEADS, n_e1, tm), lambda i, j: (0, j, i)),
            pl.BlockSpec((PEER_HEADS, PEER_NKEYS, tm), lambda i, j: (0, 0, i)),
            pl.BlockSpec((PEER_HEADS, n_e1, tm), lambda i, j: (0, j, i)),
            pl.BlockSpec((PEER_HEADS, PEER_NKEYS, tm), lambda i, j: (0, 0, i)),
            pl.BlockSpec((tm, D_MODEL), lambda i, j: (i, 0)),
            pl.BlockSpec((1, D_MODEL), lambda i, j: (0, 0)),
        ],
        out_specs=pl.BlockSpec((tm, D_MODEL), lambda i, j: (i, 0)),
        out_shape=jax.ShapeDtypeStruct((n, D_MODEL), jnp.float32),
        scratch_shapes=[pltpu.VMEM((D_MODEL, tm), jnp.float32),
                        pltpu.VMEM((n_e1 // PEER_ELB, PEER_ELB * PEER_NKEYS, tm), jnp.bfloat16)],
        compiler_params=pltpu.CompilerParams(dimension_semantics=("parallel", "arbitrary"),
                                             vmem_limit_bytes=VMEM_LIMIT),
        cost_estimate=pl.CostEstimate(
            flops=(4 * D_MODEL + 40) * n * PEER_EXPERTS, transcendentals=n * PEER_EXPERTS,
            bytes_accessed=(n // tm) * 4 * PEER_EXPERTS * D_MODEL + n * (10 * D_MODEL + 16 * PEER_HEADS * PEER_NKEYS)),
        name="peer_dense",
    )(xn_bf16, u_bf16, vt_bf16, t1, s2, e1, e2, h, norm_final.reshape(1, D_MODEL))


def peer_block(h, xn_bf16, wpq_bf16, keys_bf16, u_bf16, vt_bf16, norm_final):
    n = h.shape[0]
    tm = PEER_TM if n >= PEER_TM else PEER_TM_SMALL
    pad = -n % tm
    hp, xb = jnp.pad(h, ((0, pad), (0, 0))), jnp.pad(xn_bf16, ((0, pad), (0, 0)))
    t1, s2, e1, e2 = peer_route(xb, wpq_bf16, keys_bf16, PEER_TM_SMALL)
    return peer_dense(xb, u_bf16, vt_bf16, t1, s2, e1, e2, hp, norm_final, tm, PEER_TE)[:n]


def layer(x, pos, prev0, s0, attend, norm_mix, w_in_bf16, rw, w_out_bf16, norm_ffn, peer, norm_final):
    mu, w0, w_dec, a0, w_aaa, w_gate, k_k, k_a, r_k, gn_g, gn_b = rw
    b, t, d = x.shape
    n = b * t
    if t == 1:
        xr, posr = x.reshape(1, b, d), jnp.broadcast_to(pos, (b,))
    else:
        xr, posr = x, pos
    seq = lambda z: z.reshape(b, t, z.shape[-1])
    pr, q, k, v = map(seq, project(xr, norm_mix, w_in_bf16, posr, min(xr.shape[1], PROJ_TT)))
    r, w, k2, kk, bb, vv, g, bonus = rwkv_prep(pr, prev0, mu, w0, w_dec, a0, w_aaa, w_gate, k_k, k_a, r_k, min(t, 512))
    y, s_new = rwkv_scan(r, w, k2, kk, bb, vv, s0, RWKV_NB, min(t, 128))
    a_out = attend(q, k, v)
    flat = lambda z: z.reshape(n, z.shape[-1])
    h, xn = merge(flat(y), flat(bonus), flat(g), gn_g, gn_b, flat(a_out), flat(x), w_out_bf16, norm_ffn, min(n, 512))
    out = peer_block(h, xn, *peer, norm_final)
    heads = lambda z: z.reshape(b, t, H_ATT, D_HEAD)
    return out.reshape(b, t, d), heads(k), heads(v), s_new, pr[:, -1]


def kernel(x_prompt, x_sample, cache_k, cache_v, page_table, state_wkv, state_shift, norm_mix, w_in, mu_shift, w0, w_decay_up, a0, w_aaa_up, w_gate_up, k_k, k_a, r_k, gn_gain, gn_bias, w_out, norm_ffn, w_pq, peer_sub_keys, expert_u, expert_v, norm_final):
    l = 0
    bf16 = jnp.bfloat16
    rw = (mu_shift[l], w0[l], w_decay_up[l], a0[l], w_aaa_up[l], w_gate_up[l], k_k[l], k_a[l], r_k[l], gn_gain[l], gn_bias[l])
    peer = (w_pq[l].astype(bf16), peer_sub_keys[l].astype(bf16), expert_u[l].astype(bf16), expert_v[l].T.astype(bf16))
    shared = (norm_mix[l], w_in[l].astype(bf16), rw, w_out[l].astype(bf16), norm_ffn[l], peer, norm_final)
    bp, tp, _ = x_prompt.shape
    bs, ts, _ = x_sample.shape
    key_pages = cache_k[l][page_table]
    y_p, k_p, v_p, s_p, sh_p = layer(
        x_prompt, jnp.arange(tp, dtype=jnp.int32), jnp.zeros((bp, W_RWKV_IN), jnp.float32),
        jnp.zeros((bp, H_RWKV, D_HEAD, D_HEAD), jnp.float32), moba_prompt, *shared)
    key_pages, y_p = lax.optimization_barrier((key_pages, y_p))
    kmean = cached_block_kmeans(key_pages)
    attend_cache = lambda q, k, v: moba_sample(q, k, v, kmean, cache_k[l], cache_v[l], page_table)
    y_s, k_s, v_s, s_s, sh_s = layer(
        x_sample, PAST_LEN + jnp.arange(ts, dtype=jnp.int32), state_shift[l], state_wkv[l], attend_cache, *shared)
    return (y_p, y_s, k_p[None], v_p[None], s_p[None], sh_p[None], k_s[None], v_s[None], s_s[None], sh_s[None])
```

```python
import functools
import math

import jax
import jax.numpy as jnp
from jax import lax
from jax.experimental import pallas as pl
from jax.experimental.pallas import tpu as pltpu

D_MODEL = 1024
DEPTH = 1
PAST_LEN = 16384
PAGE_SIZE = 128
D_HEAD = 64
H_RWKV = 8
H_ATT = 8
W_RWKV = H_RWKV * D_HEAD
W_ATT = H_ATT * D_HEAD
D_DECAY_LORA = 64
D_AAA_LORA = 64
D_GATE_LORA = 128
W_RWKV_IN = 3 * W_RWKV + D_DECAY_LORA + D_AAA_LORA + D_GATE_LORA
W_IN = W_RWKV_IN + 3 * W_ATT
GN_EPS = 64e-5
NORM_EPS = 1e-6
MOBA_BLOCK = 256
MOBA_TOPK = 3
Q_CHUNK = 64
ROT_DIM = D_HEAD // 4
ROPE_THETA = 500000.0
PEER_HEADS = 8
PEER_NKEYS = 128
PEER_TOPK = 16
PEER_DQ = 256
PEER_EXPERTS = PEER_NKEYS * PEER_NKEYS
PEER_TM = 512
PEER_TM_SMALL = 256
PEER_TE = 2048
PEER_ELB = 4
PEER_SGR = 32
NEG_INF = float("-inf")
VMEM_LIMIT = 56 * 1024 * 1024
LANES = 128
SUBLANES = 8
PROJ_TT = 512
RWKV_NB = 8
NT_DIMS = (((1,), (1,)), ((), ()))


def _rms_norm(x, g):
    return x * lax.rsqrt(jnp.mean(x * x, axis=-1, keepdims=True) + NORM_EPS) * g


def rope_tables(pos):
    half = ROT_DIM // 2
    inv = 1.0 / (ROPE_THETA ** (jnp.arange(half, dtype=jnp.float32) * 2.0 / ROT_DIM))
    ang = pos.astype(jnp.float32)[:, None] * inv[None, :]
    lane = jnp.arange(LANES) % D_HEAD
    idx = lane % half
    cos = jnp.where(lane[None, :] < ROT_DIM, jnp.cos(ang)[:, idx], 1.0)
    sin = jnp.sin(ang)[:, idx]
    sin_lo = jnp.where(lane[None, :] < half, -sin, 0.0)
    sin_hi = jnp.where((lane[None, :] >= half) & (lane[None, :] < ROT_DIM), sin, 0.0)
    return cos, sin_lo, sin_hi


def _project_kernel(x_ref, g_ref, w_ref, cos_ref, slo_ref, shi_ref, pr_o, q_o, k_o, v_o):
    xn = _rms_norm(x_ref[0], g_ref[...])
    p = jnp.dot(xn.astype(jnp.bfloat16), w_ref[...], preferred_element_type=jnp.float32)
    pr_o[0] = p[:, :W_RWKV_IN]
    reps = W_ATT // LANES
    cos = jnp.concatenate([cos_ref[...]] * reps, axis=1)
    slo = jnp.concatenate([slo_ref[...]] * reps, axis=1)
    shi = jnp.concatenate([shi_ref[...]] * reps, axis=1)
    half = ROT_DIM // 2

    def rope(z):
        ahead = pltpu.roll(z, W_ATT - half, 1)
        behind = pltpu.roll(z, half, 1)
        return z * cos + ahead * slo + behind * shi

    q_o[0] = rope(p[:, W_RWKV_IN:W_RWKV_IN + W_ATT])
    k_o[0] = rope(p[:, W_RWKV_IN + W_ATT:W_RWKV_IN + 2 * W_ATT])
    v_o[0] = p[:, W_RWKV_IN + 2 * W_ATT:]


def project(x, norm_g, w_in_bf16, pos, tt):
    b, t, d = x.shape
    cos, slo, shi = rope_tables(pos)
    f32 = jnp.float32
    tab_spec = pl.BlockSpec((tt, LANES), lambda bi, ti: (ti, 0))
    out_spec = lambda w: pl.BlockSpec((1, tt, w), lambda bi, ti: (bi, ti, 0))
    return pl.pallas_call(
        _project_kernel,
        grid=(b, t // tt),
        in_specs=[
            pl.BlockSpec((1, tt, d), lambda bi, ti: (bi, ti, 0)),
            pl.BlockSpec((1, d), lambda bi, ti: (0, 0)),
            pl.BlockSpec((d, W_IN), lambda bi, ti: (0, 0)),
            tab_spec, tab_spec, tab_spec,
        ],
        out_specs=[out_spec(W_RWKV_IN), out_spec(W_ATT), out_spec(W_ATT), out_spec(W_ATT)],
        out_shape=[jax.ShapeDtypeStruct((b, t, W_RWKV_IN), f32)] + [jax.ShapeDtypeStruct((b, t, W_ATT), f32)] * 3,
        compiler_params=pltpu.CompilerParams(dimension_semantics=("parallel", "parallel"),
                                             vmem_limit_bytes=VMEM_LIMIT),
        cost_estimate=pl.CostEstimate(flops=2 * b * t * d * W_IN, transcendentals=b * t,
                                      bytes_accessed=4 * b * t * (d + W_IN) + 2 * d * W_IN + 12 * t * LANES),
        name="project",
    )(x, norm_g.reshape(1, d), w_in_bf16, cos, slo, shi)


N_PAIR = W_RWKV // LANES
HI = lax.Precision.HIGHEST


def _sigmoid(x):
    return 1.0 / (1.0 + jnp.exp(-x))


def _softplus(x):
    return jnp.maximum(x, 0.0) + jnp.log(1.0 + jnp.exp(-jnp.abs(x)))


def _rwkv_prep_kernel(p_ref, prev0_ref, mu_ref, w0_ref, wdec_ref, a0_ref, waaa_ref, wgate_ref, kk_w_ref, ka_ref,
                      rk_ref, seg_ref, r_o, w_o, k_o, kk_o, b_o, v_o, g_o, bonus_o, carry_ref):
    ti = pl.program_id(1)

    @pl.when(ti == 0)
    def _():
        carry_ref[...] = prev0_ref[0]

    p = p_ref[0]
    row = lax.broadcasted_iota(jnp.int32, p.shape, 0)
    if p.shape[0] == 1:
        prev = carry_ref[...]
    else:
        prev = jnp.where(row == 0, carry_ref[...], pltpu.roll(p, 1, 0))
    carry_ref[...] = p[p.shape[0] - 1:, :]
    ps = p + (prev - p) * mu_ref[...]
    r = ps[:, 0:W_RWKV]
    k = ps[:, W_RWKV:2 * W_RWKV]
    v = ps[:, 2 * W_RWKV:3 * W_RWKV]
    lora = ps[:, 3 * W_RWKV:3 * W_RWKV + LANES]
    gd = ps[:, 3 * W_RWKV + LANES:]
    dec = jnp.dot(jnp.tanh(lora).astype(jnp.bfloat16), wdec_ref[...], preferred_element_type=jnp.float32)
    logw = -jnp.exp(-_softplus(-(w0_ref[...] + dec)) - 0.5)
    a = _sigmoid(a0_ref[...] + jnp.dot(lora.astype(jnp.bfloat16), waaa_ref[...], preferred_element_type=jnp.float32))
    g = jnp.dot(_sigmoid(gd).astype(jnp.bfloat16), wgate_ref[...], preferred_element_type=jnp.float32)
    kk = k * kk_w_ref[...]
    sumsq = jnp.dot(kk * kk, seg_ref[...], precision=HI, preferred_element_type=jnp.float32)
    kk = kk / jnp.maximum(jnp.sqrt(sumsq), 1e-12)
    k2 = k * (1.0 + (a - 1.0) * ka_ref[...])
    rkk = jnp.dot(r * k2 * rk_ref[...], seg_ref[...], precision=HI, preferred_element_type=jnp.float32)
    r_o[0] = r
    w_o[0] = jnp.exp(logw)
    k_o[0] = k2
    kk_o[0] = kk
    b_o[0] = kk * a
    v_o[0] = v
    g_o[0] = g
    bonus_o[0] = rkk * v


def rwkv_prep(p, prev0, mu, w0, w_dec, a0, w_aaa, w_gate, k_k, k_a, r_k, tt):
    b, t, _ = p.shape
    f32 = jnp.float32
    wdec_pad = jnp.concatenate([w_dec, jnp.zeros_like(w_aaa)], axis=0).astype(jnp.bfloat16)
    waaa_pad = jnp.concatenate([jnp.zeros_like(w_dec), w_aaa], axis=0).astype(jnp.bfloat16)
    head = jnp.arange(W_RWKV) // D_HEAD
    seg = (head[:, None] == head[None, :]).astype(f32)
    row = lambda z: z.reshape(1, -1).astype(f32)
    vec_spec = lambda n: pl.BlockSpec((1, n), lambda bi, ti: (0, 0))
    mat_spec = lambda m, n: pl.BlockSpec((m, n), lambda bi, ti: (0, 0))
    out_spec = pl.BlockSpec((1, tt, W_RWKV), lambda bi, ti: (bi, ti, 0))
    out = jax.ShapeDtypeStruct((b, t, W_RWKV), f32)
    return pl.pallas_call(
        _rwkv_prep_kernel,
        grid=(b, t // tt),
        in_specs=[
            pl.BlockSpec((1, tt, W_RWKV_IN), lambda bi, ti: (bi, ti, 0)),
            pl.BlockSpec((1, 1, W_RWKV_IN), lambda bi, ti: (bi, 0, 0)),
            vec_spec(W_RWKV_IN), vec_spec(W_RWKV), mat_spec(LANES, W_RWKV), vec_spec(W_RWKV),
            mat_spec(LANES, W_RWKV), mat_spec(D_GATE_LORA, W_RWKV), vec_spec(W_RWKV), vec_spec(W_RWKV),
            vec_spec(W_RWKV), mat_spec(W_RWKV, W_RWKV),
        ],
        out_specs=[out_spec] * 8,
        out_shape=[out] * 8,
        scratch_shapes=[pltpu.VMEM((1, W_RWKV_IN), f32)],
        compiler_params=pltpu.CompilerParams(dimension_semantics=("parallel", "arbitrary"),
                                             vmem_limit_bytes=VMEM_LIMIT),
        name="rwkv_prep",
    )(p, prev0.reshape(b, 1, W_RWKV_IN), row(mu), row(w0), wdec_pad, row(a0), waaa_pad, w_gate.astype(jnp.bfloat16),
      row(k_k), row(k_a), row(r_k), seg)


def _rwkv_scan_kernel(r_ref, w_ref, k_ref, kk_ref, b_ref, v_ref, s0_ref, y_ref, sT_ref, s_ref):
    ci = pl.program_id(1)
    nb, tc = r_ref.shape[0], r_ref.shape[1]
    tiles = [(bi, p) for bi in range(nb) for p in range(N_PAIR)]

    @pl.when(ci == 0)
    def _():
        for i, (bi, p) in enumerate(tiles):
            s_ref[i] = s0_ref[bi, p]

    lane = lax.broadcasted_iota(jnp.int32, (D_HEAD, LANES), 1)
    sub = lax.broadcasted_iota(jnp.int32, (D_HEAD, LANES), 0)
    first = lane < D_HEAD
    eye2 = (sub == (lane % D_HEAD))
    sub128 = lax.broadcasted_iota(jnp.int32, (LANES, LANES), 0)
    lane128 = lax.broadcasted_iota(jnp.int32, (LANES, LANES), 1)
    eye128 = sub128 == lane128
    seg = jnp.where((sub128 // D_HEAD) == (lane128 // D_HEAD), 1.0, 0.0).astype(jnp.bfloat16)

    def head_sums(xs):
        parts = []
        for x in xs:
            hi = x.astype(jnp.bfloat16)
            parts += [hi, (x - hi.astype(jnp.float32)).astype(jnp.bfloat16)]
        both = jnp.dot(jnp.concatenate(parts, axis=0), seg, preferred_element_type=jnp.float32)
        return [both[i * LANES:i * LANES + D_HEAD] + both[i * LANES + D_HEAD:(i + 1) * LANES] for i in range(len(xs))]

    grp = min(tc, SUBLANES)

    def steps(gi, carry):
        rows = pl.ds(pl.multiple_of(gi * grp, grp), grp)
        lanes = lambda p: slice(p * LANES, (p + 1) * LANES)
        load = lambda ref: [ref[bi, rows, lanes(p)] for bi, p in tiles]
        r_g, w_g, k_g, kk_g, b_g, v_g = load(r_ref), load(w_ref), load(k_ref), load(kk_ref), load(b_ref), load(v_ref)
        states = [s_ref[i] for i in range(len(tiles))]
        yrows = [[] for _ in tiles]
        for j in range(grp):
            row = slice(j, j + 1)
            sks = head_sums([s * kk[row] for s, kk in zip(states, kk_g)])
            for i in range(len(tiles)):
                vcol = jnp.sum(jnp.where(eye128, v_g[i][row], 0.0), axis=1, keepdims=True)
                vmat = jnp.where(first, vcol[0:D_HEAD], vcol[D_HEAD:LANES])
                states[i] = states[i] * w_g[i][row] - sks[i] * b_g[i][row] + vmat * k_g[i][row]
            ys = head_sums([s * r[row] for s, r in zip(states, r_g)])
            for i in range(len(tiles)):
                yrows[i].append(jnp.sum(jnp.where(eye2, ys[i], 0.0), axis=0, keepdims=True))
        for i, (bi, p) in enumerate(tiles):
            y_ref[bi, rows, lanes(p)] = yrows[i][0] if grp == 1 else jnp.concatenate(yrows[i], axis=0)
            s_ref[i] = states[i]
        return carry

    lax.fori_loop(0, tc // grp, steps, 0)

    @pl.when(ci == pl.num_programs(1) - 1)
    def _():
        for i, (bi, p) in enumerate(tiles):
            sT_ref[bi, p] = s_ref[i]


def rwkv_scan(r, w, k, kk, bb, v, s0, nb, tc):
    b, t, _ = r.shape
    f32 = jnp.float32
    pair = lambda s: s.reshape(b, N_PAIR, 2, D_HEAD, D_HEAD).transpose(0, 1, 3, 2, 4).reshape(b, N_PAIR, D_HEAD, LANES)
    unpair = lambda s: s.reshape(b, N_PAIR, D_HEAD, 2, D_HEAD).transpose(0, 1, 3, 2, 4).reshape(b, H_RWKV, D_HEAD, D_HEAD)
    seq_spec = pl.BlockSpec((nb, tc, W_RWKV), lambda bi, ci: (bi, ci, 0))
    st_spec = pl.BlockSpec((nb, N_PAIR, D_HEAD, LANES), lambda bi, ci: (bi, 0, 0, 0))
    y, s_fin = pl.pallas_call(
        _rwkv_scan_kernel,
        grid=(b // nb, t // tc),
        in_specs=[seq_spec] * 6 + [st_spec],
        out_specs=[seq_spec, st_spec],
        out_shape=[jax.ShapeDtypeStruct((b, t, W_RWKV), f32), jax.ShapeDtypeStruct((b, N_PAIR, D_HEAD, LANES), f32)],
        scratch_shapes=[pltpu.VMEM((nb * N_PAIR, D_HEAD, LANES), f32)],
        compiler_params=pltpu.CompilerParams(dimension_semantics=("parallel", "arbitrary"),
                                             vmem_limit_bytes=VMEM_LIMIT),
        cost_estimate=pl.CostEstimate(flops=b * t * N_PAIR * (4 * LANES ** 3 + 20 * D_HEAD * LANES), transcendentals=0,
                                      bytes_accessed=4 * (7 * b * t * W_RWKV + 2 * b * W_RWKV * D_HEAD)),
        name="rwkv_scan",
    )(r, w, k, kk, bb, v, pair(s0.astype(f32)))
    return y, unpair(s_fin)


def _merge_kernel(y_ref, bonus_ref, g_ref, gng_ref, gnb_ref, seg_ref, a_ref, x_ref, wr_ref, wa_ref, nf_ref, h_o, xn_o):
    y = y_ref[...]
    inv = 1.0 / D_HEAD
    mean = jnp.dot(y, seg_ref[...], precision=HI, preferred_element_type=jnp.float32) * inv
    dlt = y - mean
    var = jnp.dot(dlt * dlt, seg_ref[...], precision=HI, preferred_element_type=jnp.float32) * inv
    yn = dlt * lax.rsqrt(var + GN_EPS) * gng_ref[...] + gnb_ref[...]
    r_out = (yn + bonus_ref[...]) * g_ref[...]
    h = (x_ref[...] + jnp.dot(r_out.astype(jnp.bfloat16), wr_ref[...], preferred_element_type=jnp.float32)
         + jnp.dot(a_ref[...].astype(jnp.bfloat16), wa_ref[...], preferred_element_type=jnp.float32))
    h_o[...] = h
    xn_o[...] = _rms_norm(h, nf_ref[...]).astype(jnp.bfloat16)


def merge(y, bonus, g, gn_g, gn_b, a_out, x, w_out_bf16, norm_ffn, tt):
    n, d = x.shape
    head = jnp.arange(W_RWKV) // D_HEAD
    seg = (head[:, None] == head[None, :]).astype(jnp.float32)
    row = lambda w: pl.BlockSpec((tt, w), lambda i: (i, 0))
    vec = lambda w: pl.BlockSpec((1, w), lambda i: (0, 0))
    mat = lambda r, c: pl.BlockSpec((r, c), lambda i: (0, 0))
    return pl.pallas_call(
        _merge_kernel,
        grid=(n // tt,),
        in_specs=[row(W_RWKV), row(W_RWKV), row(W_RWKV), vec(W_RWKV), vec(W_RWKV), mat(W_RWKV, W_RWKV),
                  row(W_ATT), row(d), mat(W_RWKV, d), mat(W_ATT, d), vec(d)],
        out_specs=[row(d), row(d)],
        out_shape=[jax.ShapeDtypeStruct((n, d), jnp.float32), jax.ShapeDtypeStruct((n, d), jnp.bfloat16)],
        compiler_params=pltpu.CompilerParams(dimension_semantics=("parallel",), vmem_limit_bytes=VMEM_LIMIT),
        name="merge",
    )(y, bonus, g, gn_g.reshape(1, -1), gn_b.reshape(1, -1), seg, a_out, x, w_out_bf16[:W_RWKV], w_out_bf16[W_RWKV:],
      norm_ffn.reshape(1, d))


def _moba_block_kernel(q_ref, k_ref, v_ref, o_ref, *, qi):
    scale = D_HEAD ** -0.5
    nb = qi + 1
    k_all = k_ref[0]
    kb = k_all.astype(jnp.bfloat16)
    vt = v_ref[0].T.astype(jnp.bfloat16)
    q2 = q_ref[0]
    lane = lax.broadcasted_iota(jnp.int32, q2.shape, 1)
    key_row = lax.broadcasted_iota(jnp.int32, (MOBA_BLOCK, MOBA_BLOCK), 0)
    qry_col = lax.broadcasted_iota(jnp.int32, (MOBA_BLOCK, MOBA_BLOCK), 1)
    row = lax.broadcasted_iota(jnp.int32, (LANES, MOBA_BLOCK), 0)
    blk = lambda z, j: z[j * MOBA_BLOCK:(j + 1) * MOBA_BLOCK]
    outs = []
    for s in range(2):
        qm = jnp.where((lane // D_HEAD) == s, q2, 0.0)
        qmb = qm.astype(jnp.bfloat16)
        picked = [None] * qi
        if qi > MOBA_TOPK:
            kmean = jnp.concatenate([jnp.mean(blk(k_all, j), axis=0, keepdims=True) for j in range(qi)], axis=0)
            gate = lax.dot_general(kmean, qm, NT_DIMS, precision=HI, preferred_element_type=jnp.float32)
            for j in range(qi):
                ahead = jnp.zeros((1, MOBA_BLOCK), jnp.float32)
                for jp in range(qi):
                    if jp != j:
                        beats = (gate[jp:jp + 1] >= gate[j:j + 1]) if jp < j else (gate[jp:jp + 1] > gate[j:j + 1])
                        ahead = ahead + jnp.where(beats, 1.0, 0.0)
                picked[j] = ahead < float(MOBA_TOPK)
        scores = []
        for j in range(nb):
            st = lax.dot_general(blk(kb, j), qmb, NT_DIMS, preferred_element_type=jnp.float32) * scale
            if j == qi:
                st = jnp.where(key_row <= qry_col, st, NEG_INF)
            elif picked[j] is not None:
                st = jnp.where(picked[j], st, NEG_INF)
            scores.append(st)
        m = functools.reduce(jnp.maximum, [jnp.max(st, axis=0, keepdims=True) for st in scores])
        ps = [jnp.exp(st - m) for st in scores]
        denom = functools.reduce(jnp.add, [jnp.sum(p, axis=0, keepdims=True) for p in ps])
        acc = functools.reduce(jnp.add, [
            jnp.dot(vt[:, j * MOBA_BLOCK:(j + 1) * MOBA_BLOCK], ps[j].astype(jnp.bfloat16),
                    preferred_element_type=jnp.float32) for j in range(nb)])
        outs.append(acc / denom)
    o_ref[0] = jnp.where((row // D_HEAD) == 0, outs[0], outs[1]).T


def moba_prompt(q, k, v):
    b, t, w = q.shape
    outs = []
    for qi in range(t // MOBA_BLOCK):
        n_keys = (qi + 1) * MOBA_BLOCK
        outs.append(pl.pallas_call(
            functools.partial(_moba_block_kernel, qi=qi),
            grid=(b, w // LANES),
            in_specs=[
                pl.BlockSpec((1, MOBA_BLOCK, LANES), lambda bi, hp, qi=qi: (bi, qi, hp)),
                pl.BlockSpec((1, n_keys, LANES), lambda bi, hp: (bi, 0, hp)),
                pl.BlockSpec((1, n_keys, LANES), lambda bi, hp: (bi, 0, hp)),
            ],
            out_specs=pl.BlockSpec((1, MOBA_BLOCK, LANES), lambda bi, hp: (bi, 0, hp)),
            out_shape=jax.ShapeDtypeStruct((b, MOBA_BLOCK, w), jnp.float32),
            compiler_params=pltpu.CompilerParams(dimension_semantics=("parallel", "parallel"),
                                                 vmem_limit_bytes=VMEM_LIMIT),
            name=f"moba_prompt_q{qi}",
        )(q, k, v))
    return jnp.concatenate(outs, axis=1)


PAGES_PER_BLOCK = MOBA_BLOCK // PAGE_SIZE


def _sample_select_kernel(q_ref, km_ref, idx_ref):
    gate = jnp.sum(km_ref[0] * q_ref[...], axis=-1, keepdims=True)
    nb = gate.shape[0]
    blocks = lax.broadcasted_iota(jnp.int32, gate.shape, 0).astype(jnp.float32)
    for i in range(MOBA_TOPK):
        m = jnp.max(gate, axis=0, keepdims=True)
        first = jnp.min(jnp.where(gate == m, blocks, float(nb)), axis=0, keepdims=True)
        gate = jnp.where(blocks == first, NEG_INF, gate)
        idx_ref[0, i] = first[0]


def sample_select(q, kmean):
    bd, nb, h, dh = kmean.shape
    idx = pl.pallas_call(
        _sample_select_kernel,
        grid=(bd,),
        in_specs=[pl.BlockSpec((1, h, dh), lambda b: (b, 0, 0)), pl.BlockSpec((1, nb, h, dh), lambda b: (b, 0, 0, 0))],
        out_specs=pl.BlockSpec((1, MOBA_TOPK, h, 1), lambda b: (b, 0, 0, 0)),
        out_shape=jax.ShapeDtypeStruct((bd, MOBA_TOPK, h, 1), jnp.float32),
        compiler_params=pltpu.CompilerParams(dimension_semantics=("parallel",), vmem_limit_bytes=VMEM_LIMIT),
        name="sample_select",
    )(q, kmean)
    return idx[..., 0].astype(jnp.int32).transpose(0, 2, 1)


def _sample_attend_kernel(q_ref, ko_ref, vo_ref, ks_ref, vs_ref, o_ref):
    scale = D_HEAD ** -0.5
    for h in range(H_ATT):
        qh = q_ref[0, h:h + 1, :]
        s_sel = jnp.sum(ks_ref[0, h] * qh, axis=1, keepdims=True) * scale
        s_own = jnp.sum(ko_ref[0, h:h + 1, :] * qh, axis=1, keepdims=True) * scale
        m = jnp.maximum(jnp.max(s_sel, axis=0, keepdims=True), s_own)
        p_sel = jnp.exp(s_sel - m)
        p_own = jnp.exp(s_own - m)
        denom = jnp.sum(p_sel, axis=0, keepdims=True) + p_own
        acc = jnp.sum(p_sel * vs_ref[0, h], axis=0, keepdims=True) + p_own * vo_ref[0, h:h + 1, :]
        o_ref[0, h:h + 1, :] = acc / denom


def sample_attend(q, k_own, v_own, k_sel, v_sel):
    bd, h, n_keys, dh = k_sel.shape
    tok = pl.BlockSpec((1, h, dh), lambda b: (b, 0, 0))
    sel = pl.BlockSpec((1, h, n_keys, dh), lambda b: (b, 0, 0, 0))
    return pl.pallas_call(
        _sample_attend_kernel,
        grid=(bd,),
        in_specs=[tok, tok, tok, sel, sel],
        out_specs=tok,
        out_shape=jax.ShapeDtypeStruct((bd, h, dh), jnp.float32),
        compiler_params=pltpu.CompilerParams(dimension_semantics=("parallel",), vmem_limit_bytes=VMEM_LIMIT),
        name="sample_attend",
    )(q, k_own, v_own, k_sel, v_sel)


def cached_block_kmeans(pages):
    bd, n_pages = pages.shape[:2]
    return jnp.mean(pages.reshape(bd, n_pages // PAGES_PER_BLOCK, MOBA_BLOCK, H_ATT, D_HEAD), axis=2)


def moba_sample(q, k, v, kmean, cache_k, cache_v, page_table):
    bd, s, w = q.shape
    assert s == 1 and PAST_LEN % MOBA_BLOCK == 0 and PAST_LEN // MOBA_BLOCK >= MOBA_TOPK
    heads = lambda z: z.reshape(bd, H_ATT, D_HEAD)
    sel = sample_select(heads(q), kmean)
    logical = sel[..., None] * PAGES_PER_BLOCK + jnp.arange(PAGES_PER_BLOCK)
    phys = page_table[jnp.arange(bd)[:, None, None, None], logical]
    hi = jnp.arange(H_ATT)[None, :, None, None]
    gather = lambda c: c[phys, :, hi, :].reshape(bd, H_ATT, MOBA_TOPK * MOBA_BLOCK, D_HEAD)
    out = sample_attend(heads(q), heads(k), heads(v), gather(cache_k), gather(cache_v))
    return out.reshape(bd, 1, w)


PEER_CANDS = [(a, b) for a in range(PEER_TOPK) for b in range(PEER_TOPK) if (a + 1) * (b + 1) <= PEER_TOPK]
PEER_NCAND = -(-len(PEER_CANDS) // 8) * 8


def _extract_topk(x, n_rows, k, exact):
    rows = lax.broadcasted_iota(jnp.int32, x.shape, 0).astype(jnp.float32)
    vals = []
    for _ in range(k):
        m = jnp.max(x, axis=0, keepdims=True)
        if exact:
            first = jnp.min(jnp.where(x == m, rows, float(n_rows)), axis=0, keepdims=True)
            x = jnp.where(rows == first, NEG_INF, x)
        else:
            x = jnp.where(x == m, NEG_INF, x)
        vals.append(m)
    return x, vals


def _peer_route_tables(q_ref, keys_ref, t1_ref, s2_ref, e1_ref, e2_ref, sv_ref, comb_ref, exact):
    half = PEER_DQ // 2
    tm = q_ref.shape[0]
    n_pad = PEER_NCAND - len(PEER_CANDS)
    ties = jnp.zeros((1, tm), jnp.float32)
    removed = lambda rem: jnp.sum(jnp.where(rem == NEG_INF, 1.0, 0.0), axis=0, keepdims=True)
    for h in range(PEER_HEADS):
        masked = []
        for p in range(2):
            qs = q_ref[:, (2 * h + p) * half:(2 * h + p + 1) * half].astype(jnp.bfloat16)
            st = lax.dot_general(keys_ref[p], qs, NT_DIMS, preferred_element_type=jnp.float32)
            rem, vals = _extract_topk(st, PEER_NKEYS, PEER_TOPK, exact)
            if not exact:
                ties = ties + (removed(rem) - float(PEER_TOPK))
            for i, v in enumerate(vals):
                sv_ref[p, i:i + 1, :] = v
            masked.append(jnp.where(rem == NEG_INF, st, NEG_INF))
        comb_ref[...] = jnp.full(comb_ref.shape, NEG_INF, jnp.float32)
        for c, (a, b) in enumerate(PEER_CANDS):
            comb_ref[c:c + 1, :] = sv_ref[0, a:a + 1, :] + sv_ref[1, b:b + 1, :]
        rem, cvals = _extract_topk(comb_ref[...], PEER_NCAND, PEER_TOPK + 1, exact)
        if not exact:
            ties = ties + (removed(rem) - float(PEER_TOPK + 1 + n_pad))
        cmax = cvals[0]
        z = jnp.zeros_like(cmax)
        for v in cvals[:PEER_TOPK]:
            z = z + jnp.exp(v - cmax)
        cut = 0.5 * (cvals[PEER_TOPK - 1] + cvals[PEER_TOPK])
        t1_ref[h] = cut - masked[0]
        s2_ref[h] = masked[1]
        e1_ref[h] = jnp.exp(masked[0] - sv_ref[0, 0:1, :])
        e2_ref[h] = jnp.exp(masked[1] - sv_ref[1, 0:1, :]) / z
    return ties


def _peer_route_kernel(xn_ref, wpq_ref, keys_ref, t1_ref, s2_ref, e1_ref, e2_ref, q_ref, sv_ref, comb_ref):
    q_ref[...] = jnp.dot(xn_ref[...], wpq_ref[...], preferred_element_type=jnp.float32)
    tables = functools.partial(_peer_route_tables, q_ref, keys_ref, t1_ref, s2_ref, e1_ref, e2_ref, sv_ref, comb_ref)
    ties = tables(exact=False)

    @pl.when(jnp.max(ties) > 0.0)
    def _():
        tables(exact=True)


def peer_route(xn_bf16, wpq_bf16, keys_bf16, tm):
    n = xn_bf16.shape[0]
    tab = jax.ShapeDtypeStruct((PEER_HEADS, PEER_NKEYS, n), jnp.float32)
    tab_spec = pl.BlockSpec((PEER_HEADS, PEER_NKEYS, tm), lambda i: (0, 0, i))
    return pl.pallas_call(
        _peer_route_kernel,
        grid=(n // tm,),
        in_specs=[
            pl.BlockSpec((tm, D_MODEL), lambda i: (i, 0)),
            pl.BlockSpec((D_MODEL, PEER_HEADS * PEER_DQ), lambda i: (0, 0)),
            pl.BlockSpec((2, PEER_NKEYS, PEER_DQ // 2), lambda i: (0, 0, 0)),
        ],
        out_specs=[tab_spec] * 4,
        out_shape=[tab] * 4,
        scratch_shapes=[pltpu.VMEM((tm, PEER_HEADS * PEER_DQ), jnp.float32),
                        pltpu.VMEM((2, PEER_TOPK, tm), jnp.float32),
                        pltpu.VMEM((PEER_NCAND, tm), jnp.float32)],
        compiler_params=pltpu.CompilerParams(dimension_semantics=("parallel",),
                                             vmem_limit_bytes=VMEM_LIMIT),
        cost_estimate=pl.CostEstimate(
            flops=2 * n * PEER_HEADS * PEER_DQ * (D_MODEL + PEER_NKEYS) + 6 * n * PEER_HEADS * PEER_NKEYS * PEER_TOPK,
            transcendentals=2 * n * PEER_HEADS * PEER_NKEYS,
            bytes_accessed=2 * n * D_MODEL + 2 * D_MODEL * PEER_HEADS * PEER_DQ + 16 * n * PEER_HEADS * PEER_NKEYS),
        name="peer_route",
    )(xn_bf16, wpq_bf16, keys_bf16)


def _gelu_exact(x):
    return 0.5 * x * (1.0 + lax.erf(x * (1.0 / math.sqrt(2.0))))


def _peer_dense_kernel(xn_ref, u_ref, vt_ref, t1_ref, s2_ref, e1_ref, e2_ref, h_ref, nfin_ref, o_ref, acc_ref, p_ref):
    j = pl.program_id(1)
    n_e1 = u_ref.shape[0] // PEER_NKEYS
    tm = xn_ref.shape[0]

    @pl.when(j == 0)
    def _():
        acc_ref[...] = jnp.zeros_like(acc_ref)

    unit_rows = lambda eb: slice(eb * PEER_ELB * PEER_NKEYS, (eb + 1) * PEER_ELB * PEER_NKEYS)
    ats = [lax.dot_general(u_ref[unit_rows(eb), :], xn_ref[...], NT_DIMS, preferred_element_type=jnp.float32)
           for eb in range(n_e1 // PEER_ELB)]
    for eb in range(n_e1 // PEER_ELB):
        erows = unit_rows(eb)
        at = ats[eb]
        for c in range(tm // LANES):
            cols = slice(c * LANES, (c + 1) * LANES)
            for r0 in range(0, PEER_NKEYS, PEER_SGR):
                accs = [jnp.zeros((PEER_SGR, LANES), jnp.float32) for _ in range(PEER_ELB)]
                for h in range(PEER_HEADS):
                    s2t = s2_ref[h, r0:r0 + PEER_SGR, cols]
                    e2t = e2_ref[h, r0:r0 + PEER_SGR, cols]
                    for i in range(PEER_ELB):
                        el = eb * PEER_ELB + i
                        picked = s2t >= t1_ref[h, el:el + 1, cols]
                        accs[i] = accs[i] + jnp.where(picked, e2t, 0.0) * e1_ref[h, el:el + 1, cols]
                for i in range(PEER_ELB):
                    lo = i * PEER_NKEYS + r0
                    act = _gelu_exact(at[lo:lo + PEER_SGR, cols])
                    p_ref[eb, lo:lo + PEER_SGR, cols] = (accs[i] * act).astype(jnp.bfloat16)
        acc_ref[...] += jnp.dot(vt_ref[:, erows], p_ref[eb], preferred_element_type=jnp.float32)

    @pl.when(j == pl.num_programs(1) - 1)
    def _():
        o_ref[...] = _rms_norm(h_ref[...] + acc_ref[...].T, nfin_ref[...])


def peer_dense(xn_bf16, u_bf16, vt_bf16, t1, s2, e1, e2, h, norm_final, tm, te):
    n = xn_bf16.shape[0]
    n_e1 = te // PEER_NKEYS
    return pl.pallas_call(
        _peer_dense_kernel,
        grid=(n // tm, PEER_EXPERTS // te),
        in_specs=[
            pl.BlockSpec((tm, D_MODEL), lambda i, j: (i, 0)),
            pl.BlockSpec((te, D_MODEL), lambda i, j: (j, 0)),
            pl.BlockSpec((D_MODEL, te), lambda i, j: (0, j)),
            pl.BlockSpec((PEER_HEADS, n_e1, tm), lambda i, j: (0, j, i)),
            pl.BlockSpec((PEER_HEADS, PEER_NKEYS, tm), lambda i, j: (0, 0, i)),
            pl.BlockSpec((PEER_HEADS, n_e1, tm), lambda i, j: (0, j, i)),
            pl.BlockSpec((PEER_HEADS, PEER_NKEYS, tm), lambda i, j: (0, 0, i)),
            pl.BlockSpec((tm, D_MODEL), lambda i, j: (i, 0)),
            pl.BlockSpec((1, D_MODEL), lambda i, j: (0, 0)),
        ],
        out_specs=pl.BlockSpec((tm, D_MODEL), lambda i, j: (i, 0)),
        out_shape=jax.ShapeDtypeStruct((n, D_MODEL), jnp.float32),
        scratch_shapes=[pltpu.VMEM((D_MODEL, tm), jnp.float32),
                        pltpu.VMEM((n_e1 // PEER_ELB, PEER_ELB * PEER_NKEYS, tm), jnp.bfloat16)],
        compiler_params=pltpu.CompilerParams(dimension_semantics=("parallel", "arbitrary"),
                                             vmem_limit_bytes=VMEM_LIMIT),
        cost_estimate=pl.CostEstimate(
            flops=(4 * D_MODEL + 40) * n * PEER_EXPERTS, transcendentals=n * PEER_EXPERTS,
            bytes_accessed=(n // tm) * 4 * PEER_EXPERTS * D_MODEL + n * (10 * D_MODEL + 16 * PEER_HEADS * PEER_NKEYS)),
        name="peer_dense",
    )(xn_bf16, u_bf16, vt_bf16, t1, s2, e1, e2, h, norm_final.reshape(1, D_MODEL))


def peer_block(h, xn_bf16, wpq_bf16, keys_bf16, u_bf16, vt_bf16, norm_final):
    n = h.shape[0]
    tm = PEER_TM if n >= PEER_TM else PEER_TM_SMALL
    pad = -n % tm
    hp, xb = jnp.pad(h, ((0, pad), (0, 0))), jnp.pad(xn_bf16, ((0, pad), (0, 0)))
    t1, s2, e1, e2 = peer_route(xb, wpq_bf16, keys_bf16, PEER_TM_SMALL)
    return peer_dense(xb, u_bf16, vt_bf16, t1, s2, e1, e2, hp, norm_final, tm, PEER_TE)[:n]


def layer(x, pos, prev0, s0, attend, norm_mix, w_in_bf16, rw, w_out_bf16, norm_ffn, peer, norm_final):
    mu, w0, w_dec, a0, w_aaa, w_gate, k_k, k_a, r_k, gn_g, gn_b = rw
    b, t, d = x.shape
    n = b * t
    if t == 1:
        xr, posr = x.reshape(1, b, d), jnp.broadcast_to(pos, (b,))
    else:
        xr, posr = x, pos
    seq = lambda z: z.reshape(b, t, z.shape[-1])
    pr, q, k, v = map(seq, project(xr, norm_mix, w_in_bf16, posr, min(xr.shape[1], PROJ_TT)))
    r, w, k2, kk, bb, vv, g, bonus = rwkv_prep(pr, prev0, mu, w0, w_dec, a0, w_aaa, w_gate, k_k, k_a, r_k, min(t, 512))
    y, s_new = rwkv_scan(r, w, k2, kk, bb, vv, s0, RWKV_NB, min(t, 128))
    a_out = attend(q, k, v)
    flat = lambda z: z.reshape(n, z.shape[-1])
    h, xn = merge(flat(y), flat(bonus), flat(g), gn_g, gn_b, flat(a_out), flat(x), w_out_bf16, norm_ffn, min(n, 512))
    out = peer_block(h, xn, *peer, norm_final)
    heads = lambda z: z.reshape(b, t, H_ATT, D_HEAD)
    return out.reshape(b, t, d), heads(k), heads(v), s_new, pr[:, -1]


def kernel(x_prompt, x_sample, cache_k, cache_v, page_table, state_wkv, state_shift, norm_mix, w_in, mu_shift, w0, w_decay_up, a0, w_aaa_up, w_gate_up, k_k, k_a, r_k, gn_gain, gn_bias, w_out, norm_ffn, w_pq, peer_sub_keys, expert_u, expert_v, norm_final):
    l = 0
    bf16 = jnp.bfloat16
    rw = (mu_shift[l], w0[l], w_decay_up[l], a0[l], w_aaa_up[l], w_gate_up[l], k_k[l], k_a[l], r_k[l], gn_gain[l], gn_bias[l])
    peer = (w_pq[l].astype(bf16), peer_sub_keys[l].astype(bf16), expert_u[l].astype(bf16), expert_v[l].T.astype(bf16))
    shared = (norm_mix[l], w_in[l].astype(bf16), rw, w_out[l].astype(bf16), norm_ffn[l], peer, norm_final)
    bp, tp, _ = x_prompt.shape
    bs, ts, _ = x_sample.shape
    key_pages = cache_k[l][page_table]
    y_p, k_p, v_p, s_p, sh_p = layer(
        x_prompt, jnp.arange(tp, dtype=jnp.int32), jnp.zeros((bp, W_RWKV_IN), jnp.float32),
        jnp.zeros((bp, H_RWKV, D_HEAD, D_HEAD), jnp.float32), moba_prompt, *shared)
    key_pages, y_p = lax.optimization_barrier((key_pages, y_p))
    kmean = cached_block_kmeans(key_pages)
    attend_cache = lambda q, k, v: moba_sample(q, k, v, kmean, cache_k[l], cache_v[l], page_table)
    y_s, k_s, v_s, s_s, sh_s = layer(
        x_sample, PAST_LEN + jnp.arange(ts, dtype=jnp.int32), state_shift[l], state_wkv[l], attend_cache, *shared)
    return (y_p, y_s, k_p[None], v_p[None], s_p[None], sh_p[None], k_s[None], v_s[None], s_s[None], sh_s[None])
```

```python
import functools
import math

import jax
import jax.numpy as jnp
from jax import lax
from jax.experimental import pallas as pl
from jax.experimental.pallas import tpu as pltpu

D_MODEL = 1024
DEPTH = 1
PAST_LEN = 16384
PAGE_SIZE = 128
D_HEAD = 64
H_RWKV = 8
H_ATT = 8
W_RWKV = H_RWKV * D_HEAD
W_ATT = H_ATT * D_HEAD
D_DECAY_LORA = 64
D_AAA_LORA = 64
D_GATE_LORA = 128
W_RWKV_IN = 3 * W_RWKV + D_DECAY_LORA + D_AAA_LORA + D_GATE_LORA
W_IN = W_RWKV_IN + 3 * W_ATT
GN_EPS = 64e-5
NORM_EPS = 1e-6
MOBA_BLOCK = 256
MOBA_TOPK = 3
Q_CHUNK = 64
ROT_DIM = D_HEAD // 4
ROPE_THETA = 500000.0
PEER_HEADS = 8
PEER_NKEYS = 128
PEER_TOPK = 16
PEER_DQ = 256
PEER_EXPERTS = PEER_NKEYS * PEER_NKEYS
PEER_TM = 512
PEER_TM_SMALL = 256
PEER_TE = 2048
PEER_ELB = 4
PEER_SGR = 32
NEG_INF = float("-inf")
VMEM_LIMIT = 56 * 1024 * 1024
LANES = 128
SUBLANES = 8
PROJ_TT = 512
RWKV_NB = 8
NT_DIMS = (((1,), (1,)), ((), ()))


def _rms_norm(x, g):
    return x * lax.rsqrt(jnp.mean(x * x, axis=-1, keepdims=True) + NORM_EPS) * g


def rope_tables(pos):
    half = ROT_DIM // 2
    inv = 1.0 / (ROPE_THETA ** (jnp.arange(half, dtype=jnp.float32) * 2.0 / ROT_DIM))
    ang = pos.astype(jnp.float32)[:, None] * inv[None, :]
    lane = jnp.arange(LANES) % D_HEAD
    idx = lane % half
    cos = jnp.where(lane[None, :] < ROT_DIM, jnp.cos(ang)[:, idx], 1.0)
    sin = jnp.sin(ang)[:, idx]
    sin_lo = jnp.where(lane[None, :] < half, -sin, 0.0)
    sin_hi = jnp.where((lane[None, :] >= half) & (lane[None, :] < ROT_DIM), sin, 0.0)
    return cos, sin_lo, sin_hi


def _project_kernel(x_ref, g_ref, w_ref, cos_ref, slo_ref, shi_ref, pr_o, q_o, k_o, v_o):
    xn = _rms_norm(x_ref[0], g_ref[...])
    p = jnp.dot(xn.astype(jnp.bfloat16), w_ref[...], preferred_element_type=jnp.float32)
    pr_o[0] = p[:, :W_RWKV_IN]
    reps = W_ATT // LANES
    cos = jnp.concatenate([cos_ref[...]] * reps, axis=1)
    slo = jnp.concatenate([slo_ref[...]] * reps, axis=1)
    shi = jnp.concatenate([shi_ref[...]] * reps, axis=1)
    half = ROT_DIM // 2

    def rope(z):
        ahead = pltpu.roll(z, W_ATT - half, 1)
        behind = pltpu.roll(z, half, 1)
        return z * cos + ahead * slo + behind * shi

    q_o[0] = rope(p[:, W_RWKV_IN:W_RWKV_IN + W_ATT])
    k_o[0] = rope(p[:, W_RWKV_IN + W_ATT:W_RWKV_IN + 2 * W_ATT])
    v_o[0] = p[:, W_RWKV_IN + 2 * W_ATT:]


def project(x, norm_g, w_in_bf16, pos, tt):
    b, t, d = x.shape
    cos, slo, shi = rope_tables(pos)
    f32 = jnp.float32
    tab_spec = pl.BlockSpec((tt, LANES), lambda bi, ti: (ti, 0))
    out_spec = lambda w: pl.BlockSpec((1, tt, w), lambda bi, ti: (bi, ti, 0))
    return pl.pallas_call(
        _project_kernel,
        grid=(b, t // tt),
        in_specs=[
            pl.BlockSpec((1, tt, d), lambda bi, ti: (bi, ti, 0)),
            pl.BlockSpec((1, d), lambda bi, ti: (0, 0)),
            pl.BlockSpec((d, W_IN), lambda bi, ti: (0, 0)),
            tab_spec, tab_spec, tab_spec,
        ],
        out_specs=[out_spec(W_RWKV_IN), out_spec(W_ATT), out_spec(W_ATT), out_spec(W_ATT)],
        out_shape=[jax.ShapeDtypeStruct((b, t, W_RWKV_IN), f32)] + [jax.ShapeDtypeStruct((b, t, W_ATT), f32)] * 3,
        compiler_params=pltpu.CompilerParams(dimension_semantics=("parallel", "parallel"),
                                             vmem_limit_bytes=VMEM_LIMIT),
        cost_estimate=pl.CostEstimate(flops=2 * b * t * d * W_IN, transcendentals=b * t,
                                      bytes_accessed=4 * b * t * (d + W_IN) + 2 * d * W_IN + 12 * t * LANES),
        name="project",
    )(x, norm_g.reshape(1, d), w_in_bf16, cos, slo, shi)


N_PAIR = W_RWKV // LANES
HI = lax.Precision.HIGHEST


def _sigmoid(x):
    return 1.0 / (1.0 + jnp.exp(-x))


def _softplus(x):
    return jnp.maximum(x, 0.0) + jnp.log(1.0 + jnp.exp(-jnp.abs(x)))


def _rwkv_prep_kernel(p_ref, prev0_ref, mu_ref, w0_ref, wdec_ref, a0_ref, waaa_ref, wgate_ref, kk_w_ref, ka_ref,
                      rk_ref, seg_ref, r_o, w_o, k_o, kk_o, b_o, v_o, g_o, bonus_o, carry_ref):
    ti = pl.program_id(1)

    @pl.when(ti == 0)
    def _():
        carry_ref[...] = prev0_ref[0]

    p = p_ref[0]
    row = lax.broadcasted_iota(jnp.int32, p.shape, 0)
    if p.shape[0] == 1:
        prev = carry_ref[...]
    else:
        prev = jnp.where(row == 0, carry_ref[...], pltpu.roll(p, 1, 0))
    carry_ref[...] = p[p.shape[0] - 1:, :]
    ps = p + (prev - p) * mu_ref[...]
    r = ps[:, 0:W_RWKV]
    k = ps[:, W_RWKV:2 * W_RWKV]
    v = ps[:, 2 * W_RWKV:3 * W_RWKV]
    lora = ps[:, 3 * W_RWKV:3 * W_RWKV + LANES]
    gd = ps[:, 3 * W_RWKV + LANES:]
    dec = jnp.dot(jnp.tanh(lora).astype(jnp.bfloat16), wdec_ref[...], preferred_element_type=jnp.float32)
    logw = -jnp.exp(-_softplus(-(w0_ref[...] + dec)) - 0.5)
    a = _sigmoid(a0_ref[...] + jnp.dot(lora.astype(jnp.bfloat16), waaa_ref[...], preferred_element_type=jnp.float32))
    g = jnp.dot(_sigmoid(gd).astype(jnp.bfloat16), wgate_ref[...], preferred_element_type=jnp.float32)
    kk = k * kk_w_ref[...]
    sumsq = jnp.dot(kk * kk, seg_ref[...], precision=HI, preferred_element_type=jnp.float32)
    kk = kk / jnp.maximum(jnp.sqrt(sumsq), 1e-12)
    k2 = k * (1.0 + (a - 1.0) * ka_ref[...])
    rkk = jnp.dot(r * k2 * rk_ref[...], seg_ref[...], precision=HI, preferred_element_type=jnp.float32)
    r_o[0] = r
    w_o[0] = jnp.exp(logw)
    k_o[0] = k2
    kk_o[0] = kk
    b_o[0] = kk * a
    v_o[0] = v
    g_o[0] = g
    bonus_o[0] = rkk * v


def rwkv_prep(p, prev0, mu, w0, w_dec, a0, w_aaa, w_gate, k_k, k_a, r_k, tt):
    b, t, _ = p.shape
    f32 = jnp.float32
    wdec_pad = jnp.concatenate([w_dec, jnp.zeros_like(w_aaa)], axis=0).astype(jnp.bfloat16)
    waaa_pad = jnp.concatenate([jnp.zeros_like(w_dec), w_aaa], axis=0).astype(jnp.bfloat16)
    head = jnp.arange(W_RWKV) // D_HEAD
    seg = (head[:, None] == head[None, :]).astype(f32)
    row = lambda z: z.reshape(1, -1).astype(f32)
    vec_spec = lambda n: pl.BlockSpec((1, n), lambda bi, ti: (0, 0))
    mat_spec = lambda m, n: pl.BlockSpec((m, n), lambda bi, ti: (0, 0))
    out_spec = pl.BlockSpec((1, tt, W_RWKV), lambda bi, ti: (bi, ti, 0))
    out = jax.ShapeDtypeStruct((b, t, W_RWKV), f32)
    return pl.pallas_call(
        _rwkv_prep_kernel,
        grid=(b, t // tt),
        in_specs=[
            pl.BlockSpec((1, tt, W_RWKV_IN), lambda bi, ti: (bi, ti, 0)),
            pl.BlockSpec((1, 1, W_RWKV_IN), lambda bi, ti: (bi, 0, 0)),
            vec_spec(W_RWKV_IN), vec_spec(W_RWKV), mat_spec(LANES, W_RWKV), vec_spec(W_RWKV),
            mat_spec(LANES, W_RWKV), mat_spec(D_GATE_LORA, W_RWKV), vec_spec(W_RWKV), vec_spec(W_RWKV),
            vec_spec(W_RWKV), mat_spec(W_RWKV, W_RWKV),
        ],
        out_specs=[out_spec] * 8,
        out_shape=[out] * 8,
        scratch_shapes=[pltpu.VMEM((1, W_RWKV_IN), f32)],
        compiler_params=pltpu.CompilerParams(dimension_semantics=("parallel", "arbitrary"),
                                             vmem_limit_bytes=VMEM_LIMIT),
        name="rwkv_prep",
    )(p, prev0.reshape(b, 1, W_RWKV_IN), row(mu), row(w0), wdec_pad, row(a0), waaa_pad, w_gate.astype(jnp.bfloat16),
      row(k_k), row(k_a), row(r_k), seg)


def _rwkv_scan_kernel(r_ref, w_ref, k_ref, kk_ref, b_ref, v_ref, s0_ref, y_ref, sT_ref, s_ref):
    ci = pl.program_id(1)
    nb, tc = r_ref.shape[0], r_ref.shape[1]
    tiles = [(bi, p) for bi in range(nb) for p in range(N_PAIR)]

    @pl.when(ci == 0)
    def _():
        for i, (bi, p) in enumerate(tiles):
            s_ref[i] = s0_ref[bi, p]

    lane = lax.broadcasted_iota(jnp.int32, (D_HEAD, LANES), 1)
    sub = lax.broadcasted_iota(jnp.int32, (D_HEAD, LANES), 0)
    first = lane < D_HEAD
    eye2 = (sub == (lane % D_HEAD))
    sub128 = lax.broadcasted_iota(jnp.int32, (LANES, LANES), 0)
    lane128 = lax.broadcasted_iota(jnp.int32, (LANES, LANES), 1)
    eye128 = sub128 == lane128
    seg = jnp.where((sub128 // D_HEAD) == (lane128 // D_HEAD), 1.0, 0.0).astype(jnp.bfloat16)

    def head_sums(xs):
        parts = []
        for x in xs:
            hi = x.astype(jnp.bfloat16)
            parts += [hi, (x - hi.astype(jnp.float32)).astype(jnp.bfloat16)]
        both = jnp.dot(jnp.concatenate(parts, axis=0), seg, preferred_element_type=jnp.float32)
        return [both[i * LANES:i * LANES + D_HEAD] + both[i * LANES + D_HEAD:(i + 1) * LANES] for i in range(len(xs))]

    grp = min(tc, SUBLANES)

    def steps(gi, carry):
        rows = pl.ds(pl.multiple_of(gi * grp, grp), grp)
        lanes = lambda p: slice(p * LANES, (p + 1) * LANES)
        load = lambda ref: [ref[bi, rows, lanes(p)] for bi, p in tiles]
        r_g, w_g, k_g, kk_g, b_g, v_g = load(r_ref), load(w_ref), load(k_ref), load(kk_ref), load(b_ref), load(v_ref)
        states = [s_ref[i] for i in range(len(tiles))]
        yrows = [[] for _ in tiles]
        nt = len(tiles)
        sks = head_sums([s * kk[0:1] for s, kk in zip(states, kk_g)])
        for j in range(grp):
            row = slice(j, j + 1)
            for i in range(len(tiles)):
                vcol = jnp.sum(jnp.where(eye128, v_g[i][row], 0.0), axis=1, keepdims=True)
                vmat = jnp.where(first, vcol[0:D_HEAD], vcol[D_HEAD:LANES])
                states[i] = states[i] * w_g[i][row] - sks[i] * b_g[i][row] + vmat * k_g[i][row]
            prods = [s * r[row] for s, r in zip(states, r_g)]
            if j + 1 < grp:
                prods += [s * kk[j + 1:j + 2] for s, kk in zip(states, kk_g)]
            sums = head_sums(prods)
            ys, sks = sums[:nt], sums[nt:]
            for i in range(len(tiles)):
                yrows[i].append(jnp.sum(jnp.where(eye2, ys[i], 0.0), axis=0, keepdims=True))
        for i, (bi, p) in enumerate(tiles):
            y_ref[bi, rows, lanes(p)] = yrows[i][0] if grp == 1 else jnp.concatenate(yrows[i], axis=0)
            s_ref[i] = states[i]
        return carry

    lax.fori_loop(0, tc // grp, steps, 0)

    @pl.when(ci == pl.num_programs(1) - 1)
    def _():
        for i, (bi, p) in enumerate(tiles):
            sT_ref[bi, p] = s_ref[i]


def rwkv_scan(r, w, k, kk, bb, v, s0, nb, tc):
    b, t, _ = r.shape
    f32 = jnp.float32
    pair = lambda s: s.reshape(b, N_PAIR, 2, D_HEAD, D_HEAD).transpose(0, 1, 3, 2, 4).reshape(b, N_PAIR, D_HEAD, LANES)
    unpair = lambda s: s.reshape(b, N_PAIR, D_HEAD, 2, D_HEAD).transpose(0, 1, 3, 2, 4).reshape(b, H_RWKV, D_HEAD, D_HEAD)
    seq_spec = pl.BlockSpec((nb, tc, W_RWKV), lambda bi, ci: (bi, ci, 0))
    st_spec = pl.BlockSpec((nb, N_PAIR, D_HEAD, LANES), lambda bi, ci: (bi, 0, 0, 0))
    y, s_fin = pl.pallas_call(
        _rwkv_scan_kernel,
        grid=(b // nb, t // tc),
        in_specs=[seq_spec] * 6 + [st_spec],
        out_specs=[seq_spec, st_spec],
        out_shape=[jax.ShapeDtypeStruct((b, t, W_RWKV), f32), jax.ShapeDtypeStruct((b, N_PAIR, D_HEAD, LANES), f32)],
        scratch_shapes=[pltpu.VMEM((nb * N_PAIR, D_HEAD, LANES), f32)],
        compiler_params=pltpu.CompilerParams(dimension_semantics=("parallel", "arbitrary"),
                                             vmem_limit_bytes=VMEM_LIMIT),
        cost_estimate=pl.CostEstimate(flops=b * t * N_PAIR * (4 * LANES ** 3 + 20 * D_HEAD * LANES), transcendentals=0,
                                      bytes_accessed=4 * (7 * b * t * W_RWKV + 2 * b * W_RWKV * D_HEAD)),
        name="rwkv_scan",
    )(r, w, k, kk, bb, v, pair(s0.astype(f32)))
    return y, unpair(s_fin)


def _merge_kernel(y_ref, bonus_ref, g_ref, gng_ref, gnb_ref, seg_ref, a_ref, x_ref, wr_ref, wa_ref, nf_ref, h_o, xn_o):
    y = y_ref[...]
    inv = 1.0 / D_HEAD
    mean = jnp.dot(y, seg_ref[...], precision=HI, preferred_element_type=jnp.float32) * inv
    dlt = y - mean
    var = jnp.dot(dlt * dlt, seg_ref[...], precision=HI, preferred_element_type=jnp.float32) * inv
    yn = dlt * lax.rsqrt(var + GN_EPS) * gng_ref[...] + gnb_ref[...]
    r_out = (yn + bonus_ref[...]) * g_ref[...]
    h = (x_ref[...] + jnp.dot(r_out.astype(jnp.bfloat16), wr_ref[...], preferred_element_type=jnp.float32)
         + jnp.dot(a_ref[...].astype(jnp.bfloat16), wa_ref[...], preferred_element_type=jnp.float32))
    h_o[...] = h
    xn_o[...] = _rms_norm(h, nf_ref[...]).astype(jnp.bfloat16)


def merge(y, bonus, g, gn_g, gn_b, a_out, x, w_out_bf16, norm_ffn, tt):
    n, d = x.shape
    head = jnp.arange(W_RWKV) // D_HEAD
    seg = (head[:, None] == head[None, :]).astype(jnp.float32)
    row = lambda w: pl.BlockSpec((tt, w), lambda i: (i, 0))
    vec = lambda w: pl.BlockSpec((1, w), lambda i: (0, 0))
    mat = lambda r, c: pl.BlockSpec((r, c), lambda i: (0, 0))
    return pl.pallas_call(
        _merge_kernel,
        grid=(n // tt,),
        in_specs=[row(W_RWKV), row(W_RWKV), row(W_RWKV), vec(W_RWKV), vec(W_RWKV), mat(W_RWKV, W_RWKV),
                  row(W_ATT), row(d), mat(W_RWKV, d), mat(W_ATT, d), vec(d)],
        out_specs=[row(d), row(d)],
        out_shape=[jax.ShapeDtypeStruct((n, d), jnp.float32), jax.ShapeDtypeStruct((n, d), jnp.bfloat16)],
        compiler_params=pltpu.CompilerParams(dimension_semantics=("parallel",), vmem_limit_bytes=VMEM_LIMIT),
        name="merge",
    )(y, bonus, g, gn_g.reshape(1, -1), gn_b.reshape(1, -1), seg, a_out, x, w_out_bf16[:W_RWKV], w_out_bf16[W_RWKV:],
      norm_ffn.reshape(1, d))


def _moba_block_kernel(q_ref, k_ref, v_ref, o_ref, *, qi):
    scale = D_HEAD ** -0.5
    nb = qi + 1
    k_all = k_ref[0]
    kb = k_all.astype(jnp.bfloat16)
    vt = v_ref[0].T.astype(jnp.bfloat16)
    q2 = q_ref[0]
    lane = lax.broadcasted_iota(jnp.int32, q2.shape, 1)
    key_row = lax.broadcasted_iota(jnp.int32, (MOBA_BLOCK, MOBA_BLOCK), 0)
    qry_col = lax.broadcasted_iota(jnp.int32, (MOBA_BLOCK, MOBA_BLOCK), 1)
    row = lax.broadcasted_iota(jnp.int32, (LANES, MOBA_BLOCK), 0)
    blk = lambda z, j: z[j * MOBA_BLOCK:(j + 1) * MOBA_BLOCK]
    outs = []
    for s in range(2):
        qm = jnp.where((lane // D_HEAD) == s, q2, 0.0)
        qmb = qm.astype(jnp.bfloat16)
        picked = [None] * qi
        if qi > MOBA_TOPK:
            kmean = jnp.concatenate([jnp.mean(blk(k_all, j), axis=0, keepdims=True) for j in range(qi)], axis=0)
            gate = lax.dot_general(kmean, qm, NT_DIMS, precision=HI, preferred_element_type=jnp.float32)
            for j in range(qi):
                ahead = jnp.zeros((1, MOBA_BLOCK), jnp.float32)
                for jp in range(qi):
                    if jp != j:
                        beats = (gate[jp:jp + 1] >= gate[j:j + 1]) if jp < j else (gate[jp:jp + 1] > gate[j:j + 1])
                        ahead = ahead + jnp.where(beats, 1.0, 0.0)
                picked[j] = ahead < float(MOBA_TOPK)
        scores = []
        for j in range(nb):
            st = lax.dot_general(blk(kb, j), qmb, NT_DIMS, preferred_element_type=jnp.float32) * scale
            if j == qi:
                st = jnp.where(key_row <= qry_col, st, NEG_INF)
            elif picked[j] is not None:
                st = jnp.where(picked[j], st, NEG_INF)
            scores.append(st)
        m = functools.reduce(jnp.maximum, [jnp.max(st, axis=0, keepdims=True) for st in scores])
        ps = [jnp.exp(st - m) for st in scores]
        denom = functools.reduce(jnp.add, [jnp.sum(p, axis=0, keepdims=True) for p in ps])
        acc = functools.reduce(jnp.add, [
            jnp.dot(vt[:, j * MOBA_BLOCK:(j + 1) * MOBA_BLOCK], ps[j].astype(jnp.bfloat16),
                    preferred_element_type=jnp.float32) for j in range(nb)])
        outs.append(acc / denom)
    o_ref[0] = jnp.where((row // D_HEAD) == 0, outs[0], outs[1]).T


def moba_prompt(q, k, v):
    b, t, w = q.shape
    outs = []
    for qi in range(t // MOBA_BLOCK):
        n_keys = (qi + 1) * MOBA_BLOCK
        outs.append(pl.pallas_call(
            functools.partial(_moba_block_kernel, qi=qi),
            grid=(b, w // LANES),
            in_specs=[
                pl.BlockSpec((1, MOBA_BLOCK, LANES), lambda bi, hp, qi=qi: (bi, qi, hp)),
                pl.BlockSpec((1, n_keys, LANES), lambda bi, hp: (bi, 0, hp)),
                pl.BlockSpec((1, n_keys, LANES), lambda bi, hp: (bi, 0, hp)),
            ],
            out_specs=pl.BlockSpec((1, MOBA_BLOCK, LANES), lambda bi, hp: (bi, 0, hp)),
            out_shape=jax.ShapeDtypeStruct((b, MOBA_BLOCK, w), jnp.float32),
            compiler_params=pltpu.CompilerParams(dimension_semantics=("parallel", "parallel"),
                                                 vmem_limit_bytes=VMEM_LIMIT),
            name=f"moba_prompt_q{qi}",
        )(q, k, v))
    return jnp.concatenate(outs, axis=1)


PAGES_PER_BLOCK = MOBA_BLOCK // PAGE_SIZE


def _sample_select_kernel(q_ref, km_ref, idx_ref):
    gate = jnp.sum(km_ref[0] * q_ref[...], axis=-1, keepdims=True)
    nb = gate.shape[0]
    blocks = lax.broadcasted_iota(jnp.int32, gate.shape, 0).astype(jnp.float32)
    for i in range(MOBA_TOPK):
        m = jnp.max(gate, axis=0, keepdims=True)
        first = jnp.min(jnp.where(gate == m, blocks, float(nb)), axis=0, keepdims=True)
        gate = jnp.where(blocks == first, NEG_INF, gate)
        idx_ref[0, i] = first[0]


def sample_select(q, kmean):
    bd, nb, h, dh = kmean.shape
    idx = pl.pallas_call(
        _sample_select_kernel,
        grid=(bd,),
        in_specs=[pl.BlockSpec((1, h, dh), lambda b: (b, 0, 0)), pl.BlockSpec((1, nb, h, dh), lambda b: (b, 0, 0, 0))],
        out_specs=pl.BlockSpec((1, MOBA_TOPK, h, 1), lambda b: (b, 0, 0, 0)),
        out_shape=jax.ShapeDtypeStruct((bd, MOBA_TOPK, h, 1), jnp.float32),
        compiler_params=pltpu.CompilerParams(dimension_semantics=("parallel",), vmem_limit_bytes=VMEM_LIMIT),
        name="sample_select",
    )(q, kmean)
    return idx[..., 0].astype(jnp.int32).transpose(0, 2, 1)


def _sample_attend_kernel(q_ref, ko_ref, vo_ref, ks_ref, vs_ref, o_ref):
    scale = D_HEAD ** -0.5
    for h in range(H_ATT):
        qh = q_ref[0, h:h + 1, :]
        s_sel = jnp.sum(ks_ref[0, h] * qh, axis=1, keepdims=True) * scale
        s_own = jnp.sum(ko_ref[0, h:h + 1, :] * qh, axis=1, keepdims=True) * scale
        m = jnp.maximum(jnp.max(s_sel, axis=0, keepdims=True), s_own)
        p_sel = jnp.exp(s_sel - m)
        p_own = jnp.exp(s_own - m)
        denom = jnp.sum(p_sel, axis=0, keepdims=True) + p_own
        acc = jnp.sum(p_sel * vs_ref[0, h], axis=0, keepdims=True) + p_own * vo_ref[0, h:h + 1, :]
        o_ref[0, h:h + 1, :] = acc / denom


def sample_attend(q, k_own, v_own, k_sel, v_sel):
    bd, h, n_keys, dh = k_sel.shape
    tok = pl.BlockSpec((1, h, dh), lambda b: (b, 0, 0))
    sel = pl.BlockSpec((1, h, n_keys, dh), lambda b: (b, 0, 0, 0))
    return pl.pallas_call(
        _sample_attend_kernel,
        grid=(bd,),
        in_specs=[tok, tok, tok, sel, sel],
        out_specs=tok,
        out_shape=jax.ShapeDtypeStruct((bd, h, dh), jnp.float32),
        compiler_params=pltpu.CompilerParams(dimension_semantics=("parallel",), vmem_limit_bytes=VMEM_LIMIT),
        name="sample_attend",
    )(q, k_own, v_own, k_sel, v_sel)


def cached_block_kmeans(pages):
    bd, n_pages = pages.shape[:2]
    return jnp.mean(pages.reshape(bd, n_pages // PAGES_PER_BLOCK, MOBA_BLOCK, H_ATT, D_HEAD), axis=2)


def moba_sample(q, k, v, kmean, cache_k, cache_v, page_table):
    bd, s, w = q.shape
    assert s == 1 and PAST_LEN % MOBA_BLOCK == 0 and PAST_LEN // MOBA_BLOCK >= MOBA_TOPK
    heads = lambda z: z.reshape(bd, H_ATT, D_HEAD)
    sel = sample_select(heads(q), kmean)
    logical = sel[..., None] * PAGES_PER_BLOCK + jnp.arange(PAGES_PER_BLOCK)
    phys = page_table[jnp.arange(bd)[:, None, None, None], logical]
    hi = jnp.arange(H_ATT)[None, :, None, None]
    gather = lambda c: c[phys, :, hi, :].reshape(bd, H_ATT, MOBA_TOPK * MOBA_BLOCK, D_HEAD)
    out = sample_attend(heads(q), heads(k), heads(v), gather(cache_k), gather(cache_v))
    return out.reshape(bd, 1, w)


PEER_CANDS = [(a, b) for a in range(PEER_TOPK) for b in range(PEER_TOPK) if (a + 1) * (b + 1) <= PEER_TOPK]
PEER_NCAND = -(-len(PEER_CANDS) // 8) * 8


def _extract_topk(x, n_rows, k, exact):
    rows = lax.broadcasted_iota(jnp.int32, x.shape, 0).astype(jnp.float32)
    vals = []
    for _ in range(k):
        m = jnp.max(x, axis=0, keepdims=True)
        if exact:
            first = jnp.min(jnp.where(x == m, rows, float(n_rows)), axis=0, keepdims=True)
            x = jnp.where(rows == first, NEG_INF, x)
        else:
            x = jnp.where(x == m, NEG_INF, x)
        vals.append(m)
    return x, vals


def _peer_route_tables(q_ref, keys_ref, t1_ref, s2_ref, e1_ref, e2_ref, sv_ref, comb_ref, exact):
    half = PEER_DQ // 2
    tm = q_ref.shape[0]
    n_pad = PEER_NCAND - len(PEER_CANDS)
    ties = jnp.zeros((1, tm), jnp.float32)
    removed = lambda rem: jnp.sum(jnp.where(rem == NEG_INF, 1.0, 0.0), axis=0, keepdims=True)
    for h in range(PEER_HEADS):
        masked = []
        for p in range(2):
            qs = q_ref[:, (2 * h + p) * half:(2 * h + p + 1) * half].astype(jnp.bfloat16)
            st = lax.dot_general(keys_ref[p], qs, NT_DIMS, preferred_element_type=jnp.float32)
            rem, vals = _extract_topk(st, PEER_NKEYS, PEER_TOPK, exact)
            if not exact:
                ties = ties + (removed(rem) - float(PEER_TOPK))
            for i, v in enumerate(vals):
                sv_ref[p, i:i + 1, :] = v
            masked.append(jnp.where(rem == NEG_INF, st, NEG_INF))
        comb_ref[...] = jnp.full(comb_ref.shape, NEG_INF, jnp.float32)
        for c, (a, b) in enumerate(PEER_CANDS):
            comb_ref[c:c + 1, :] = sv_ref[0, a:a + 1, :] + sv_ref[1, b:b + 1, :]
        rem, cvals = _extract_topk(comb_ref[...], PEER_NCAND, PEER_TOPK + 1, exact)
        if not exact:
            ties = ties + (removed(rem) - float(PEER_TOPK + 1 + n_pad))
        cmax = cvals[0]
        z = jnp.zeros_like(cmax)
        for v in cvals[:PEER_TOPK]:
            z = z + jnp.exp(v - cmax)
        cut = 0.5 * (cvals[PEER_TOPK - 1] + cvals[PEER_TOPK])
        t1_ref[h] = cut - masked[0]
        s2_ref[h] = masked[1]
        e1_ref[h] = jnp.exp(masked[0] - sv_ref[0, 0:1, :])
        e2_ref[h] = jnp.exp(masked[1] - sv_ref[1, 0:1, :]) / z
    return ties


def _peer_route_kernel(xn_ref, wpq_ref, keys_ref, t1_ref, s2_ref, e1_ref, e2_ref, q_ref, sv_ref, comb_ref):
    q_ref[...] = jnp.dot(xn_ref[...], wpq_ref[...], preferred_element_type=jnp.float32)
    tables = functools.partial(_peer_route_tables, q_ref, keys_ref, t1_ref, s2_ref, e1_ref, e2_ref, sv_ref, comb_ref)
    ties = tables(exact=False)

    @pl.when(jnp.max(ties) > 0.0)
    def _():
        tables(exact=True)


def peer_route(xn_bf16, wpq_bf16, keys_bf16, tm):
    n = xn_bf16.shape[0]
    tab = jax.ShapeDtypeStruct((PEER_HEADS, PEER_NKEYS, n), jnp.float32)
    tab_spec = pl.BlockSpec((PEER_HEADS, PEER_NKEYS, tm), lambda i: (0, 0, i))
    return pl.pallas_call(
        _peer_route_kernel,
        grid=(n // tm,),
        in_specs=[
            pl.BlockSpec((tm, D_MODEL), lambda i: (i, 0)),
            pl.BlockSpec((D_MODEL, PEER_HEADS * PEER_DQ), lambda i: (0, 0)),
            pl.BlockSpec((2, PEER_NKEYS, PEER_DQ // 2), lambda i: (0, 0, 0)),
        ],
        out_specs=[tab_spec] * 4,
        out_shape=[tab] * 4,
        scratch_shapes=[pltpu.VMEM((tm, PEER_HEADS * PEER_DQ), jnp.float32),
                        pltpu.VMEM((2, PEER_TOPK, tm), jnp.float32),
                        pltpu.VMEM((PEER_NCAND, tm), jnp.float32)],
        compiler_params=pltpu.CompilerParams(dimension_semantics=("parallel",),
                                             vmem_limit_bytes=VMEM_LIMIT),
        cost_estimate=pl.CostEstimate(
            flops=2 * n * PEER_HEADS * PEER_DQ * (D_MODEL + PEER_NKEYS) + 6 * n * PEER_HEADS * PEER_NKEYS * PEER_TOPK,
            transcendentals=2 * n * PEER_HEADS * PEER_NKEYS,
            bytes_accessed=2 * n * D_MODEL + 2 * D_MODEL * PEER_HEADS * PEER_DQ + 16 * n * PEER_HEADS * PEER_NKEYS),
        name="peer_route",
    )(xn_bf16, wpq_bf16, keys_bf16)


def _gelu_exact(x):
    return 0.5 * x * (1.0 + lax.erf(x * (1.0 / math.sqrt(2.0))))


def _peer_dense_kernel(xn_ref, u_ref, vt_ref, t1_ref, s2_ref, e1_ref, e2_ref, h_ref, nfin_ref, o_ref, acc_ref, p_ref):
    j = pl.program_id(1)
    n_e1 = u_ref.shape[0] // PEER_NKEYS
    tm = xn_ref.shape[0]

    @pl.when(j == 0)
    def _():
        acc_ref[...] = jnp.zeros_like(acc_ref)

    unit_rows = lambda eb: slice(eb * PEER_ELB * PEER_NKEYS, (eb + 1) * PEER_ELB * PEER_NKEYS)
    ats = [lax.dot_general(u_ref[unit_rows(eb), :], xn_ref[...], NT_DIMS, preferred_element_type=jnp.float32)
           for eb in range(n_e1 // PEER_ELB)]
    for eb in range(n_e1 // PEER_ELB):
        erows = unit_rows(eb)
        at = ats[eb]
        for c in range(tm // LANES):
            cols = slice(c * LANES, (c + 1) * LANES)
            for r0 in range(0, PEER_NKEYS, PEER_SGR):
                accs = [jnp.zeros((PEER_SGR, LANES), jnp.float32) for _ in range(PEER_ELB)]
                for h in range(PEER_HEADS):
                    s2t = s2_ref[h, r0:r0 + PEER_SGR, cols]
                    e2t = e2_ref[h, r0:r0 + PEER_SGR, cols]
                    for i in range(PEER_ELB):
                        el = eb * PEER_ELB + i
                        picked = s2t >= t1_ref[h, el:el + 1, cols]
                        accs[i] = accs[i] + jnp.where(picked, e2t, 0.0) * e1_ref[h, el:el + 1, cols]
                for i in range(PEER_ELB):
                    lo = i * PEER_NKEYS + r0
                    act = _gelu_exact(at[lo:lo + PEER_SGR, cols])
                    p_ref[eb, lo:lo + PEER_SGR, cols] = (accs[i] * act).astype(jnp.bfloat16)
        acc_ref[...] += jnp.dot(vt_ref[:, erows], p_ref[eb], preferred_element_type=jnp.float32)

    @pl.when(j == pl.num_programs(1) - 1)
    def _():
        o_ref[...] = _rms_norm(h_ref[...] + acc_ref[...].T, nfin_ref[...])


def peer_dense(xn_bf16, u_bf16, vt_bf16, t1, s2, e1, e2, h, norm_final, tm, te):
    n = xn_bf16.shape[0]
    n_e1 = te // PEER_NKEYS
    return pl.pallas_call(
        _peer_dense_kernel,
        grid=(n // tm, PEER_EXPERTS // te),
        in_specs=[
            pl.BlockSpec((tm, D_MODEL), lambda i, j: (i, 0)),
            pl.BlockSpec((te, D_MODEL), lambda i, j: (j, 0)),
            pl.BlockSpec((D_MODEL, te), lambda i, j: (0, j)),
            pl.BlockSpec((PEER_HEADS, n_e1, tm), lambda i, j: (0, j, i)),
            pl.BlockSpec((PEER_HEADS, PEER_NKEYS, tm), lambda i, j: (0, 0, i)),
            pl.BlockSpec((PEER_HEADS, n_e1, tm), lambda i, j: (0, j, i)),
            pl.BlockSpec((PEER_HEADS, PEER_NKEYS, tm), lambda i, j: (0, 0, i)),
            pl.BlockSpec((tm, D_MODEL), lambda i, j: (i, 0)),
            pl.BlockSpec((1, D_MODEL), lambda i, j: (0, 0)),
        ],
        out_specs=pl.BlockSpec((tm, D_MODEL), lambda i, j: (i, 0)),
        out_shape=jax.ShapeDtypeStruct((n, D_MODEL), jnp.float32),
        scratch_shapes=[pltpu.VMEM((D_MODEL, tm), jnp.float32),
                        pltpu.VMEM((n_e1 // PEER_ELB, PEER_ELB * PEER_NKEYS, tm), jnp.bfloat16)],
        compiler_params=pltpu.CompilerParams(dimension_semantics=("parallel", "arbitrary"),
                                             vmem_limit_bytes=VMEM_LIMIT),
        cost_estimate=pl.CostEstimate(
            flops=(4 * D_MODEL + 40) * n * PEER_EXPERTS, transcendentals=n * PEER_EXPERTS,
            bytes_accessed=(n // tm) * 4 * PEER_EXPERTS * D_MODEL + n * (10 * D_MODEL + 16 * PEER_HEADS * PEER_NKEYS)),
        name="peer_dense",
    )(xn_bf16, u_bf16, vt_bf16, t1, s2, e1, e2, h, norm_final.reshape(1, D_MODEL))


def peer_block(h, xn_bf16, wpq_bf16, keys_bf16, u_bf16, vt_bf16, norm_final):
    n = h.shape[0]
    tm = PEER_TM if n >= PEER_TM else PEER_TM_SMALL
    pad = -n % tm
    hp, xb = jnp.pad(h, ((0, pad), (0, 0))), jnp.pad(xn_bf16, ((0, pad), (0, 0)))
    t1, s2, e1, e2 = peer_route(xb, wpq_bf16, keys_bf16, PEER_TM_SMALL)
    return peer_dense(xb, u_bf16, vt_bf16, t1, s2, e1, e2, hp, norm_final, tm, PEER_TE)[:n]


def layer(x, pos, prev0, s0, attend, norm_mix, w_in_bf16, rw, w_out_bf16, norm_ffn, peer, norm_final):
    mu, w0, w_dec, a0, w_aaa, w_gate, k_k, k_a, r_k, gn_g, gn_b = rw
    b, t, d = x.shape
    n = b * t
    if t == 1:
        xr, posr = x.reshape(1, b, d), jnp.broadcast_to(pos, (b,))
    else:
        xr, posr = x, pos
    seq = lambda z: z.reshape(b, t, z.shape[-1])
    pr, q, k, v = map(seq, project(xr, norm_mix, w_in_bf16, posr, min(xr.shape[1], PROJ_TT)))
    r, w, k2, kk, bb, vv, g, bonus = rwkv_prep(pr, prev0, mu, w0, w_dec, a0, w_aaa, w_gate, k_k, k_a, r_k, min(t, 512))
    y, s_new = rwkv_scan(r, w, k2, kk, bb, vv, s0, RWKV_NB, min(t, 128))
    a_out = attend(q, k, v)
    flat = lambda z: z.reshape(n, z.shape[-1])
    h, xn = merge(flat(y), flat(bonus), flat(g), gn_g, gn_b, flat(a_out), flat(x), w_out_bf16, norm_ffn, min(n, 512))
    out = peer_block(h, xn, *peer, norm_final)
    heads = lambda z: z.reshape(b, t, H_ATT, D_HEAD)
    return out.reshape(b, t, d), heads(k), heads(v), s_new, pr[:, -1]


def kernel(x_prompt, x_sample, cache_k, cache_v, page_table, state_wkv, state_shift, norm_mix, w_in, mu_shift, w0, w_decay_up, a0, w_aaa_up, w_gate_up, k_k, k_a, r_k, gn_gain, gn_bias, w_out, norm_ffn, w_pq, peer_sub_keys, expert_u, expert_v, norm_final):
    l = 0
    bf16 = jnp.bfloat16
    rw = (mu_shift[l], w0[l], w_decay_up[l], a0[l], w_aaa_up[l], w_gate_up[l], k_k[l], k_a[l], r_k[l], gn_gain[l], gn_bias[l])
    peer = (w_pq[l].astype(bf16), peer_sub_keys[l].astype(bf16), expert_u[l].astype(bf16), expert_v[l].T.astype(bf16))
    shared = (norm_mix[l], w_in[l].astype(bf16), rw, w_out[l].astype(bf16), norm_ffn[l], peer, norm_final)
    bp, tp, _ = x_prompt.shape
    bs, ts, _ = x_sample.shape
    key_pages = cache_k[l][page_table]
    y_p, k_p, v_p, s_p, sh_p = layer(
        x_prompt, jnp.arange(tp, dtype=jnp.int32), jnp.zeros((bp, W_RWKV_IN), jnp.float32),
        jnp.zeros((bp, H_RWKV, D_HEAD, D_HEAD), jnp.float32), moba_prompt, *shared)
    key_pages, y_p = lax.optimization_barrier((key_pages, y_p))
    kmean = cached_block_kmeans(key_pages)
    attend_cache = lambda q, k, v: moba_sample(q, k, v, kmean, cache_k[l], cache_v[l], page_table)
    y_s, k_s, v_s, s_s, sh_s = layer(
        x_sample, PAST_LEN + jnp.arange(ts, dtype=jnp.int32), state_shift[l], state_wkv[l], attend_cache, *shared)
    return (y_p, y_s, k_p[None], v_p[None], s_p[None], sh_p[None], k_s[None], v_s[None], s_s[None], sh_s[None])
```
